```python
import jax
import jax.numpy as jnp
from jax import lax
import numpy as np

D_MODEL = 1024
BATCH = 8
SEQ = 2048
DEPTH = 4

CHUNK = 64
EPS = 1e-6
N_MOD = 6

D_HGRN = 512
D_S5 = 256
D_LRU = 256
D_MIX = D_HGRN + D_S5 + D_LRU

HGRN_HEADS = 4
HGRN_DK = D_HGRN // HGRN_HEADS

S5_GROUP = 16
S5_GROUPS = D_S5 // S5_GROUP
S5_STATE = 64
S5_DT_MIN = 0.001
S5_DT_MAX = 0.1

LRU_BLOCKS = 4
LRU_BLOCK = D_LRU // LRU_BLOCKS
CONV_W = 4
LRU_C = 8.0

PEER_HEADS = 8
N_KEYS = 128
N_EXPERTS = N_KEYS * N_KEYS
PEER_DQ = 256
PEER_TOPK = 16
PEER_BLOCK = 128

IN_COLS = 4 * D_HGRN + D_S5 + 2 * D_LRU
IN_SPLITS = (D_HGRN, 2 * D_HGRN, 3 * D_HGRN, 4 * D_HGRN, 4 * D_HGRN + D_S5, 4 * D_HGRN + D_S5 + D_LRU)

kernel_name = 'hybrid_hgrn2_s5_rglru_peer_adaln'


def _normalize(y):
    y32 = y.astype(jnp.float32)
    return y32 * lax.rsqrt(jnp.mean(y32 * y32, axis=-1, keepdims=True) + EPS)


def rms_norm(x, g):
    return (_normalize(x) * g.astype(jnp.float32)).astype(x.dtype)


def hgrn2_mixer(q, f_pre, i, g, lb):
    f32 = jnp.float32
    bsz, seq, _ = q.shape
    nc = seq // CHUNK
    lb = lb.astype(f32)
    qa = jax.nn.silu(q.astype(f32))
    f = lb + (1.0 - lb) * jax.nn.sigmoid(f_pre.astype(f32))
    logf = jnp.log(f)
    k = 1.0 - f
    v = i.astype(f32)

    def to_chunks(t):
        return t.reshape(bsz, nc, CHUNK, HGRN_HEADS, HGRN_DK).transpose(1, 0, 3, 2, 4)

    qc, kc, vc, gc = to_chunks(qa), to_chunks(k), to_chunks(v), to_chunks(logf)
    causal = jnp.tril(jnp.ones((CHUNK, CHUNK), dtype=bool))[None, None, :, :, None]

    def step(state, inp):
        qb, kb, vb, gb = inp
        b = jnp.cumsum(gb, axis=2)
        diff = b[:, :, :, None, :] - b[:, :, None, :, :]
        decay = jnp.exp(jnp.where(causal, diff, -jnp.inf))
        scores = jnp.einsum('bhtd,bhsd,bhtsd->bhts', qb, kb, decay)
        o = (jnp.einsum('bhts,bhsv->bhtv', scores, vb)
             + jnp.einsum('bhtd,bhdv->bhtv', qb * jnp.exp(b), state))
        b_last = b[:, :, -1:, :]
        state = (state * jnp.exp(b_last[:, :, 0, :, None])
                 + jnp.einsum('bhsd,bhsv->bhdv', kb * jnp.exp(b_last - b), vb))
        return state, o

    s0 = jnp.zeros((bsz, HGRN_HEADS, HGRN_DK, HGRN_DK), f32)
    _, o = lax.scan(step, s0, (qc, kc, vc, gc))
    o = o.transpose(1, 0, 3, 2, 4).reshape(bsz, seq, HGRN_HEADS, HGRN_DK)
    o = _normalize(o).reshape(bsz, seq, D_HGRN)
    return o * jax.nn.silu(g.astype(f32))


def s5_mixer(u, a_re, a_im, b_re, b_im, c_re, c_im, log_dt, d_skip, w_glu, b_glu):
    f32 = jnp.float32
    bsz, seq, _ = u.shape
    u32 = u.astype(f32).reshape(bsz, seq, S5_GROUPS, S5_GROUP)
    lam_re = jnp.minimum(a_re.astype(f32), -1e-4)
    lam_im = a_im.astype(f32)
    dt = jnp.exp(log_dt.astype(f32))[:, None]
    mag = jnp.exp(lam_re * dt)
    ab_re = mag * jnp.cos(lam_im * dt)
    ab_im = mag * jnp.sin(lam_im * dt)
    den = lam_re * lam_re + lam_im * lam_im
    nr = ab_re - 1.0
    z_re = (nr * lam_re + ab_im * lam_im) / den
    z_im = (ab_im * lam_re - nr * lam_im) / den
    br = b_re.astype(f32)
    bi = b_im.astype(f32)
    bb_re = z_re[..., None] * br - z_im[..., None] * bi
    bb_im = z_re[..., None] * bi + z_im[..., None] * br
    bu_re = jnp.einsum('blgc,gpc->blgp', u32, bb_re)
    bu_im = jnp.einsum('blgc,gpc->blgp', u32, bb_im)
    a_full_re = jnp.broadcast_to(ab_re, bu_re.shape)
    a_full_im = jnp.broadcast_to(ab_im, bu_im.shape)

    def combine(e1, e2):
        a1r, a1i, b1r, b1i = e1
        a2r, a2i, b2r, b2i = e2
        return (a2r * a1r - a2i * a1i,
                a2r * a1i + a2i * a1r,
                a2r * b1r - a2i * b1i + b2r,
                a2r * b1i + a2i * b1r + b2i)

    _, _, xr, xi = lax.associative_scan(combine, (a_full_re, a_full_im, bu_re, bu_im), axis=1)
    y = (jnp.einsum('blgp,gcp->blgc', xr, c_re.astype(f32))
         - jnp.einsum('blgp,gcp->blgc', xi, c_im.astype(f32)))
    y = (y + d_skip.astype(f32).reshape(S5_GROUPS, S5_GROUP) * u32).reshape(bsz, seq, D_S5)
    z = jax.nn.gelu(y)
    return z * jax.nn.sigmoid(z @ w_glu.astype(f32) + b_glu.astype(f32))


def rglru_mixer(xb, zb, conv_w, conv_b, w_a, b_a, w_x, b_x, lam):
    f32 = jnp.float32
    bsz, seq, _ = xb.shape
    x32 = xb.astype(f32)
    rhs = conv_w.astype(f32)[:, None, :]
    xc = lax.conv_general_dilated(x32, rhs, window_strides=(1,), padding=[(CONV_W - 1, 0)],
                                  dimension_numbers=('NWC', 'WIO', 'NWC'),
                                  feature_group_count=D_LRU) + conv_b.astype(f32)
    xblk = xc.reshape(bsz, seq, LRU_BLOCKS, LRU_BLOCK)
    r = jax.nn.sigmoid(jnp.einsum('blhi,hij->blhj', xblk, w_a.astype(f32)).reshape(bsz, seq, D_LRU)
                       + b_a.astype(f32))
    ig = jax.nn.sigmoid(jnp.einsum('blhi,hij->blhj', xblk, w_x.astype(f32)).reshape(bsz, seq, D_LRU)
                        + b_x.astype(f32))
    log_a = -LRU_C * r * jax.nn.softplus(-lam.astype(f32))
    a = jnp.exp(log_a)
    bterm = jnp.sqrt(-jnp.expm1(2.0 * log_a)) * (ig * xc)

    def combine(e1, e2):
        a1, b1 = e1
        a2, b2 = e2
        return a1 * a2, a2 * b1 + b2

    _, h = lax.associative_scan(combine, (a, bterm), axis=1)
    return h * jax.nn.gelu(zb.astype(f32))


def peer_ffn(h, w_q, sub_keys, u_tab, v_tab):
    f32 = jnp.float32
    bsz, seq, dm = h.shape
    ntok = bsz * seq
    ht = h.reshape(ntok, dm)
    q = (ht @ w_q).astype(f32).reshape(ntok, PEER_HEADS, 2, PEER_DQ // 2)
    s = jnp.einsum('thpd,pkd->thpk', q, sub_keys.astype(f32))
    s_top, i_top = lax.top_k(s, PEER_TOPK)
    cand = s_top[:, :, 0, :, None] + s_top[:, :, 1, None, :]
    cand_idx = i_top[:, :, 0, :, None] * N_KEYS + i_top[:, :, 1, None, :]
    best, pos = lax.top_k(cand.reshape(ntok, PEER_HEADS, PEER_TOPK * PEER_TOPK), PEER_TOPK)
    experts = jnp.take_along_axis(cand_idx.reshape(ntok, PEER_HEADS, PEER_TOPK * PEER_TOPK), pos, axis=-1)
    gates = jax.nn.softmax(best, axis=-1)
    nb = ntok // PEER_BLOCK

    def block(args):
        hb, eb, gb = args
        act = jax.nn.gelu(jnp.einsum('td,thkd->thk', hb.astype(f32), u_tab[eb].astype(f32)))
        return jnp.einsum('thk,thkd->td', gb * act, v_tab[eb].astype(f32))

    out = lax.map(block, (ht.reshape(nb, PEER_BLOCK, dm),
                          experts.reshape(nb, PEER_BLOCK, PEER_HEADS, PEER_TOPK),
                          gates.reshape(nb, PEER_BLOCK, PEER_HEADS, PEER_TOPK)))
    return out.reshape(bsz, seq, dm).astype(h.dtype)


def setup_inputs(seed: int = 0) -> dict:
    key = jax.random.key(seed)
    ks = iter(jax.random.split(key, 40))
    nrm = lambda shape, scale: jax.random.normal(next(ks), shape, jnp.float32) * scale
    gain = lambda shape: 1.0 + nrm(shape, 0.05)
    inp = {}
    inp['x'] = nrm((BATCH, SEQ, D_MODEL), 1.0)
    inp['c'] = nrm((BATCH, D_MODEL), 1.0)
    inp['w_mod'] = nrm((DEPTH, D_MODEL, N_MOD * D_MODEL), 0.5 * D_MODEL ** -0.5)
    inp['b_mod'] = nrm((DEPTH, N_MOD * D_MODEL), 0.02)
    inp['g_mix'] = gain((DEPTH, D_MODEL))
    inp['w_in'] = nrm((DEPTH, D_MODEL, IN_COLS), D_MODEL ** -0.5)
    inp['hgrn_lb_logits'] = nrm((DEPTH, D_HGRN), 0.1)
    inp['s5_a_re'] = -0.5 + nrm((DEPTH, S5_GROUPS, S5_STATE), 0.01)
    inp['s5_a_im'] = (jnp.pi * jnp.arange(S5_STATE, dtype=jnp.float32))[None, None, :] + nrm((DEPTH, S5_GROUPS, S5_STATE), 0.01)
    inp['s5_b_re'] = nrm((DEPTH, S5_GROUPS, S5_STATE, S5_GROUP), (2.0 * S5_GROUP) ** -0.5)
    inp['s5_b_im'] = nrm((DEPTH, S5_GROUPS, S5_STATE, S5_GROUP), (2.0 * S5_GROUP) ** -0.5)
    inp['s5_c_re'] = nrm((DEPTH, S5_GROUPS, S5_GROUP, S5_STATE), (2.0 * S5_STATE) ** -0.5)
    inp['s5_c_im'] = nrm((DEPTH, S5_GROUPS, S5_GROUP, S5_STATE), (2.0 * S5_STATE) ** -0.5)
    inp['s5_log_dt'] = jax.random.uniform(next(ks), (DEPTH, S5_GROUPS), jnp.float32,
                                          np.log(S5_DT_MIN), np.log(S5_DT_MAX))
    inp['s5_d'] = nrm((DEPTH, D_S5), 1.0)
    inp['s5_w_glu'] = nrm((DEPTH, D_S5, D_S5), D_S5 ** -0.5)
    inp['s5_b_glu'] = nrm((DEPTH, D_S5), 0.02)
    inp['lru_conv_w'] = nrm((DEPTH, CONV_W, D_LRU), CONV_W ** -0.5)
    inp['lru_conv_b'] = nrm((DEPTH, D_LRU), 0.02)
    inp['lru_w_a'] = nrm((DEPTH, LRU_BLOCKS, LRU_BLOCK, LRU_BLOCK), LRU_BLOCK ** -0.5)
    inp['lru_b_a'] = nrm((DEPTH, D_LRU), 0.02)
    inp['lru_w_x'] = nrm((DEPTH, LRU_BLOCKS, LRU_BLOCK, LRU_BLOCK), LRU_BLOCK ** -0.5)
    inp['lru_b_x'] = nrm((DEPTH, D_LRU), 0.02)
    a_c = jax.random.uniform(next(ks), (DEPTH, D_LRU), jnp.float32, 0.9, 0.999)
    sig = a_c ** (1.0 / LRU_C)
    inp['lru_lambda'] = jnp.log(sig) - jnp.log1p(-sig)
    inp['g_branch'] = gain((DEPTH, D_MIX))
    inp['w_out'] = nrm((DEPTH, D_MIX, D_MODEL), D_MIX ** -0.5)
    inp['g_ffn'] = gain((DEPTH, D_MODEL))
    inp['peer_w_q'] = nrm((DEPTH, D_MODEL, PEER_HEADS * PEER_DQ), D_MODEL ** -0.5)
    inp['peer_sub_keys'] = nrm((DEPTH, 2, N_KEYS, PEER_DQ // 2), (PEER_DQ // 2) ** -0.5)
    inp['peer_u'] = nrm((DEPTH, N_EXPERTS, D_MODEL), D_MODEL ** -0.5)
    inp['peer_v'] = nrm((DEPTH, N_EXPERTS, D_MODEL), (PEER_HEADS * PEER_TOPK) ** -0.5)
    inp['g_final'] = gain((D_MODEL,))
    return inp


def reference(x, c, w_mod, b_mod, g_mix, w_in, hgrn_lb_logits, s5_a_re, s5_a_im, s5_b_re, s5_b_im,
              s5_c_re, s5_c_im, s5_log_dt, s5_d, s5_w_glu, s5_b_glu, lru_conv_w, lru_conv_b,
              lru_w_a, lru_b_a, lru_w_x, lru_b_x, lru_lambda, g_branch, w_out, g_ffn,
              peer_w_q, peer_sub_keys, peer_u, peer_v, g_final):
    cond = jax.nn.silu(c)
    lb_soft = jax.nn.softmax(hgrn_lb_logits.astype(jnp.float32), axis=0)
    lb_all = jnp.cumsum(lb_soft, axis=0) - lb_soft[0:1]
    for l in range(DEPTH):
        mod = (cond @ w_mod[l] + b_mod[l])[:, None, :]
        sh1, sc1, gt1, sh2, sc2, gt2 = jnp.split(mod, N_MOD, axis=-1)
        h = rms_norm(x, g_mix[l]) * (1.0 + sc1) + sh1
        p = h @ w_in[l]
        q_a, f_a, i_a, g_a, u_b, x_c, z_c = jnp.split(p, IN_SPLITS, axis=-1)
        y_a = hgrn2_mixer(q_a, f_a, i_a, g_a, lb_all[l])
        y_b = s5_mixer(u_b, s5_a_re[l], s5_a_im[l], s5_b_re[l], s5_b_im[l], s5_c_re[l], s5_c_im[l],
                       s5_log_dt[l], s5_d[l], s5_w_glu[l], s5_b_glu[l])
        y_c = rglru_mixer(x_c, z_c, lru_conv_w[l], lru_conv_b[l], lru_w_a[l], lru_b_a[l],
                          lru_w_x[l], lru_b_x[l], lru_lambda[l])
        y = jnp.concatenate([_normalize(y_a), _normalize(y_b), _normalize(y_c)], axis=-1)
        y = (y * g_branch[l].astype(jnp.float32)).astype(x.dtype)
        x = x + gt1 * (y @ w_out[l])
        h2 = rms_norm(x, g_ffn[l]) * (1.0 + sc2) + sh2
        x = x + gt2 * peer_ffn(h2, peer_w_q[l], peer_sub_keys[l], peer_u[l], peer_v[l])
    return rms_norm(x, g_final)
```

```python
import functools

import numpy as np
import jax
import jax.numpy as jnp
from jax import lax
from jax.experimental import pallas as pl
from jax.experimental.pallas import tpu as pltpu

F32 = jnp.float32
BF16 = jnp.bfloat16

EPS = 1e-6
N_MOD = 6
HGRN_HEADS = 4
HGRN_DK = 128
D_HGRN = HGRN_HEADS * HGRN_DK
S5_GROUP = 16
S5_GROUPS = 16
S5_STATE = 64
D_S5 = S5_GROUP * S5_GROUPS
N_S5_STATE = S5_GROUPS * S5_STATE
LRU_BLOCKS = 4
LRU_BLOCK = 64
D_LRU = LRU_BLOCKS * LRU_BLOCK
CONV_W = 4
LRU_C = 8.0
PEER_HEADS = 8
N_KEYS = 128
PEER_DQ = 256
PEER_TOPK = 16

SUBLANES = 8
LANES = 128
VMEM_LIMIT_BYTES = 56 * 1024 * 1024

CHUNK_T = 64
TOKEN_TILE = 512
EXPERT_TILE = 1024
RETR_TILE = 256
MOD_COL_TILE = 1536


def _sigmoid(x):
    return 1.0 / (1.0 + jnp.exp(-x))


def _silu(x):
    return x * _sigmoid(x)


def _gelu(x):
    return 0.5 * x * (1.0 + jnp.tanh(0.7978845608028654 * (x + 0.044715 * (x * x * x))))


def _dot(a, b):
    return jnp.dot(a, b, preferred_element_type=F32)


def _dot_nt(a, b):
    return lax.dot_general(a, b, (((1,), (1,)), ((), ())), preferred_element_type=F32)


def _dot_tn(a, b):
    return lax.dot_general(a, b, (((0,), (0,)), ((), ())), preferred_element_type=F32)


def _split2(x):
    hi = x.astype(BF16)
    lo = (x - hi.astype(F32)).astype(BF16)
    return hi, lo


def _params(semantics):
    return pltpu.CompilerParams(dimension_semantics=semantics, vmem_limit_bytes=VMEM_LIMIT_BYTES)


def _mod_body(c_ref, w_ref, b_ref, o_ref):
    cond = _silu(c_ref[...])
    c_hi, c_lo = _split2(cond)
    w_hi, w_lo = _split2(w_ref[0])
    o_ref[0] = _dot(c_hi, w_hi) + _dot(c_hi, w_lo) + _dot(c_lo, w_hi) + b_ref[0]


def _modulation(c, w_mod, b_mod):
    depth, d, n = w_mod.shape
    bsz = c.shape[0]
    tn = MOD_COL_TILE
    return pl.pallas_call(
        _mod_body,
        grid=(depth, n // tn),
        in_specs=[
            pl.BlockSpec((bsz, d), lambda l, j: (0, 0)),
            pl.BlockSpec((1, d, tn), lambda l, j: (l, 0, j)),
            pl.BlockSpec((1, 1, tn), lambda l, j: (l, 0, j)),
        ],
        out_specs=pl.BlockSpec((1, bsz, tn), lambda l, j: (l, 0, j)),
        out_shape=jax.ShapeDtypeStruct((depth, bsz, n), F32),
        compiler_params=_params(("arbitrary", "arbitrary")),
    )(c, w_mod, b_mod.reshape(depth, 1, n))


def _prep_body(lbl_ref, are_ref, aim_ref, ldt_ref, bre_ref, bim_ref,
               lb_ref, abr_ref, abi_ref, bbr_ref, bbi_ref):
    logits = lbl_ref[...]
    depth = logits.shape[0]
    rows = [logits[l:l + 1] for l in range(depth)]
    mx = functools.reduce(jnp.maximum, rows)
    ex = [jnp.exp(r - mx) for r in rows]
    den = functools.reduce(lambda a, b: a + b, ex)
    soft = [e / den for e in ex]
    run = soft[0]
    for l in range(depth):
        if l > 0:
            run = run + soft[l]
        lb_ref[l:l + 1, :] = run - soft[0]

    lam_re = jnp.minimum(are_ref[...], -1e-4)
    lam_im = aim_ref[...]
    dt = jnp.exp(ldt_ref[...])
    mag = jnp.exp(lam_re * dt)
    ab_re = mag * jnp.cos(lam_im * dt)
    ab_im = mag * jnp.sin(lam_im * dt)
    den2 = lam_re * lam_re + lam_im * lam_im
    nr = ab_re - 1.0
    z_re = (nr * lam_re + ab_im * lam_im) / den2
    z_im = (ab_im * lam_re - nr * lam_im) / den2
    abr_ref[...] = ab_re
    abi_ref[...] = ab_im
    br = bre_ref[...]
    bi = bim_ref[...]
    bbr_ref[...] = z_re * br - z_im * bi
    bbi_ref[...] = z_re * bi + z_im * br


def _prepare_params(lb_logits, a_re, a_im, log_dt, b_re, b_im):
    depth = lb_logits.shape[0]
    n = N_S5_STATE
    flat = lambda a: a.reshape(depth, 1, n)
    ldt = jnp.broadcast_to(log_dt[:, :, None], (depth, S5_GROUPS, S5_STATE))
    bt = lambda b: b.transpose(0, 3, 1, 2).reshape(depth, S5_GROUP, n)
    out_shape = (
        jax.ShapeDtypeStruct((depth, D_HGRN), F32),
        jax.ShapeDtypeStruct((depth, 1, n), F32),
        jax.ShapeDtypeStruct((depth, 1, n), F32),
        jax.ShapeDtypeStruct((depth, S5_GROUP, n), F32),
        jax.ShapeDtypeStruct((depth, S5_GROUP, n), F32),
    )
    return pl.pallas_call(_prep_body, out_shape=out_shape)(
        lb_logits, flat(a_re), flat(a_im), flat(ldt), bt(b_re), bt(b_im))


def _inproj_body(x_ref, mod_ref, g_ref, w_ref, pa_ref, pg_ref, pb_ref, pc_ref):
    x = x_ref[...]
    tl, bsz, d = x.shape
    ms = jnp.mean(x * x, axis=-1, keepdims=True)
    xn = x * lax.rsqrt(ms + EPS) * g_ref[...]
    h = xn * (1.0 + mod_ref[1]) + mod_ref[0]
    h = h.reshape(tl * bsz, d).astype(BF16)
    na = pa_ref.shape[0] * LANES
    ng = pg_ref.shape[1]
    nb = pb_ref.shape[1]
    pa = _dot(h, w_ref[:, 0:na])
    for cg in range(pa_ref.shape[0]):
        pa_ref[cg] = pa[:, cg * LANES:(cg + 1) * LANES]
    pg_ref[...] = _dot(h, w_ref[:, na:na + ng])
    pb_ref[...] = _dot(h, w_ref[:, na + ng:na + ng + nb])
    pc_ref[...] = _dot(h, w_ref[:, na + ng + nb:])


def _in_projection(x_tm, mod_l, g_mix, w_in16):
    seq, bsz, d = x_tm.shape
    tl = TOKEN_TILE // bsz
    tm = tl * bsz
    rows = seq * bsz
    ncg = 3 * D_HGRN // LANES
    ng, nb, nc = D_HGRN, D_S5, 2 * D_LRU
    return pl.pallas_call(
        _inproj_body,
        grid=(seq // tl,),
        in_specs=[
            pl.BlockSpec((tl, bsz, d), lambda i: (i, 0, 0)),
            pl.BlockSpec((N_MOD, bsz, d), lambda i: (0, 0, 0)),
            pl.BlockSpec((1, d), lambda i: (0, 0)),
            pl.BlockSpec(w_in16.shape, lambda i: (0, 0)),
        ],
        out_specs=[
            pl.BlockSpec((ncg, tm, LANES), lambda i: (0, i, 0)),
            pl.BlockSpec((tm, ng), lambda i: (i, 0)),
            pl.BlockSpec((tm, nb), lambda i: (i, 0)),
            pl.BlockSpec((tm, nc), lambda i: (i, 0)),
        ],
        out_shape=(
            jax.ShapeDtypeStruct((ncg, rows, LANES), F32),
            jax.ShapeDtypeStruct((rows, ng), F32),
            jax.ShapeDtypeStruct((rows, nb), F32),
            jax.ShapeDtypeStruct((rows, nc), F32),
        ),
        compiler_params=_params(("arbitrary",)),
    )(x_tm, mod_l, g_mix.reshape(1, d), w_in16)


def _hgrn_exponent_matrix(chunk):
    n_levels = int(np.log2(chunk))
    mat = np.zeros((n_levels + 2, chunk, chunk), np.float32)
    u = np.arange(chunk)
    for k in range(n_levels):
        g = chunk >> (k + 1)
        m = 2 * g
        for t in range(chunk):
            r = (t // m) * m + g - 1
            if t % m >= g:
                mat[k, t] = (u > r) & (u <= t)
            else:
                mat[k, t] = (u > t) & (u <= r)
    for t in range(chunk):
        mat[n_levels, t] = u <= t
        mat[n_levels + 1, t] = u > t
    return mat.reshape((n_levels + 2) * chunk, chunk)


def _mixer_body(pa_ref, pg_ref, pb_ref, pc_ref, tmat_ref, lb_ref,
                ar_ref, ai_ref, bbd_ref, cre_ref, cim_ref, dsk_ref, wglu_ref, bglu_ref,
                cw_ref, cb_ref, wa_ref, ba_ref, wx_ref, bx_ref, lam_ref, gbr_ref,
                y_ref,
                st_ref, o_scr, bu_scr, xs_scr, s5_state, lru_state, xext, la_scr, lb_scr, hs_scr):
    rows = pa_ref.shape[1]
    bsz = s5_state.shape[0]
    chunk = rows // bsz
    n_levels = tmat_ref.shape[0] // chunk - 2
    dk = HGRN_DK

    @pl.when(pl.program_id(0) == 0)
    def _():
        st_ref[...] = jnp.zeros_like(st_ref)
        s5_state[...] = jnp.zeros_like(s5_state)
        lru_state[...] = jnp.zeros_like(lru_state)
        xext[0:(CONV_W - 1) * bsz, :] = jnp.zeros(((CONV_W - 1) * bsz, D_LRU), F32)

    ti = lax.broadcasted_iota(jnp.int32, (chunk, chunk), 0)
    si = lax.broadcasted_iota(jnp.int32, (chunk, chunk), 1)
    masks = []
    for k in range(n_levels):
        g = chunk >> (k + 1)
        m = 2 * g
        same = (ti // m) == (si // m)
        masks.append(jnp.where(same, jnp.where((ti % m) >= g, jnp.where((si % m) < g, 1.0, 0.0), 0.0), 0.0))
    diag = jnp.where(ti == si, 1.0, 0.0)
    lb = lb_ref[...]
    tmat = tmat_ref[...]

    def hgrn_batch(b, carry):
        rws = pl.ds(b, chunk, stride=bsz)
        heads = lambda first: jnp.concatenate(
            [pa_ref[first + h, rws, :] for h in range(HGRN_HEADS)], axis=1)
        qraw = heads(0)
        fpre = heads(HGRN_HEADS)
        v16 = heads(2 * HGRN_HEADS).astype(BF16)
        qa = _silu(qraw)
        f = lb + (1.0 - lb) * _sigmoid(fpre)
        logf = jnp.log(f)
        kk = 1.0 - f
        l_hi, l_lo = _split2(logf)
        decay = jnp.exp(_dot(tmat, l_hi) + _dot(tmat, l_lo))
        cum = decay[n_levels * chunk:(n_levels + 1) * chunk]
        qb16 = (qa * cum).astype(BF16)
        ks16 = (kk * decay[(n_levels + 1) * chunk:]).astype(BF16)
        total = cum[chunk - 1:chunk, :]
        qa16 = qa.astype(BF16)
        kk16 = kk.astype(BF16)
        for h in range(HGRN_HEADS):
            sl = slice(h * dk, (h + 1) * dk)
            scores = diag * _dot_nt(qa16[:, sl], kk16[:, sl])
            for k in range(n_levels):
                dl = decay[k * chunk:(k + 1) * chunk, sl]
                ql = (qa[:, sl] * dl).astype(BF16)
                kl = (kk[:, sl] * dl).astype(BF16)
                scores = scores + masks[k] * _dot_nt(ql, kl)
            st = st_ref[b, h]
            o = _dot(scores.astype(BF16), v16[:, sl]) + _dot_nt(qb16[:, sl], st.astype(BF16))
            o_scr[h, rws, :] = o
            st_ref[b, h] = st * total[:, sl] + _dot_tn(v16[:, sl], ks16[:, sl])
        return carry

    lax.fori_loop(0, bsz, hgrn_batch, 0)

    gate = _silu(pg_ref[...])
    ya = []
    ssq = None
    for h in range(HGRN_HEADS):
        sl = slice(h * dk, (h + 1) * dk)
        oh = o_scr[h]
        oh = oh * lax.rsqrt(jnp.mean(oh * oh, axis=-1, keepdims=True) + EPS) * gate[:, sl]
        ya.append(oh)
        s = jnp.sum(oh * oh, axis=-1, keepdims=True)
        ssq = s if ssq is None else ssq + s
    scale = lax.rsqrt(ssq * (1.0 / D_HGRN) + EPS)
    for h in range(HGRN_HEADS):
        sl = slice(h * dk, (h + 1) * dk)
        y_ref[:, sl] = (ya[h] * scale * gbr_ref[:, sl]).astype(y_ref.dtype)

    n = N_S5_STATE
    u = pb_ref[...]
    bu_scr[...] = _dot(u.astype(BF16), bbd_ref[...])
    ar = jnp.broadcast_to(ar_ref[...], (bsz, n))
    ai = jnp.broadcast_to(ai_ref[...], (bsz, n))

    def s5_step(t, st):
        xr, xi = st
        r = pl.ds(pl.multiple_of(t * bsz, bsz), bsz)
        nxr = ar * xr - ai * xi + bu_scr[r, 0:n]
        nxi = ar * xi + ai * xr + bu_scr[r, n:2 * n]
        xs_scr[r, 0:n] = nxr
        xs_scr[r, n:2 * n] = nxi
        return nxr, nxi

    xr, xi = lax.fori_loop(0, chunk, s5_step, (s5_state[:, 0:n], s5_state[:, n:2 * n]))
    s5_state[:, 0:n] = xr
    s5_state[:, n:2 * n] = xi
    yb = (_dot(xs_scr[:, 0:n].astype(BF16), cre_ref[...])
          - _dot(xs_scr[:, n:2 * n].astype(BF16), cim_ref[...]))
    yb = yb + dsk_ref[...] * u
    z = _gelu(yb)
    yb = z * _sigmoid(_dot(z.astype(BF16), wglu_ref[...]) + bglu_ref[...])
    yb = yb * lax.rsqrt(jnp.mean(yb * yb, axis=-1, keepdims=True) + EPS)
    y_ref[:, D_HGRN:D_HGRN + D_S5] = (yb * gbr_ref[:, D_HGRN:D_HGRN + D_S5]).astype(y_ref.dtype)

    tail = (CONV_W - 1) * bsz
    xin = pc_ref[:, 0:D_LRU]
    xext[tail:tail + rows, :] = xin
    xc = cb_ref[...]
    for w in range(CONV_W):
        xc = xc + cw_ref[w:w + 1, :] * xext[w * bsz:w * bsz + rows, :]
    xext[0:tail, :] = xin[rows - tail:rows, :]
    xc16 = xc.astype(BF16)
    rg = _sigmoid(_dot(xc16, wa_ref[...]) + ba_ref[...])
    ig = _sigmoid(_dot(xc16, wx_ref[...]) + bx_ref[...])
    nl = -lam_ref[...]
    softplus = jnp.maximum(nl, 0.0) + jnp.log(1.0 + jnp.exp(-jnp.abs(nl)))
    log_a = (-LRU_C) * rg * softplus
    la_scr[...] = jnp.exp(log_a)
    lb_scr[...] = jnp.sqrt(1.0 - jnp.exp(2.0 * log_a)) * (ig * xc)

    def lru_step(t, hstate):
        r = pl.ds(pl.multiple_of(t * bsz, bsz), bsz)
        hn = la_scr[r, :] * hstate + lb_scr[r, :]
        hs_scr[r, :] = hn
        return hn

    lru_state[...] = lax.fori_loop(0, chunk, lru_step, lru_state[...])
    yc = hs_scr[...] * _gelu(pc_ref[:, D_LRU:2 * D_LRU])
    yc = yc * lax.rsqrt(jnp.mean(yc * yc, axis=-1, keepdims=True) + EPS)
    y_ref[:, D_HGRN + D_S5:] = (yc * gbr_ref[:, D_HGRN + D_S5:]).astype(y_ref.dtype)


def _mixer(pa, pg, pb, pc, bsz, tmat, lb, s5p, lrup, g_branch):
    rows_total = pg.shape[0]
    rows = CHUNK_T * bsz
    d_mix = D_HGRN + D_S5 + D_LRU
    n = N_S5_STATE
    full = lambda a: pl.BlockSpec(a.shape, lambda i, nd=a.ndim: (0,) * nd)
    consts = [tmat, lb, *s5p, *lrup, g_branch]
    return pl.pallas_call(
        _mixer_body,
        grid=(rows_total // rows,),
        in_specs=[
            pl.BlockSpec((pa.shape[0], rows, LANES), lambda i: (0, i, 0)),
            pl.BlockSpec((rows, pg.shape[1]), lambda i: (i, 0)),
            pl.BlockSpec((rows, pb.shape[1]), lambda i: (i, 0)),
            pl.BlockSpec((rows, pc.shape[1]), lambda i: (i, 0)),
        ] + [full(a) for a in consts],
        out_specs=pl.BlockSpec((rows, d_mix), lambda i: (i, 0)),
        out_shape=jax.ShapeDtypeStruct((rows_total, d_mix), BF16),
        scratch_shapes=[
            pltpu.VMEM((bsz, HGRN_HEADS, HGRN_DK, HGRN_DK), F32),
            pltpu.VMEM((HGRN_HEADS, rows, HGRN_DK), F32),
            pltpu.VMEM((rows, 2 * n), F32),
            pltpu.VMEM((rows, 2 * n), F32),
            pltpu.VMEM((bsz, 2 * n), F32),
            pltpu.VMEM((bsz, D_LRU), F32),
            pltpu.VMEM(((CONV_W - 1) * bsz + rows, D_LRU), F32),
            pltpu.VMEM((rows, D_LRU), F32),
            pltpu.VMEM((rows, D_LRU), F32),
            pltpu.VMEM((rows, D_LRU), F32),
        ],
        compiler_params=_params(("arbitrary",)),
    )(pa, pg, pb, pc, *consts)


def _outproj_body(x_ref, y_ref, mod_ref, g_ref, wo_ref, wq_ref, x1_ref, h2_ref, q_ref):
    x = x_ref[...]
    tl, bsz, d = x.shape
    o = _dot(y_ref[...], wo_ref[...]).reshape(tl, bsz, d)
    x1 = x + mod_ref[2] * o
    x1_ref[...] = x1
    ms = jnp.mean(x1 * x1, axis=-1, keepdims=True)
    h2 = x1 * lax.rsqrt(ms + EPS) * g_ref[...]
    h2 = h2 * (1.0 + mod_ref[4]) + mod_ref[3]
    h2 = h2.reshape(tl * bsz, d).astype(BF16)
    h2_ref[...] = h2
    q_ref[...] = _dot(h2, wq_ref[...])


def _out_projection(x_tm, y, mod_l, g_ffn, w_out16, w_q16):
    seq, bsz, d = x_tm.shape
    tl = TOKEN_TILE // bsz
    tm = tl * bsz
    rows = seq * bsz
    nq = w_q16.shape[1]
    return pl.pallas_call(
        _outproj_body,
        grid=(seq // tl,),
        in_specs=[
            pl.BlockSpec((tl, bsz, d), lambda i: (i, 0, 0)),
            pl.BlockSpec((tm, y.shape[1]), lambda i: (i, 0)),
            pl.BlockSpec((N_MOD, bsz, d), lambda i: (0, 0, 0)),
            pl.BlockSpec((1, d), lambda i: (0, 0)),
            pl.BlockSpec(w_out16.shape, lambda i: (0, 0)),
            pl.BlockSpec(w_q16.shape, lambda i: (0, 0)),
        ],
        out_specs=[
            pl.BlockSpec((tl, bsz, d), lambda i: (i, 0, 0)),
            pl.BlockSpec((tm, d), lambda i: (i, 0)),
            pl.BlockSpec((tm, nq), lambda i: (i, 0)),
        ],
        out_shape=(
            jax.ShapeDtypeStruct((seq, bsz, d), F32),
            jax.ShapeDtypeStruct((rows, d), BF16),
            jax.ShapeDtypeStruct((rows, nq), F32),
        ),
        compiler_params=_params(("arbitrary",)),
    )(x_tm, y, mod_l, g_ffn.reshape(1, d), w_out16, w_q16)


def _bitonic_sort_network(n):
    ops = []
    k = 2
    while k <= n:
        j = k // 2
        while j >= 1:
            for i in range(n):
                l = i ^ j
                if l > i:
                    ops.append((i, l, (i & k) == 0))
            j //= 2
        k *= 2
    return ops


def _bitonic_merge_network(n):
    ops = []
    j = n // 2
    while j >= 1:
        for i in range(n):
            l = i ^ j
            if l > i:
                ops.append((i, l, True))
        j //= 2
    return ops


def _apply_network(vals, ops):
    vals = list(vals)
    for i, j, first_max in ops:
        hi = jnp.maximum(vals[i], vals[j])
        lo = jnp.minimum(vals[i], vals[j])
        vals[i], vals[j] = (hi, lo) if first_max else (lo, hi)
    return vals


def _top_sorted(s, k):
    n_keys = s.shape[0]
    n_slabs = n_keys // SUBLANES
    assert n_slabs == k
    slabs = [s[i * SUBLANES:(i + 1) * SUBLANES, :] for i in range(n_slabs)]
    slabs = _apply_network(slabs, _bitonic_sort_network(k))
    merge = _bitonic_merge_network(k)
    shift = SUBLANES // 2
    while shift >= 1:
        rolled = [pltpu.roll(v, shift, 0) for v in slabs]
        slabs = [jnp.maximum(slabs[i], rolled[k - 1 - i]) for i in range(k)]
        slabs = _apply_network(slabs, merge)
        shift //= 2
    return slabs


def _staircase(k):
    return [(x, y) for x in range(k) for y in range(k) if (x + 1) * (y + 1) <= k]


def _retrieval_body(q_ref, keys_ref, r1_ref, e1_ref, n0_ref, c0_ref, s_scr, a_scr, b_scr):
    k = PEER_TOPK
    half = PEER_DQ // 2
    tok = q_ref.shape[0]
    neg = jnp.full((SUBLANES, tok), -jnp.inf, F32)

    for h in range(PEER_HEADS):
        for p in range(2):
            off = h * PEER_DQ + p * half
            s = _dot_nt(keys_ref[p], q_ref[:, off:off + half].astype(BF16))
            s_scr[h, p] = s
            top = _top_sorted(s, k)
            dst = a_scr if p == 0 else b_scr
            for x in range(k):
                dst[x, h:h + 1, :] = top[x][0:1, :]

    a = [a_scr[x] for x in range(k)]
    b = [b_scr[y] for y in range(k)]
    pairs = _staircase(k)
    cands = [a[x] + b[y] for x, y in pairs]
    n_sort = 1
    while n_sort < len(cands):
        n_sort *= 2
    ranked = _apply_network(cands + [neg] * (n_sort - len(cands)), _bitonic_sort_network(n_sort))
    tau = ranked[k - 1]
    best = a[0] + b[0]
    z = jnp.zeros_like(tau)
    for cnd in cands:
        z = z + jnp.where(cnd >= tau, jnp.exp(cnd - best), 0.0)
    inv_z = 1.0 / z

    for h in range(PEER_HEADS):
        s0 = s_scr[h, 0]
        s1 = s_scr[h, 1]
        tau_h = tau[h:h + 1, :]
        cnt = jnp.zeros_like(s0)
        rank = jnp.zeros_like(s1)
        for y in range(k):
            by = b[y][h:h + 1, :]
            cnt = cnt + jnp.where(s0 + by >= tau_h, 1.0, 0.0)
            rank = rank + jnp.where(by > s1, 1.0, 0.0)
        cnt = jnp.where(s0 >= a[k - 1][h:h + 1, :], cnt, 0.0)
        n0_ref[h] = cnt
        r1_ref[h] = rank.astype(r1_ref.dtype)
        e1_ref[h] = jnp.exp(s1 - b[0][h:h + 1, :]).astype(e1_ref.dtype)
        c0_ref[h] = jnp.exp(s0 - a[0][h:h + 1, :]) * inv_z[h:h + 1, :]


def _retrieval(q, keys16):
    rows = q.shape[0]
    tok = RETR_TILE
    tab = lambda dt: jax.ShapeDtypeStruct((PEER_HEADS, N_KEYS, rows), dt)
    tspec = pl.BlockSpec((PEER_HEADS, N_KEYS, tok), lambda i: (0, 0, i))
    return pl.pallas_call(
        _retrieval_body,
        grid=(rows // tok,),
        in_specs=[
            pl.BlockSpec((tok, q.shape[1]), lambda i: (i, 0)),
            pl.BlockSpec(keys16.shape, lambda i: (0, 0, 0)),
        ],
        out_specs=[tspec, tspec, tspec, tspec],
        out_shape=(tab(BF16), tab(BF16), tab(F32), tab(F32)),
        scratch_shapes=[
            pltpu.VMEM((PEER_HEADS, 2, N_KEYS, tok), F32),
            pltpu.VMEM((PEER_TOPK, SUBLANES, tok), F32),
            pltpu.VMEM((PEER_TOPK, SUBLANES, tok), F32),
        ],
        compiler_params=_params(("arbitrary",)),
    )(q, keys16)


def _peer_body(h2_ref, r1_ref, e1_ref, n0_ref, c0_ref, u_ref, vt_ref, x1_ref, mod_ref,
               x2_ref, h2t_scr, acc_scr, a_scr, p_scr):
    j = pl.program_id(1)
    n_j = pl.num_programs(1)
    te = u_ref.shape[0]
    tok = h2_ref.shape[0]
    rows_per_step = te // N_KEYS
    pack = 2 * SUBLANES

    @pl.when(j == 0)
    def _():
        h2t_scr[...] = h2_ref[...].astype(F32).T.astype(BF16)
        acc_scr[...] = jnp.zeros_like(acc_scr)

    a_scr[...] = _dot(u_ref[...], h2t_scr[...])

    def key_row(i, carry):
        gi = j * rows_per_step + i
        base = pl.multiple_of(i * N_KEYS, N_KEYS)
        act = _gelu(a_scr[pl.ds(base, N_KEYS), :]).astype(BF16)
        gates = [jnp.zeros((pack, tok), BF16) for _ in range(N_KEYS // pack)]
        for h in range(PEER_HEADS):
            nb = jnp.broadcast_to(n0_ref[h, pl.ds(gi, 1), :], (pack, tok)).astype(BF16)
            cb = jnp.broadcast_to(c0_ref[h, pl.ds(gi, 1), :], (pack, tok)).astype(BF16)
            for c in range(N_KEYS // pack):
                rs = slice(c * pack, (c + 1) * pack)
                sel = jnp.where(r1_ref[h, rs, :] < nb, e1_ref[h, rs, :], jnp.zeros((pack, tok), BF16))
                gates[c] = gates[c] + sel * cb
        for c in range(N_KEYS // pack):
            p_scr[pl.ds(base + c * pack, pack), :] = gates[c] * act[c * pack:(c + 1) * pack, :]
        return carry

    lax.fori_loop(0, rows_per_step, key_row, 0)
    acc_scr[...] += _dot(vt_ref[...], p_scr[...])

    @pl.when(j == n_j - 1)
    def _():
        tl, bsz, d = x1_ref.shape
        out = acc_scr[...].T.reshape(tl, bsz, d)
        x2_ref[...] = x1_ref[...] + mod_ref[5] * out


def _peer_dense(h2, tables, u16, vt16, x1, mod_l):
    seq, bsz, d = x1.shape
    tok = TOKEN_TILE
    tl = tok // bsz
    te = EXPERT_TILE
    n_exp = u16.shape[0]
    tspec = pl.BlockSpec((PEER_HEADS, N_KEYS, tok), lambda i, j: (0, 0, i))
    return pl.pallas_call(
        _peer_body,
        grid=(seq // tl, n_exp // te),
        in_specs=[
            pl.BlockSpec((tok, d), lambda i, j: (i, 0)),
            tspec, tspec, tspec, tspec,
            pl.BlockSpec((te, d), lambda i, j: (j, 0)),
            pl.BlockSpec((d, te), lambda i, j: (0, j)),
            pl.BlockSpec((tl, bsz, d), lambda i, j: (i, 0, 0)),
            pl.BlockSpec((N_MOD, bsz, d), lambda i, j: (0, 0, 0)),
        ],
        out_specs=pl.BlockSpec((tl, bsz, d), lambda i, j: (i, 0, 0)),
        out_shape=jax.ShapeDtypeStruct((seq, bsz, d), F32),
        scratch_shapes=[
            pltpu.VMEM((d, tok), BF16),
            pltpu.VMEM((d, tok), F32),
            pltpu.VMEM((te, tok), F32),
            pltpu.VMEM((te, tok), BF16),
        ],
        compiler_params=_params(("arbitrary", "arbitrary")),
    )(h2, *tables, u16, vt16, x1, mod_l)


def _final_body(x_ref, g_ref, o_ref):
    x = x_ref[...]
    o_ref[...] = x * lax.rsqrt(jnp.mean(x * x, axis=-1, keepdims=True) + EPS) * g_ref[...]


def _final_norm(x_tm, g):
    seq, bsz, d = x_tm.shape
    tl = TOKEN_TILE // bsz
    return pl.pallas_call(
        _final_body,
        grid=(seq // tl,),
        in_specs=[pl.BlockSpec((tl, bsz, d), lambda i: (i, 0, 0)),
                  pl.BlockSpec((1, d), lambda i: (0, 0))],
        out_specs=pl.BlockSpec((tl, bsz, d), lambda i: (i, 0, 0)),
        out_shape=jax.ShapeDtypeStruct((seq, bsz, d), F32),
        compiler_params=_params(("arbitrary",)),
    )(x_tm, g.reshape(1, d))


def _block_diag(blocks):
    n, r, c = blocks.shape
    eye = jnp.eye(n, dtype=blocks.dtype)
    return (blocks[:, :, None, :] * eye[:, None, :, None]).reshape(n * r, n * c)


def kernel(x, c, w_mod, b_mod, g_mix, w_in, hgrn_lb_logits, s5_a_re, s5_a_im, s5_b_re, s5_b_im, s5_c_re, s5_c_im, s5_log_dt, s5_d, s5_w_glu, s5_b_glu, lru_conv_w, lru_conv_b, lru_w_a, lru_b_a, lru_w_x, lru_b_x, lru_lambda, g_branch, w_out, g_ffn, peer_w_q, peer_sub_keys, peer_u, peer_v, g_final):
    bsz, seq, d = x.shape
    depth = w_in.shape[0]
    assert bsz == SUBLANES and seq % CHUNK_T == 0 and (seq * bsz) % TOKEN_TILE == 0

    x_tm = x.transpose(1, 0, 2)
    mod = _modulation(c, w_mod, b_mod)
    mod = mod.reshape(depth, bsz, N_MOD, d).transpose(0, 2, 1, 3)
    lb_all, ab_re, ab_im, bb_re, bb_im = _prepare_params(
        hgrn_lb_logits, s5_a_re, s5_a_im, s5_log_dt, s5_b_re, s5_b_im)
    tmat = jnp.asarray(_hgrn_exponent_matrix(CHUNK_T), BF16)
    row = lambda a: a.reshape(1, -1)

    for l in range(depth):
        bbd = jnp.concatenate([
            _block_diag(bb_re[l].reshape(S5_GROUP, S5_GROUPS, S5_STATE).transpose(1, 0, 2)),
            _block_diag(bb_im[l].reshape(S5_GROUP, S5_GROUPS, S5_STATE).transpose(1, 0, 2)),
        ], axis=1).astype(BF16)
        cre = _block_diag(s5_c_re[l].transpose(0, 2, 1)).astype(BF16)
        cim = _block_diag(s5_c_im[l].transpose(0, 2, 1)).astype(BF16)
        s5p = (ab_re[l], ab_im[l], bbd, cre, cim, row(s5_d[l]),
               s5_w_glu[l].astype(BF16), row(s5_b_glu[l]))
        lrup = (lru_conv_w[l], row(lru_conv_b[l]),
                _block_diag(lru_w_a[l]).astype(BF16), row(lru_b_a[l]),
                _block_diag(lru_w_x[l]).astype(BF16), row(lru_b_x[l]), row(lru_lambda[l]))

        pa, pg, pb, pc = _in_projection(x_tm, mod[l], g_mix[l], w_in[l].astype(BF16))
        y = _mixer(pa, pg, pb, pc, bsz, tmat, lb_all[l:l + 1], s5p, lrup, row(g_branch[l]))
        x1, h2, q = _out_projection(x_tm, y, mod[l], g_ffn[l],
                                    w_out[l].astype(BF16), peer_w_q[l].astype(BF16))
        tables = _retrieval(q, peer_sub_keys[l].astype(BF16))
        x_tm = _peer_dense(h2, tables, peer_u[l].astype(BF16),
                           peer_v[l].astype(BF16).T, x1, mod[l])

    out = _final_norm(x_tm, g_final)
    return out.transpose(1, 0, 2)
```

```python
import functools

import numpy as np
import jax
import jax.numpy as jnp
from jax import lax
from jax.experimental import pallas as pl
from jax.experimental.pallas import tpu as pltpu

F32 = jnp.float32
BF16 = jnp.bfloat16

EPS = 1e-6
N_MOD = 6
HGRN_HEADS = 4
HGRN_DK = 128
D_HGRN = HGRN_HEADS * HGRN_DK
S5_GROUP = 16
S5_GROUPS = 16
S5_STATE = 64
D_S5 = S5_GROUP * S5_GROUPS
N_S5_STATE = S5_GROUPS * S5_STATE
LRU_BLOCKS = 4
LRU_BLOCK = 64
D_LRU = LRU_BLOCKS * LRU_BLOCK
CONV_W = 4
LRU_C = 8.0
PEER_HEADS = 8
N_KEYS = 128
PEER_DQ = 256
PEER_TOPK = 16

SUBLANES = 8
LANES = 128
VMEM_LIMIT_BYTES = 56 * 1024 * 1024

CHUNK_T = 64
TOKEN_TILE = 512
EXPERT_TILE = 1024
PEER_TOKEN_TILE = 1024
PEER_PIECE = 256
RETR_TILE = 256
MOD_COL_TILE = 1536


def _sigmoid(x):
    return 1.0 / (1.0 + jnp.exp(-x))


def _silu(x):
    return x * _sigmoid(x)


def _gelu(x):
    return 0.5 * x * (1.0 + jnp.tanh(0.7978845608028654 * (x + 0.044715 * (x * x * x))))


def _dot(a, b):
    return jnp.dot(a, b, preferred_element_type=F32)


def _dot_nt(a, b):
    return lax.dot_general(a, b, (((1,), (1,)), ((), ())), preferred_element_type=F32)


def _dot_tn(a, b):
    return lax.dot_general(a, b, (((0,), (0,)), ((), ())), preferred_element_type=F32)


def _split2(x):
    hi = x.astype(BF16)
    lo = (x - hi.astype(F32)).astype(BF16)
    return hi, lo


def _params(semantics):
    return pltpu.CompilerParams(dimension_semantics=semantics, vmem_limit_bytes=VMEM_LIMIT_BYTES)


def _mod_body(c_ref, w_ref, b_ref, o_ref):
    cond = _silu(c_ref[...])
    c_hi, c_lo = _split2(cond)
    w_hi, w_lo = _split2(w_ref[0])
    o_ref[0] = _dot(c_hi, w_hi) + _dot(c_hi, w_lo) + _dot(c_lo, w_hi) + b_ref[0]


def _modulation(c, w_mod, b_mod):
    depth, d, n = w_mod.shape
    bsz = c.shape[0]
    tn = MOD_COL_TILE
    return pl.pallas_call(
        _mod_body,
        grid=(depth, n // tn),
        in_specs=[
            pl.BlockSpec((bsz, d), lambda l, j: (0, 0)),
            pl.BlockSpec((1, d, tn), lambda l, j: (l, 0, j)),
            pl.BlockSpec((1, 1, tn), lambda l, j: (l, 0, j)),
        ],
        out_specs=pl.BlockSpec((1, bsz, tn), lambda l, j: (l, 0, j)),
        out_shape=jax.ShapeDtypeStruct((depth, bsz, n), F32),
        compiler_params=_params(("arbitrary", "arbitrary")),
    )(c, w_mod, b_mod.reshape(depth, 1, n))


def _prep_body(lbl_ref, are_ref, aim_ref, ldt_ref, bre_ref, bim_ref,
               lb_ref, abr_ref, abi_ref, bbr_ref, bbi_ref):
    logits = lbl_ref[...]
    depth = logits.shape[0]
    rows = [logits[l:l + 1] for l in range(depth)]
    mx = functools.reduce(jnp.maximum, rows)
    ex = [jnp.exp(r - mx) for r in rows]
    den = functools.reduce(lambda a, b: a + b, ex)
    soft = [e / den for e in ex]
    run = soft[0]
    for l in range(depth):
        if l > 0:
            run = run + soft[l]
        lb_ref[l:l + 1, :] = run - soft[0]

    lam_re = jnp.minimum(are_ref[...], -1e-4)
    lam_im = aim_ref[...]
    dt = jnp.exp(ldt_ref[...])
    mag = jnp.exp(lam_re * dt)
    ab_re = mag * jnp.cos(lam_im * dt)
    ab_im = mag * jnp.sin(lam_im * dt)
    den2 = lam_re * lam_re + lam_im * lam_im
    nr = ab_re - 1.0
    z_re = (nr * lam_re + ab_im * lam_im) / den2
    z_im = (ab_im * lam_re - nr * lam_im) / den2
    abr_ref[...] = ab_re
    abi_ref[...] = ab_im
    br = bre_ref[...]
    bi = bim_ref[...]
    bbr_ref[...] = z_re * br - z_im * bi
    bbi_ref[...] = z_re * bi + z_im * br


def _prepare_params(lb_logits, a_re, a_im, log_dt, b_re, b_im):
    depth = lb_logits.shape[0]
    n = N_S5_STATE
    flat = lambda a: a.reshape(depth, 1, n)
    ldt = jnp.broadcast_to(log_dt[:, :, None], (depth, S5_GROUPS, S5_STATE))
    bt = lambda b: b.transpose(0, 3, 1, 2).reshape(depth, S5_GROUP, n)
    out_shape = (
        jax.ShapeDtypeStruct((depth, D_HGRN), F32),
        jax.ShapeDtypeStruct((depth, 1, n), F32),
        jax.ShapeDtypeStruct((depth, 1, n), F32),
        jax.ShapeDtypeStruct((depth, S5_GROUP, n), F32),
        jax.ShapeDtypeStruct((depth, S5_GROUP, n), F32),
    )
    return pl.pallas_call(_prep_body, out_shape=out_shape)(
        lb_logits, flat(a_re), flat(a_im), flat(ldt), bt(b_re), bt(b_im))


def _inproj_body(x_ref, mod_ref, g_ref, w_ref, pa_ref, pg_ref, pb_ref, pc_ref):
    x = x_ref[...]
    tl, bsz, d = x.shape
    ms = jnp.mean(x * x, axis=-1, keepdims=True)
    xn = x * lax.rsqrt(ms + EPS) * g_ref[...]
    h = xn * (1.0 + mod_ref[1]) + mod_ref[0]
    h = h.reshape(tl * bsz, d).astype(BF16)
    na = pa_ref.shape[0] * LANES
    ng = pg_ref.shape[1]
    nb = pb_ref.shape[1]
    pa = _dot(h, w_ref[:, 0:na])
    for cg in range(pa_ref.shape[0]):
        pa_ref[cg] = pa[:, cg * LANES:(cg + 1) * LANES]
    pg_ref[...] = _dot(h, w_ref[:, na:na + ng])
    pb_ref[...] = _dot(h, w_ref[:, na + ng:na + ng + nb])
    pc_ref[...] = _dot(h, w_ref[:, na + ng + nb:])


def _in_projection(x_tm, mod_l, g_mix, w_in16):
    seq, bsz, d = x_tm.shape
    tl = TOKEN_TILE // bsz
    tm = tl * bsz
    rows = seq * bsz
    ncg = 3 * D_HGRN // LANES
    ng, nb, nc = D_HGRN, D_S5, 2 * D_LRU
    return pl.pallas_call(
        _inproj_body,
        grid=(seq // tl,),
        in_specs=[
            pl.BlockSpec((tl, bsz, d), lambda i: (i, 0, 0)),
            pl.BlockSpec((N_MOD, bsz, d), lambda i: (0, 0, 0)),
            pl.BlockSpec((1, d), lambda i: (0, 0)),
            pl.BlockSpec(w_in16.shape, lambda i: (0, 0)),
        ],
        out_specs=[
            pl.BlockSpec((ncg, tm, LANES), lambda i: (0, i, 0)),
            pl.BlockSpec((tm, ng), lambda i: (i, 0)),
            pl.BlockSpec((tm, nb), lambda i: (i, 0)),
            pl.BlockSpec((tm, nc), lambda i: (i, 0)),
        ],
        out_shape=(
            jax.ShapeDtypeStruct((ncg, rows, LANES), F32),
            jax.ShapeDtypeStruct((rows, ng), F32),
            jax.ShapeDtypeStruct((rows, nb), F32),
            jax.ShapeDtypeStruct((rows, nc), F32),
        ),
        compiler_params=_params(("arbitrary",)),
    )(x_tm, mod_l, g_mix.reshape(1, d), w_in16)


def _hgrn_exponent_matrix(chunk):
    n_levels = int(np.log2(chunk))
    mat = np.zeros((n_levels + 2, chunk, chunk), np.float32)
    u = np.arange(chunk)
    for k in range(n_levels):
        g = chunk >> (k + 1)
        m = 2 * g
        for t in range(chunk):
            r = (t // m) * m + g - 1
            if t % m >= g:
                mat[k, t] = (u > r) & (u <= t)
            else:
                mat[k, t] = (u > t) & (u <= r)
    for t in range(chunk):
        mat[n_levels, t] = u <= t
        mat[n_levels + 1, t] = u > t
    return mat.reshape((n_levels + 2) * chunk, chunk)


def _mixer_body(pa_ref, pg_ref, pb_ref, pc_ref, tmat_ref, lb_ref,
                ar_ref, ai_ref, bbd_ref, cre_ref, cim_ref, dsk_ref, wglu_ref, bglu_ref,
                cw_ref, cb_ref, wa_ref, ba_ref, wx_ref, bx_ref, lam_ref, gbr_ref,
                y_ref,
                st_ref, o_scr, bu_scr, xs_scr, s5_state, lru_state, xext, la_scr, lb_scr, hs_scr):
    rows = pa_ref.shape[1]
    bsz = s5_state.shape[0]
    chunk = rows // bsz
    n_levels = tmat_ref.shape[0] // chunk - 2
    dk = HGRN_DK

    @pl.when(pl.program_id(0) == 0)
    def _():
        st_ref[...] = jnp.zeros_like(st_ref)
        s5_state[...] = jnp.zeros_like(s5_state)
        lru_state[...] = jnp.zeros_like(lru_state)
        xext[0:(CONV_W - 1) * bsz, :] = jnp.zeros(((CONV_W - 1) * bsz, D_LRU), F32)

    ti = lax.broadcasted_iota(jnp.int32, (chunk, chunk), 0)
    si = lax.broadcasted_iota(jnp.int32, (chunk, chunk), 1)
    masks = []
    for k in range(n_levels):
        g = chunk >> (k + 1)
        m = 2 * g
        same = (ti // m) == (si // m)
        masks.append(jnp.where(same, jnp.where((ti % m) >= g, jnp.where((si % m) < g, 1.0, 0.0), 0.0), 0.0))
    diag = jnp.where(ti == si, 1.0, 0.0)
    lb = lb_ref[...]
    tmat = tmat_ref[...]

    def hgrn_batch(b, carry):
        rws = pl.ds(b, chunk, stride=bsz)
        heads = lambda first: jnp.concatenate(
            [pa_ref[first + h, rws, :] for h in range(HGRN_HEADS)], axis=1)
        qraw = heads(0)
        fpre = heads(HGRN_HEADS)
        v16 = heads(2 * HGRN_HEADS).astype(BF16)
        qa = _silu(qraw)
        f = lb + (1.0 - lb) * _sigmoid(fpre)
        logf = jnp.log(f)
        kk = 1.0 - f
        l_hi, l_lo = _split2(logf)
        decay = jnp.exp(_dot(tmat, l_hi) + _dot(tmat, l_lo))
        cum = decay[n_levels * chunk:(n_levels + 1) * chunk]
        qb16 = (qa * cum).astype(BF16)
        ks16 = (kk * decay[(n_levels + 1) * chunk:]).astype(BF16)
        total = cum[chunk - 1:chunk, :]
        qa16 = qa.astype(BF16)
        kk16 = kk.astype(BF16)
        for h in range(HGRN_HEADS):
            sl = slice(h * dk, (h + 1) * dk)
            scores = diag * _dot_nt(qa16[:, sl], kk16[:, sl])
            for k in range(n_levels):
                dl = decay[k * chunk:(k + 1) * chunk, sl]
                ql = (qa[:, sl] * dl).astype(BF16)
                kl = (kk[:, sl] * dl).astype(BF16)
                scores = scores + masks[k] * _dot_nt(ql, kl)
            st = st_ref[b, h]
            o = _dot(scores.astype(BF16), v16[:, sl]) + _dot_nt(qb16[:, sl], st.astype(BF16))
            o_scr[h, rws, :] = o
            st_ref[b, h] = st * total[:, sl] + _dot_tn(v16[:, sl], ks16[:, sl])
        return carry

    lax.fori_loop(0, bsz, hgrn_batch, 0)

    gate = _silu(pg_ref[...])
    ya = []
    ssq = None
    for h in range(HGRN_HEADS):
        sl = slice(h * dk, (h + 1) * dk)
        oh = o_scr[h]
        oh = oh * lax.rsqrt(jnp.mean(oh * oh, axis=-1, keepdims=True) + EPS) * gate[:, sl]
        ya.append(oh)
        s = jnp.sum(oh * oh, axis=-1, keepdims=True)
        ssq = s if ssq is None else ssq + s
    scale = lax.rsqrt(ssq * (1.0 / D_HGRN) + EPS)
    for h in range(HGRN_HEADS):
        sl = slice(h * dk, (h + 1) * dk)
        y_ref[:, sl] = (ya[h] * scale * gbr_ref[:, sl]).astype(y_ref.dtype)

    n = N_S5_STATE
    u = pb_ref[...]
    bu_scr[...] = _dot(u.astype(BF16), bbd_ref[...])
    ar = jnp.broadcast_to(ar_ref[...], (bsz, n))
    ai = jnp.broadcast_to(ai_ref[...], (bsz, n))

    def s5_step(t, st):
        xr, xi = st
        r = pl.ds(pl.multiple_of(t * bsz, bsz), bsz)
        nxr = ar * xr - ai * xi + bu_scr[r, 0:n]
        nxi = ar * xi + ai * xr + bu_scr[r, n:2 * n]
        xs_scr[r, 0:n] = nxr
        xs_scr[r, n:2 * n] = nxi
        return nxr, nxi

    xr, xi = lax.fori_loop(0, chunk, s5_step, (s5_state[:, 0:n], s5_state[:, n:2 * n]))
    s5_state[:, 0:n] = xr
    s5_state[:, n:2 * n] = xi
    yb = (_dot(xs_scr[:, 0:n].astype(BF16), cre_ref[...])
          - _dot(xs_scr[:, n:2 * n].astype(BF16), cim_ref[...]))
    yb = yb + dsk_ref[...] * u
    z = _gelu(yb)
    yb = z * _sigmoid(_dot(z.astype(BF16), wglu_ref[...]) + bglu_ref[...])
    yb = yb * lax.rsqrt(jnp.mean(yb * yb, axis=-1, keepdims=True) + EPS)
    y_ref[:, D_HGRN:D_HGRN + D_S5] = (yb * gbr_ref[:, D_HGRN:D_HGRN + D_S5]).astype(y_ref.dtype)

    tail = (CONV_W - 1) * bsz
    xin = pc_ref[:, 0:D_LRU]
    xext[tail:tail + rows, :] = xin
    xc = cb_ref[...]
    for w in range(CONV_W):
        xc = xc + cw_ref[w:w + 1, :] * xext[w * bsz:w * bsz + rows, :]
    xext[0:tail, :] = xin[rows - tail:rows, :]
    xc16 = xc.astype(BF16)
    rg = _sigmoid(_dot(xc16, wa_ref[...]) + ba_ref[...])
    ig = _sigmoid(_dot(xc16, wx_ref[...]) + bx_ref[...])
    nl = -lam_ref[...]
    softplus = jnp.maximum(nl, 0.0) + jnp.log(1.0 + jnp.exp(-jnp.abs(nl)))
    log_a = (-LRU_C) * rg * softplus
    la_scr[...] = jnp.exp(log_a)
    lb_scr[...] = jnp.sqrt(1.0 - jnp.exp(2.0 * log_a)) * (ig * xc)

    def lru_step(t, hstate):
        r = pl.ds(pl.multiple_of(t * bsz, bsz), bsz)
        hn = la_scr[r, :] * hstate + lb_scr[r, :]
        hs_scr[r, :] = hn
        return hn

    lru_state[...] = lax.fori_loop(0, chunk, lru_step, lru_state[...])
    yc = hs_scr[...] * _gelu(pc_ref[:, D_LRU:2 * D_LRU])
    yc = yc * lax.rsqrt(jnp.mean(yc * yc, axis=-1, keepdims=True) + EPS)
    y_ref[:, D_HGRN + D_S5:] = (yc * gbr_ref[:, D_HGRN + D_S5:]).astype(y_ref.dtype)


def _mixer(pa, pg, pb, pc, bsz, tmat, lb, s5p, lrup, g_branch):
    rows_total = pg.shape[0]
    rows = CHUNK_T * bsz
    d_mix = D_HGRN + D_S5 + D_LRU
    n = N_S5_STATE
    full = lambda a: pl.BlockSpec(a.shape, lambda i, nd=a.ndim: (0,) * nd)
    consts = [tmat, lb, *s5p, *lrup, g_branch]
    return pl.pallas_call(
        _mixer_body,
        grid=(rows_total // rows,),
        in_specs=[
            pl.BlockSpec((pa.shape[0], rows, LANES), lambda i: (0, i, 0)),
            pl.BlockSpec((rows, pg.shape[1]), lambda i: (i, 0)),
            pl.BlockSpec((rows, pb.shape[1]), lambda i: (i, 0)),
            pl.BlockSpec((rows, pc.shape[1]), lambda i: (i, 0)),
        ] + [full(a) for a in consts],
        out_specs=pl.BlockSpec((rows, d_mix), lambda i: (i, 0)),
        out_shape=jax.ShapeDtypeStruct((rows_total, d_mix), BF16),
        scratch_shapes=[
            pltpu.VMEM((bsz, HGRN_HEADS, HGRN_DK, HGRN_DK), F32),
            pltpu.VMEM((HGRN_HEADS, rows, HGRN_DK), F32),
            pltpu.VMEM((rows, 2 * n), F32),
            pltpu.VMEM((rows, 2 * n), F32),
            pltpu.VMEM((bsz, 2 * n), F32),
            pltpu.VMEM((bsz, D_LRU), F32),
            pltpu.VMEM(((CONV_W - 1) * bsz + rows, D_LRU), F32),
            pltpu.VMEM((rows, D_LRU), F32),
            pltpu.VMEM((rows, D_LRU), F32),
            pltpu.VMEM((rows, D_LRU), F32),
        ],
        compiler_params=_params(("arbitrary",)),
    )(pa, pg, pb, pc, *consts)


def _outproj_body(x_ref, y_ref, mod_ref, g_ref, wo_ref, wq_ref, x1_ref, h2_ref, q_ref):
    x = x_ref[...]
    tl, bsz, d = x.shape
    o = _dot(y_ref[...], wo_ref[...]).reshape(tl, bsz, d)
    x1 = x + mod_ref[2] * o
    x1_ref[...] = x1
    ms = jnp.mean(x1 * x1, axis=-1, keepdims=True)
    h2 = x1 * lax.rsqrt(ms + EPS) * g_ref[...]
    h2 = h2 * (1.0 + mod_ref[4]) + mod_ref[3]
    h2 = h2.reshape(tl * bsz, d).astype(BF16)
    h2_ref[...] = h2
    q_ref[...] = _dot(h2, wq_ref[...])


def _out_projection(x_tm, y, mod_l, g_ffn, w_out16, w_q16):
    seq, bsz, d = x_tm.shape
    tl = TOKEN_TILE // bsz
    tm = tl * bsz
    rows = seq * bsz
    nq = w_q16.shape[1]
    return pl.pallas_call(
        _outproj_body,
        grid=(seq // tl,),
        in_specs=[
            pl.BlockSpec((tl, bsz, d), lambda i: (i, 0, 0)),
            pl.BlockSpec((tm, y.shape[1]), lambda i: (i, 0)),
            pl.BlockSpec((N_MOD, bsz, d), lambda i: (0, 0, 0)),
            pl.BlockSpec((1, d), lambda i: (0, 0)),
            pl.BlockSpec(w_out16.shape, lambda i: (0, 0)),
            pl.BlockSpec(w_q16.shape, lambda i: (0, 0)),
        ],
        out_specs=[
            pl.BlockSpec((tl, bsz, d), lambda i: (i, 0, 0)),
            pl.BlockSpec((tm, d), lambda i: (i, 0)),
            pl.BlockSpec((tm, nq), lambda i: (i, 0)),
        ],
        out_shape=(
            jax.ShapeDtypeStruct((seq, bsz, d), F32),
            jax.ShapeDtypeStruct((rows, d), BF16),
            jax.ShapeDtypeStruct((rows, nq), F32),
        ),
        compiler_params=_params(("arbitrary",)),
    )(x_tm, y, mod_l, g_ffn.reshape(1, d), w_out16, w_q16)


def _bitonic_sort_network(n):
    ops = []
    k = 2
    while k <= n:
        j = k // 2
        while j >= 1:
            for i in range(n):
                l = i ^ j
                if l > i:
                    ops.append((i, l, (i & k) == 0))
            j //= 2
        k *= 2
    return ops


def _bitonic_merge_network(n):
    ops = []
    j = n // 2
    while j >= 1:
        for i in range(n):
            l = i ^ j
            if l > i:
                ops.append((i, l, True))
        j //= 2
    return ops


def _apply_network(vals, ops):
    vals = list(vals)
    for i, j, first_max in ops:
        hi = jnp.maximum(vals[i], vals[j])
        lo = jnp.minimum(vals[i], vals[j])
        vals[i], vals[j] = (hi, lo) if first_max else (lo, hi)
    return vals


def _top_sorted(s, k):
    n_keys = s.shape[0]
    n_slabs = n_keys // SUBLANES
    assert n_slabs == k
    slabs = [s[i * SUBLANES:(i + 1) * SUBLANES, :] for i in range(n_slabs)]
    slabs = _apply_network(slabs, _bitonic_sort_network(k))
    merge = _bitonic_merge_network(k)
    shift = SUBLANES // 2
    while shift >= 1:
        rolled = [pltpu.roll(v, shift, 0) for v in slabs]
        slabs = [jnp.maximum(slabs[i], rolled[k - 1 - i]) for i in range(k)]
        slabs = _apply_network(slabs, merge)
        shift //= 2
    return slabs


def _staircase(k):
    return [(x, y) for x in range(k) for y in range(k) if (x + 1) * (y + 1) <= k]


def _retrieval_body(q_ref, keys_ref, r1_ref, e1_ref, n0_ref, c0_ref, s_scr, a_scr, b_scr):
    k = PEER_TOPK
    half = PEER_DQ // 2
    tok = q_ref.shape[0]
    neg = jnp.full((SUBLANES, tok), -jnp.inf, F32)

    for h in range(PEER_HEADS):
        for p in range(2):
            off = h * PEER_DQ + p * half
            s = _dot_nt(keys_ref[p], q_ref[:, off:off + half].astype(BF16))
            s_scr[h, p] = s
            top = _top_sorted(s, k)
            dst = a_scr if p == 0 else b_scr
            for x in range(k):
                dst[x, h:h + 1, :] = top[x][0:1, :]

    a = [a_scr[x] for x in range(k)]
    b = [b_scr[y] for y in range(k)]
    pairs = _staircase(k)
    cands = [a[x] + b[y] for x, y in pairs]
    n_sort = 1
    while n_sort < len(cands):
        n_sort *= 2
    ranked = _apply_network(cands + [neg] * (n_sort - len(cands)), _bitonic_sort_network(n_sort))
    tau = ranked[k - 1]
    best = a[0] + b[0]
    z = jnp.zeros_like(tau)
    for cnd in cands:
        z = z + jnp.where(cnd >= tau, jnp.exp(cnd - best), 0.0)
    inv_z = 1.0 / z

    for h in range(PEER_HEADS):
        s0 = s_scr[h, 0]
        s1 = s_scr[h, 1]
        tau_h = tau[h:h + 1, :]
        cnt = jnp.zeros_like(s0)
        rank = jnp.zeros_like(s1)
        for y in range(k):
            by = b[y][h:h + 1, :]
            cnt = cnt + jnp.where(s0 + by >= tau_h, 1.0, 0.0)
            rank = rank + jnp.where(by > s1, 1.0, 0.0)
        cnt = jnp.where(s0 >= a[k - 1][h:h + 1, :], cnt, 0.0)
        n0_ref[h] = cnt
        r1_ref[h] = rank.astype(r1_ref.dtype)
        e1_ref[h] = jnp.exp(s1 - b[0][h:h + 1, :]).astype(e1_ref.dtype)
        c0_ref[h] = jnp.exp(s0 - a[0][h:h + 1, :]) * inv_z[h:h + 1, :]


def _retrieval(q, keys16):
    rows = q.shape[0]
    tok = RETR_TILE
    tab = lambda dt: jax.ShapeDtypeStruct((PEER_HEADS, N_KEYS, rows), dt)
    tspec = pl.BlockSpec((PEER_HEADS, N_KEYS, tok), lambda i: (0, 0, i))
    return pl.pallas_call(
        _retrieval_body,
        grid=(rows // tok,),
        in_specs=[
            pl.BlockSpec((tok, q.shape[1]), lambda i: (i, 0)),
            pl.BlockSpec(keys16.shape, lambda i: (0, 0, 0)),
        ],
        out_specs=[tspec, tspec, tspec, tspec],
        out_shape=(tab(BF16), tab(BF16), tab(F32), tab(F32)),
        scratch_shapes=[
            pltpu.VMEM((PEER_HEADS, 2, N_KEYS, tok), F32),
            pltpu.VMEM((PEER_TOPK, SUBLANES, tok), F32),
            pltpu.VMEM((PEER_TOPK, SUBLANES, tok), F32),
        ],
        compiler_params=_params(("arbitrary",)),
    )(q, keys16)


def _peer_body(h2_ref, r1_ref, e1_ref, n0_ref, c0_ref, u_ref, vt_ref, x1_ref, mod_ref,
               x2_ref, h2t_scr, acc_scr, a_scr, p_scr):
    j = pl.program_id(1)
    n_j = pl.num_programs(1)
    te = u_ref.shape[0]
    tok = h2_ref.shape[0]
    rows_per_step = te // N_KEYS
    pack = 2 * SUBLANES

    @pl.when(j == 0)
    def _():
        h2t_scr[...] = h2_ref[...].astype(F32).T.astype(BF16)
        acc_scr[...] = jnp.zeros_like(acc_scr)
        a_scr[...] = jnp.zeros_like(a_scr)

    zero = jnp.zeros((pack, PEER_PIECE), BF16)
    for c in range(tok // PEER_PIECE):
        cols = slice(c * PEER_PIECE, (c + 1) * PEER_PIECE)
        for r in range(rows_per_step):
            act = _gelu(a_scr[r * N_KEYS:(r + 1) * N_KEYS, cols]).astype(BF16)
            gates = [zero for _ in range(N_KEYS // pack)]
            for h in range(PEER_HEADS):
                nb = jnp.broadcast_to(n0_ref[h, r:r + 1, cols], (pack, PEER_PIECE)).astype(BF16)
                cb = jnp.broadcast_to(c0_ref[h, r:r + 1, cols], (pack, PEER_PIECE)).astype(BF16)
                for k in range(N_KEYS // pack):
                    rs = slice(k * pack, (k + 1) * pack)
                    sel = jnp.where(r1_ref[h, rs, cols] < nb, e1_ref[h, rs, cols], zero)
                    gates[k] = gates[k] + sel * cb
            for k in range(N_KEYS // pack):
                p_scr[r * N_KEYS + k * pack:r * N_KEYS + (k + 1) * pack, cols] = (
                    gates[k] * act[k * pack:(k + 1) * pack, :])
        acc_scr[:, cols] += _dot(vt_ref[...], p_scr[:, cols])
        a_scr[:, cols] = _dot(u_ref[...], h2t_scr[:, cols])

    @pl.when(j == n_j - 1)
    def _():
        tl, bsz, d = x1_ref.shape
        out = acc_scr[...].T.reshape(tl, bsz, d)
        x2_ref[...] = x1_ref[...] + mod_ref[5] * out


def _peer_dense(h2, tables, u16, vt16, x1, mod_l):
    seq, bsz, d = x1.shape
    tok = PEER_TOKEN_TILE
    tl = tok // bsz
    te = EXPERT_TILE
    n_tiles = u16.shape[0] // te
    rows_per_step = te // N_KEYS
    cur = lambda j: jnp.minimum(j, n_tiles - 1)
    prev = lambda j: jnp.maximum(j - 1, 0)
    tspec = pl.BlockSpec((PEER_HEADS, N_KEYS, tok), lambda i, j: (0, 0, i))
    rspec = pl.BlockSpec((PEER_HEADS, rows_per_step, tok), lambda i, j: (0, prev(j), i))
    return pl.pallas_call(
        _peer_body,
        grid=(seq // tl, n_tiles + 1),
        in_specs=[
            pl.BlockSpec((tok, d), lambda i, j: (i, 0)),
            tspec, tspec, rspec, rspec,
            pl.BlockSpec((te, d), lambda i, j: (cur(j), 0)),
            pl.BlockSpec((d, te), lambda i, j: (0, prev(j))),
            pl.BlockSpec((tl, bsz, d), lambda i, j: (i, 0, 0)),
            pl.BlockSpec((N_MOD, bsz, d), lambda i, j: (0, 0, 0)),
        ],
        out_specs=pl.BlockSpec((tl, bsz, d), lambda i, j: (i, 0, 0)),
        out_shape=jax.ShapeDtypeStruct((seq, bsz, d), F32),
        scratch_shapes=[
            pltpu.VMEM((d, tok), BF16),
            pltpu.VMEM((d, tok), F32),
            pltpu.VMEM((te, tok), F32),
            pltpu.VMEM((te, tok), BF16),
        ],
        compiler_params=_params(("arbitrary", "arbitrary")),
    )(h2, *tables, u16, vt16, x1, mod_l)


def _final_body(x_ref, g_ref, o_ref):
    x = x_ref[...]
    o_ref[...] = x * lax.rsqrt(jnp.mean(x * x, axis=-1, keepdims=True) + EPS) * g_ref[...]


def _final_norm(x_tm, g):
    seq, bsz, d = x_tm.shape
    tl = TOKEN_TILE // bsz
    return pl.pallas_call(
        _final_body,
        grid=(seq // tl,),
        in_specs=[pl.BlockSpec((tl, bsz, d), lambda i: (i, 0, 0)),
                  pl.BlockSpec((1, d), lambda i: (0, 0))],
        out_specs=pl.BlockSpec((tl, bsz, d), lambda i: (i, 0, 0)),
        out_shape=jax.ShapeDtypeStruct((seq, bsz, d), F32),
        compiler_params=_params(("arbitrary",)),
    )(x_tm, g.reshape(1, d))


def _block_diag(blocks):
    n, r, c = blocks.shape
    eye = jnp.eye(n, dtype=blocks.dtype)
    return (blocks[:, :, None, :] * eye[:, None, :, None]).reshape(n * r, n * c)


def kernel(x, c, w_mod, b_mod, g_mix, w_in, hgrn_lb_logits, s5_a_re, s5_a_im, s5_b_re, s5_b_im, s5_c_re, s5_c_im, s5_log_dt, s5_d, s5_w_glu, s5_b_glu, lru_conv_w, lru_conv_b, lru_w_a, lru_b_a, lru_w_x, lru_b_x, lru_lambda, g_branch, w_out, g_ffn, peer_w_q, peer_sub_keys, peer_u, peer_v, g_final):
    bsz, seq, d = x.shape
    depth = w_in.shape[0]
    assert bsz == SUBLANES and seq % CHUNK_T == 0 and (seq * bsz) % TOKEN_TILE == 0

    x_tm = x.transpose(1, 0, 2)
    mod = _modulation(c, w_mod, b_mod)
    mod = mod.reshape(depth, bsz, N_MOD, d).transpose(0, 2, 1, 3)
    lb_all, ab_re, ab_im, bb_re, bb_im = _prepare_params(
        hgrn_lb_logits, s5_a_re, s5_a_im, s5_log_dt, s5_b_re, s5_b_im)
    tmat = jnp.asarray(_hgrn_exponent_matrix(CHUNK_T), BF16)
    row = lambda a: a.reshape(1, -1)

    for l in range(depth):
        bbd = jnp.concatenate([
            _block_diag(bb_re[l].reshape(S5_GROUP, S5_GROUPS, S5_STATE).transpose(1, 0, 2)),
            _block_diag(bb_im[l].reshape(S5_GROUP, S5_GROUPS, S5_STATE).transpose(1, 0, 2)),
        ], axis=1).astype(BF16)
        cre = _block_diag(s5_c_re[l].transpose(0, 2, 1)).astype(BF16)
        cim = _block_diag(s5_c_im[l].transpose(0, 2, 1)).astype(BF16)
        s5p = (ab_re[l], ab_im[l], bbd, cre, cim, row(s5_d[l]),
               s5_w_glu[l].astype(BF16), row(s5_b_glu[l]))
        lrup = (lru_conv_w[l], row(lru_conv_b[l]),
                _block_diag(lru_w_a[l]).astype(BF16), row(lru_b_a[l]),
                _block_diag(lru_w_x[l]).astype(BF16), row(lru_b_x[l]), row(lru_lambda[l]))

        pa, pg, pb, pc = _in_projection(x_tm, mod[l], g_mix[l], w_in[l].astype(BF16))
        y = _mixer(pa, pg, pb, pc, bsz, tmat, lb_all[l:l + 1], s5p, lrup, row(g_branch[l]))
        x1, h2, q = _out_projection(x_tm, y, mod[l], g_ffn[l],
                                    w_out[l].astype(BF16), peer_w_q[l].astype(BF16))
        tables = _retrieval(q, peer_sub_keys[l].astype(BF16))
        x_tm = _peer_dense(h2, tables, peer_u[l].astype(BF16),
                           peer_v[l].astype(BF16).T, x1, mod[l])

    out = _final_norm(x_tm, g_final)
    return out.transpose(1, 0, 2)
```

```python
import functools

import numpy as np
import jax
import jax.numpy as jnp
from jax import lax
from jax.experimental import pallas as pl
from jax.experimental.pallas import tpu as pltpu

F32 = jnp.float32
BF16 = jnp.bfloat16

EPS = 1e-6
N_MOD = 6
HGRN_HEADS = 4
HGRN_DK = 128
D_HGRN = HGRN_HEADS * HGRN_DK
S5_GROUP = 16
S5_GROUPS = 16
S5_STATE = 64
D_S5 = S5_GROUP * S5_GROUPS
N_S5_STATE = S5_GROUPS * S5_STATE
LRU_BLOCKS = 4
LRU_BLOCK = 64
D_LRU = LRU_BLOCKS * LRU_BLOCK
CONV_W = 4
LRU_C = 8.0
PEER_HEADS = 8
N_KEYS = 128
PEER_DQ = 256
PEER_TOPK = 16

SUBLANES = 8
LANES = 128
VMEM_LIMIT_BYTES = 56 * 1024 * 1024

CHUNK_T = 64
TOKEN_TILE = 512
EXPERT_TILE = 1024
PEER_TOKEN_TILE = 1024
PEER_PIECE = 256
RETR_TILE = 256
MOD_COL_TILE = 1536


def _sigmoid(x):
    return 1.0 / (1.0 + jnp.exp(-x))


def _silu(x):
    return x * _sigmoid(x)


def _gelu(x):
    return 0.5 * x * (1.0 + jnp.tanh(0.7978845608028654 * (x + 0.044715 * (x * x * x))))


def _dot(a, b):
    return jnp.dot(a, b, preferred_element_type=F32)


def _dot_nt(a, b):
    return lax.dot_general(a, b, (((1,), (1,)), ((), ())), preferred_element_type=F32)


def _dot_tn(a, b):
    return lax.dot_general(a, b, (((0,), (0,)), ((), ())), preferred_element_type=F32)


def _split2(x):
    hi = x.astype(BF16)
    lo = (x - hi.astype(F32)).astype(BF16)
    return hi, lo


def _params(semantics):
    return pltpu.CompilerParams(dimension_semantics=semantics, vmem_limit_bytes=VMEM_LIMIT_BYTES)


def _mod_body(c_ref, w_ref, b_ref, o_ref):
    cond = _silu(c_ref[...])
    c_hi, c_lo = _split2(cond)
    w_hi, w_lo = _split2(w_ref[0])
    o_ref[0] = _dot(c_hi, w_hi) + _dot(c_hi, w_lo) + _dot(c_lo, w_hi) + b_ref[0]


def _modulation(c, w_mod, b_mod):
    depth, d, n = w_mod.shape
    bsz = c.shape[0]
    tn = MOD_COL_TILE
    return pl.pallas_call(
        _mod_body,
        grid=(depth, n // tn),
        in_specs=[
            pl.BlockSpec((bsz, d), lambda l, j: (0, 0)),
            pl.BlockSpec((1, d, tn), lambda l, j: (l, 0, j)),
            pl.BlockSpec((1, 1, tn), lambda l, j: (l, 0, j)),
        ],
        out_specs=pl.BlockSpec((1, bsz, tn), lambda l, j: (l, 0, j)),
        out_shape=jax.ShapeDtypeStruct((depth, bsz, n), F32),
        compiler_params=_params(("arbitrary", "arbitrary")),
    )(c, w_mod, b_mod.reshape(depth, 1, n))


def _prep_body(lbl_ref, are_ref, aim_ref, ldt_ref, bre_ref, bim_ref,
               lb_ref, abr_ref, abi_ref, bbr_ref, bbi_ref):
    logits = lbl_ref[...]
    depth = logits.shape[0]
    rows = [logits[l:l + 1] for l in range(depth)]
    mx = functools.reduce(jnp.maximum, rows)
    ex = [jnp.exp(r - mx) for r in rows]
    den = functools.reduce(lambda a, b: a + b, ex)
    soft = [e / den for e in ex]
    run = soft[0]
    for l in range(depth):
        if l > 0:
            run = run + soft[l]
        lb_ref[l:l + 1, :] = run - soft[0]

    lam_re = jnp.minimum(are_ref[...], -1e-4)
    lam_im = aim_ref[...]
    dt = jnp.exp(ldt_ref[...])
    mag = jnp.exp(lam_re * dt)
    ab_re = mag * jnp.cos(lam_im * dt)
    ab_im = mag * jnp.sin(lam_im * dt)
    den2 = lam_re * lam_re + lam_im * lam_im
    nr = ab_re - 1.0
    z_re = (nr * lam_re + ab_im * lam_im) / den2
    z_im = (ab_im * lam_re - nr * lam_im) / den2
    abr_ref[...] = ab_re
    abi_ref[...] = ab_im
    br = bre_ref[...]
    bi = bim_ref[...]
    bbr_ref[...] = z_re * br - z_im * bi
    bbi_ref[...] = z_re * bi + z_im * br


def _prepare_params(lb_logits, a_re, a_im, log_dt, b_re, b_im):
    depth = lb_logits.shape[0]
    n = N_S5_STATE
    flat = lambda a: a.reshape(depth, 1, n)
    ldt = jnp.broadcast_to(log_dt[:, :, None], (depth, S5_GROUPS, S5_STATE))
    bt = lambda b: b.transpose(0, 3, 1, 2).reshape(depth, S5_GROUP, n)
    out_shape = (
        jax.ShapeDtypeStruct((depth, D_HGRN), F32),
        jax.ShapeDtypeStruct((depth, 1, n), F32),
        jax.ShapeDtypeStruct((depth, 1, n), F32),
        jax.ShapeDtypeStruct((depth, S5_GROUP, n), F32),
        jax.ShapeDtypeStruct((depth, S5_GROUP, n), F32),
    )
    return pl.pallas_call(_prep_body, out_shape=out_shape)(
        lb_logits, flat(a_re), flat(a_im), flat(ldt), bt(b_re), bt(b_im))


def _inproj_body(x_ref, mod_ref, g_ref, w_ref, pa_ref, pg_ref, pb_ref, pc_ref):
    x = x_ref[...]
    tl, bsz, d = x.shape
    ms = jnp.mean(x * x, axis=-1, keepdims=True)
    xn = x * lax.rsqrt(ms + EPS) * g_ref[...]
    h = xn * (1.0 + mod_ref[1]) + mod_ref[0]
    h = h.reshape(tl * bsz, d).astype(BF16)
    na = pa_ref.shape[0] * LANES
    ng = pg_ref.shape[1]
    nb = pb_ref.shape[1]
    pa = _dot(h, w_ref[:, 0:na])
    for cg in range(pa_ref.shape[0]):
        pa_ref[cg] = pa[:, cg * LANES:(cg + 1) * LANES]
    pg_ref[...] = _dot(h, w_ref[:, na:na + ng])
    pb_ref[...] = _dot(h, w_ref[:, na + ng:na + ng + nb])
    pc_ref[...] = _dot(h, w_ref[:, na + ng + nb:])


def _in_projection(x_tm, mod_l, g_mix, w_in16):
    seq, bsz, d = x_tm.shape
    tl = TOKEN_TILE // bsz
    tm = tl * bsz
    rows = seq * bsz
    ncg = 3 * D_HGRN // LANES
    ng, nb, nc = D_HGRN, D_S5, 2 * D_LRU
    return pl.pallas_call(
        _inproj_body,
        grid=(seq // tl,),
        in_specs=[
            pl.BlockSpec((tl, bsz, d), lambda i: (i, 0, 0)),
            pl.BlockSpec((N_MOD, bsz, d), lambda i: (0, 0, 0)),
            pl.BlockSpec((1, d), lambda i: (0, 0)),
            pl.BlockSpec(w_in16.shape, lambda i: (0, 0)),
        ],
        out_specs=[
            pl.BlockSpec((ncg, tm, LANES), lambda i: (0, i, 0)),
            pl.BlockSpec((tm, ng), lambda i: (i, 0)),
            pl.BlockSpec((tm, nb), lambda i: (i, 0)),
            pl.BlockSpec((tm, nc), lambda i: (i, 0)),
        ],
        out_shape=(
            jax.ShapeDtypeStruct((ncg, rows, LANES), F32),
            jax.ShapeDtypeStruct((rows, ng), F32),
            jax.ShapeDtypeStruct((rows, nb), F32),
            jax.ShapeDtypeStruct((rows, nc), F32),
        ),
        compiler_params=_params(("arbitrary",)),
    )(x_tm, mod_l, g_mix.reshape(1, d), w_in16)


def _hgrn_exponent_matrix(chunk):
    n_levels = int(np.log2(chunk))
    mat = np.zeros((n_levels + 2, chunk, chunk), np.float32)
    u = np.arange(chunk)
    for k in range(n_levels):
        g = chunk >> (k + 1)
        m = 2 * g
        for t in range(chunk):
            r = (t // m) * m + g - 1
            if t % m >= g:
                mat[k, t] = (u > r) & (u <= t)
            else:
                mat[k, t] = (u > t) & (u <= r)
    for t in range(chunk):
        mat[n_levels, t] = u <= t
        mat[n_levels + 1, t] = u > t
    return mat.reshape((n_levels + 2) * chunk, chunk)


def _mixer_body(pa_ref, pg_ref, pb_ref, pc_ref, tmat_ref, lb_ref,
                ar_ref, ai_ref, bbd_ref, cre_ref, cim_ref, dsk_ref, wglu_ref, bglu_ref,
                cw_ref, cb_ref, wa_ref, ba_ref, wx_ref, bx_ref, lam_ref, gbr_ref,
                y_ref,
                st_ref, o_scr, bu_scr, xs_scr, s5_state, lru_state, xext, la_scr, lb_scr, hs_scr):
    rows = pa_ref.shape[1]
    bsz = s5_state.shape[0]
    chunk = rows // bsz
    n_levels = tmat_ref.shape[0] // chunk - 2
    dk = HGRN_DK

    @pl.when(pl.program_id(0) == 0)
    def _():
        st_ref[...] = jnp.zeros_like(st_ref)
        s5_state[...] = jnp.zeros_like(s5_state)
        lru_state[...] = jnp.zeros_like(lru_state)
        xext[0:(CONV_W - 1) * bsz, :] = jnp.zeros(((CONV_W - 1) * bsz, D_LRU), F32)

    ti = lax.broadcasted_iota(jnp.int32, (chunk, chunk), 0)
    si = lax.broadcasted_iota(jnp.int32, (chunk, chunk), 1)
    masks = []
    for k in range(n_levels):
        g = chunk >> (k + 1)
        m = 2 * g
        same = (ti // m) == (si // m)
        masks.append(jnp.where(same, jnp.where((ti % m) >= g, jnp.where((si % m) < g, 1.0, 0.0), 0.0), 0.0))
    diag = jnp.where(ti == si, 1.0, 0.0)
    lb = lb_ref[...]
    tmat = tmat_ref[...]

    def hgrn_batch(b, carry):
        rws = pl.ds(b, chunk, stride=bsz)
        heads = lambda first: jnp.concatenate(
            [pa_ref[first + h, rws, :] for h in range(HGRN_HEADS)], axis=1)
        qraw = heads(0)
        fpre = heads(HGRN_HEADS)
        v16 = heads(2 * HGRN_HEADS).astype(BF16)
        qa = _silu(qraw)
        f = lb + (1.0 - lb) * _sigmoid(fpre)
        logf = jnp.log(f)
        kk = 1.0 - f
        l_hi, l_lo = _split2(logf)
        decay = jnp.exp(_dot(tmat, l_hi) + _dot(tmat, l_lo))
        cum = decay[n_levels * chunk:(n_levels + 1) * chunk]
        qb16 = (qa * cum).astype(BF16)
        ks16 = (kk * decay[(n_levels + 1) * chunk:]).astype(BF16)
        total = cum[chunk - 1:chunk, :]
        qa16 = qa.astype(BF16)
        kk16 = kk.astype(BF16)
        for h in range(HGRN_HEADS):
            sl = slice(h * dk, (h + 1) * dk)
            scores = diag * _dot_nt(qa16[:, sl], kk16[:, sl])
            for k in range(n_levels):
                dl = decay[k * chunk:(k + 1) * chunk, sl]
                ql = (qa[:, sl] * dl).astype(BF16)
                kl = (kk[:, sl] * dl).astype(BF16)
                scores = scores + masks[k] * _dot_nt(ql, kl)
            st = st_ref[b, h]
            o = _dot(scores.astype(BF16), v16[:, sl]) + _dot_nt(qb16[:, sl], st.astype(BF16))
            o_scr[h, rws, :] = o
            st_ref[b, h] = st * total[:, sl] + _dot_tn(v16[:, sl], ks16[:, sl])
        return carry

    lax.fori_loop(0, bsz, hgrn_batch, 0, unroll=2)

    gate = _silu(pg_ref[...])
    ya = []
    ssq = None
    for h in range(HGRN_HEADS):
        sl = slice(h * dk, (h + 1) * dk)
        oh = o_scr[h]
        oh = oh * lax.rsqrt(jnp.mean(oh * oh, axis=-1, keepdims=True) + EPS) * gate[:, sl]
        ya.append(oh)
        s = jnp.sum(oh * oh, axis=-1, keepdims=True)
        ssq = s if ssq is None else ssq + s
    scale = lax.rsqrt(ssq * (1.0 / D_HGRN) + EPS)
    for h in range(HGRN_HEADS):
        sl = slice(h * dk, (h + 1) * dk)
        y_ref[:, sl] = (ya[h] * scale * gbr_ref[:, sl]).astype(y_ref.dtype)

    n = N_S5_STATE
    u = pb_ref[...]
    bu_scr[...] = _dot(u.astype(BF16), bbd_ref[...])
    ar = jnp.broadcast_to(ar_ref[...], (bsz, n))
    ai = jnp.broadcast_to(ai_ref[...], (bsz, n))

    def s5_step(t, st):
        xr, xi = st
        r = pl.ds(pl.multiple_of(t * bsz, bsz), bsz)
        nxr = ar * xr - ai * xi + bu_scr[r, 0:n]
        nxi = ar * xi + ai * xr + bu_scr[r, n:2 * n]
        xs_scr[r, 0:n] = nxr
        xs_scr[r, n:2 * n] = nxi
        return nxr, nxi

    xr, xi = lax.fori_loop(0, chunk, s5_step, (s5_state[:, 0:n], s5_state[:, n:2 * n]))
    s5_state[:, 0:n] = xr
    s5_state[:, n:2 * n] = xi
    yb = (_dot(xs_scr[:, 0:n].astype(BF16), cre_ref[...])
          - _dot(xs_scr[:, n:2 * n].astype(BF16), cim_ref[...]))
    yb = yb + dsk_ref[...] * u
    z = _gelu(yb)
    yb = z * _sigmoid(_dot(z.astype(BF16), wglu_ref[...]) + bglu_ref[...])
    yb = yb * lax.rsqrt(jnp.mean(yb * yb, axis=-1, keepdims=True) + EPS)
    y_ref[:, D_HGRN:D_HGRN + D_S5] = (yb * gbr_ref[:, D_HGRN:D_HGRN + D_S5]).astype(y_ref.dtype)

    tail = (CONV_W - 1) * bsz
    xin = pc_ref[:, 0:D_LRU]
    xext[tail:tail + rows, :] = xin
    xc = cb_ref[...]
    for w in range(CONV_W):
        xc = xc + cw_ref[w:w + 1, :] * xext[w * bsz:w * bsz + rows, :]
    xext[0:tail, :] = xin[rows - tail:rows, :]
    xc16 = xc.astype(BF16)
    rg = _sigmoid(_dot(xc16, wa_ref[...]) + ba_ref[...])
    ig = _sigmoid(_dot(xc16, wx_ref[...]) + bx_ref[...])
    nl = -lam_ref[...]
    softplus = jnp.maximum(nl, 0.0) + jnp.log(1.0 + jnp.exp(-jnp.abs(nl)))
    log_a = (-LRU_C) * rg * softplus
    la_scr[...] = jnp.exp(log_a)
    lb_scr[...] = jnp.sqrt(1.0 - jnp.exp(2.0 * log_a)) * (ig * xc)

    def lru_step(t, hstate):
        r = pl.ds(pl.multiple_of(t * bsz, bsz), bsz)
        hn = la_scr[r, :] * hstate + lb_scr[r, :]
        hs_scr[r, :] = hn
        return hn

    lru_state[...] = lax.fori_loop(0, chunk, lru_step, lru_state[...])
    yc = hs_scr[...] * _gelu(pc_ref[:, D_LRU:2 * D_LRU])
    yc = yc * lax.rsqrt(jnp.mean(yc * yc, axis=-1, keepdims=True) + EPS)
    y_ref[:, D_HGRN + D_S5:] = (yc * gbr_ref[:, D_HGRN + D_S5:]).astype(y_ref.dtype)


def _mixer(pa, pg, pb, pc, bsz, tmat, lb, s5p, lrup, g_branch):
    rows_total = pg.shape[0]
    rows = CHUNK_T * bsz
    d_mix = D_HGRN + D_S5 + D_LRU
    n = N_S5_STATE
    full = lambda a: pl.BlockSpec(a.shape, lambda i, nd=a.ndim: (0,) * nd)
    consts = [tmat, lb, *s5p, *lrup, g_branch]
    return pl.pallas_call(
        _mixer_body,
        grid=(rows_total // rows,),
        in_specs=[
            pl.BlockSpec((pa.shape[0], rows, LANES), lambda i: (0, i, 0)),
            pl.BlockSpec((rows, pg.shape[1]), lambda i: (i, 0)),
            pl.BlockSpec((rows, pb.shape[1]), lambda i: (i, 0)),
            pl.BlockSpec((rows, pc.shape[1]), lambda i: (i, 0)),
        ] + [full(a) for a in consts],
        out_specs=pl.BlockSpec((rows, d_mix), lambda i: (i, 0)),
        out_shape=jax.ShapeDtypeStruct((rows_total, d_mix), BF16),
        scratch_shapes=[
            pltpu.VMEM((bsz, HGRN_HEADS, HGRN_DK, HGRN_DK), F32),
            pltpu.VMEM((HGRN_HEADS, rows, HGRN_DK), F32),
            pltpu.VMEM((rows, 2 * n), F32),
            pltpu.VMEM((rows, 2 * n), F32),
            pltpu.VMEM((bsz, 2 * n), F32),
            pltpu.VMEM((bsz, D_LRU), F32),
            pltpu.VMEM(((CONV_W - 1) * bsz + rows, D_LRU), F32),
            pltpu.VMEM((rows, D_LRU), F32),
            pltpu.VMEM((rows, D_LRU), F32),
            pltpu.VMEM((rows, D_LRU), F32),
        ],
        compiler_params=_params(("arbitrary",)),
    )(pa, pg, pb, pc, *consts)


def _outproj_body(x_ref, y_ref, mod_ref, g_ref, wo_ref, wq_ref, x1_ref, h2_ref, q_ref):
    x = x_ref[...]
    tl, bsz, d = x.shape
    o = _dot(y_ref[...], wo_ref[...]).reshape(tl, bsz, d)
    x1 = x + mod_ref[2] * o
    x1_ref[...] = x1
    ms = jnp.mean(x1 * x1, axis=-1, keepdims=True)
    h2 = x1 * lax.rsqrt(ms + EPS) * g_ref[...]
    h2 = h2 * (1.0 + mod_ref[4]) + mod_ref[3]
    h2 = h2.reshape(tl * bsz, d).astype(BF16)
    h2_ref[...] = h2
    q_ref[...] = _dot(h2, wq_ref[...])


def _out_projection(x_tm, y, mod_l, g_ffn, w_out16, w_q16):
    seq, bsz, d = x_tm.shape
    tl = TOKEN_TILE // bsz
    tm = tl * bsz
    rows = seq * bsz
    nq = w_q16.shape[1]
    return pl.pallas_call(
        _outproj_body,
        grid=(seq // tl,),
        in_specs=[
            pl.BlockSpec((tl, bsz, d), lambda i: (i, 0, 0)),
            pl.BlockSpec((tm, y.shape[1]), lambda i: (i, 0)),
            pl.BlockSpec((N_MOD, bsz, d), lambda i: (0, 0, 0)),
            pl.BlockSpec((1, d), lambda i: (0, 0)),
            pl.BlockSpec(w_out16.shape, lambda i: (0, 0)),
            pl.BlockSpec(w_q16.shape, lambda i: (0, 0)),
        ],
        out_specs=[
            pl.BlockSpec((tl, bsz, d), lambda i: (i, 0, 0)),
            pl.BlockSpec((tm, d), lambda i: (i, 0)),
            pl.BlockSpec((tm, nq), lambda i: (i, 0)),
        ],
        out_shape=(
            jax.ShapeDtypeStruct((seq, bsz, d), F32),
            jax.ShapeDtypeStruct((rows, d), BF16),
            jax.ShapeDtypeStruct((rows, nq), F32),
        ),
        compiler_params=_params(("arbitrary",)),
    )(x_tm, y, mod_l, g_ffn.reshape(1, d), w_out16, w_q16)


def _bitonic_sort_network(n):
    ops = []
    k = 2
    while k <= n:
        j = k // 2
        while j >= 1:
            for i in range(n):
                l = i ^ j
                if l > i:
                    ops.append((i, l, (i & k) == 0))
            j //= 2
        k *= 2
    return ops


def _bitonic_merge_network(n):
    ops = []
    j = n // 2
    while j >= 1:
        for i in range(n):
            l = i ^ j
            if l > i:
                ops.append((i, l, True))
        j //= 2
    return ops


def _apply_network(vals, ops):
    vals = list(vals)
    for i, j, first_max in ops:
        hi = jnp.maximum(vals[i], vals[j])
        lo = jnp.minimum(vals[i], vals[j])
        vals[i], vals[j] = (hi, lo) if first_max else (lo, hi)
    return vals


def _top_sorted(s, k):
    n_keys = s.shape[0]
    n_slabs = n_keys // SUBLANES
    assert n_slabs == k
    slabs = [s[i * SUBLANES:(i + 1) * SUBLANES, :] for i in range(n_slabs)]
    slabs = _apply_network(slabs, _bitonic_sort_network(k))
    merge = _bitonic_merge_network(k)
    shift = SUBLANES // 2
    while shift >= 1:
        rolled = [pltpu.roll(v, shift, 0) for v in slabs]
        slabs = [jnp.maximum(slabs[i], rolled[k - 1 - i]) for i in range(k)]
        slabs = _apply_network(slabs, merge)
        shift //= 2
    return slabs


def _staircase(k):
    return [(x, y) for x in range(k) for y in range(k) if (x + 1) * (y + 1) <= k]


def _select_tree(masks, leaves):
    if not masks:
        return leaves[0]
    half = len(leaves) // 2
    return jnp.where(masks[0], _select_tree(masks[1:], leaves[half:]), _select_tree(masks[1:], leaves[:half]))


def _prefix_count(pred, vals):
    n = len(vals)
    full = pred(vals[n - 1])
    masks = []
    count = None
    stride = n // 2
    while stride >= 1:
        pivot = _select_tree(masks, [vals[base + stride - 1] for base in range(0, n, 2 * stride)])
        hit = pred(pivot)
        masks.append(hit)
        term = jnp.where(hit, float(stride), 0.0)
        count = term if count is None else count + term
        stride //= 2
    return jnp.where(full, float(n), count)


def _retrieval_body(q_ref, keys_ref, r1_ref, e1_ref, n0_ref, c0_ref, s_scr, a_scr, b_scr,
                    afull_scr, bfull_scr):
    k = PEER_TOPK
    half = PEER_DQ // 2
    tok = q_ref.shape[0]
    neg = jnp.full((SUBLANES, tok), -jnp.inf, F32)

    for h in range(PEER_HEADS):
        for p in range(2):
            off = h * PEER_DQ + p * half
            s = _dot_nt(keys_ref[p], q_ref[:, off:off + half].astype(BF16))
            s_scr[h, p] = s
            top = _top_sorted(s, k)
            dst, dst_full = (a_scr, afull_scr) if p == 0 else (b_scr, bfull_scr)
            for x in range(k):
                dst[x, h:h + 1, :] = top[x][0:1, :]
                dst_full[h, x] = top[x]

    a = [a_scr[x] for x in range(k)]
    b = [b_scr[y] for y in range(k)]
    pairs = _staircase(k)
    cands = [a[x] + b[y] for x, y in pairs]
    n_sort = 1
    while n_sort < len(cands):
        n_sort *= 2
    ranked = _apply_network(cands + [neg] * (n_sort - len(cands)), _bitonic_sort_network(n_sort))
    tau = ranked[k - 1]
    best = a[0] + b[0]
    z = jnp.zeros_like(tau)
    for cnd in cands:
        z = z + jnp.where(cnd >= tau, jnp.exp(cnd - best), 0.0)
    inv_z = 1.0 / z

    pack = 2 * SUBLANES
    for h in range(PEER_HEADS):
        tau_h = jnp.broadcast_to(tau[h:h + 1, :], (SUBLANES, tok))
        inv_z_h = jnp.broadcast_to(inv_z[h:h + 1, :], (SUBLANES, tok))
        b_h = [bfull_scr[h, y] for y in range(k)]
        a_first = afull_scr[h, 0]
        a_last = afull_scr[h, k - 1]
        for blk in range(N_KEYS // pack):
            ranks, exps = [], []
            for sub in range(pack // SUBLANES):
                rows = slice(blk * pack + sub * SUBLANES, blk * pack + (sub + 1) * SUBLANES)
                s0 = s_scr[h, 0, rows, :]
                s1 = s_scr[h, 1, rows, :]
                ranks.append(_prefix_count(lambda v: v > s1, b_h))
                cnt = _prefix_count(lambda v: s0 + v >= tau_h, b_h)
                n0_ref[h, rows, :] = jnp.where(s0 >= a_last, cnt, 0.0)
                c0_ref[h, rows, :] = jnp.exp(s0 - a_first) * inv_z_h
                exps.append(jnp.exp(s1 - b_h[0]))
            prow = slice(blk * pack, (blk + 1) * pack)
            r1_ref[h, prow, :] = jnp.concatenate(ranks, axis=0).astype(r1_ref.dtype)
            e1_ref[h, prow, :] = jnp.concatenate(exps, axis=0).astype(e1_ref.dtype)


def _retrieval(q, keys16):
    rows = q.shape[0]
    tok = RETR_TILE
    tab = lambda dt: jax.ShapeDtypeStruct((PEER_HEADS, N_KEYS, rows), dt)
    tspec = pl.BlockSpec((PEER_HEADS, N_KEYS, tok), lambda i: (0, 0, i))
    return pl.pallas_call(
        _retrieval_body,
        grid=(rows // tok,),
        in_specs=[
            pl.BlockSpec((tok, q.shape[1]), lambda i: (i, 0)),
            pl.BlockSpec(keys16.shape, lambda i: (0, 0, 0)),
        ],
        out_specs=[tspec, tspec, tspec, tspec],
        out_shape=(tab(BF16), tab(BF16), tab(F32), tab(F32)),
        scratch_shapes=[
            pltpu.VMEM((PEER_HEADS, 2, N_KEYS, tok), F32),
            pltpu.VMEM((PEER_TOPK, SUBLANES, tok), F32),
            pltpu.VMEM((PEER_TOPK, SUBLANES, tok), F32),
            pltpu.VMEM((PEER_HEADS, PEER_TOPK, SUBLANES, tok), F32),
            pltpu.VMEM((PEER_HEADS, PEER_TOPK, SUBLANES, tok), F32),
        ],
        compiler_params=_params(("arbitrary",)),
    )(q, keys16)


def _peer_body(h2_ref, r1_ref, e1_ref, n0_ref, c0_ref, u_ref, vt_ref, x1_ref, mod_ref,
               x2_ref, h2t_scr, acc_scr, a_scr, p_scr):
    j = pl.program_id(1)
    n_j = pl.num_programs(1)
    te = u_ref.shape[0]
    tok = h2_ref.shape[0]
    rows_per_step = te // N_KEYS
    pack = 2 * SUBLANES

    @pl.when(j == 0)
    def _():
        h2t_scr[...] = h2_ref[...].astype(F32).T.astype(BF16)
        acc_scr[...] = jnp.zeros_like(acc_scr)
        a_scr[...] = jnp.zeros_like(a_scr)

    zero = jnp.zeros((pack, PEER_PIECE), BF16)
    for c in range(tok // PEER_PIECE):
        cols = slice(c * PEER_PIECE, (c + 1) * PEER_PIECE)
        for r in range(rows_per_step):
            gates = [zero for _ in range(N_KEYS // pack)]
            for h in range(PEER_HEADS):
                nb = jnp.broadcast_to(n0_ref[h, r:r + 1, cols], (pack, PEER_PIECE)).astype(BF16)
                cb = jnp.broadcast_to(c0_ref[h, r:r + 1, cols], (pack, PEER_PIECE)).astype(BF16)
                for k in range(N_KEYS // pack):
                    rs = slice(k * pack, (k + 1) * pack)
                    sel = jnp.where(r1_ref[h, rs, cols] < nb, e1_ref[h, rs, cols], zero)
                    gates[k] = gates[k] + sel * cb
            for k in range(N_KEYS // pack):
                rs = slice(r * N_KEYS + k * pack, r * N_KEYS + (k + 1) * pack)
                p_scr[rs, cols] = gates[k] * _gelu(a_scr[rs, cols])
        acc_scr[:, cols] += _dot(vt_ref[...], p_scr[:, cols])
        a_scr[:, cols] = _dot(u_ref[...], h2t_scr[:, cols]).astype(BF16)

    @pl.when(j == n_j - 1)
    def _():
        tl, bsz, d = x1_ref.shape
        out = acc_scr[...].T.reshape(tl, bsz, d)
        x2_ref[...] = x1_ref[...] + mod_ref[5] * out


def _peer_dense(h2, tables, u16, vt16, x1, mod_l):
    seq, bsz, d = x1.shape
    tok = PEER_TOKEN_TILE
    tl = tok // bsz
    te = EXPERT_TILE
    n_tiles = u16.shape[0] // te
    rows_per_step = te // N_KEYS
    cur = lambda j: jnp.minimum(j, n_tiles - 1)
    prev = lambda j: jnp.maximum(j - 1, 0)
    tspec = pl.BlockSpec((PEER_HEADS, N_KEYS, tok), lambda i, j: (0, 0, i))
    rspec = pl.BlockSpec((PEER_HEADS, rows_per_step, tok), lambda i, j: (0, prev(j), i))
    return pl.pallas_call(
        _peer_body,
        grid=(seq // tl, n_tiles + 1),
        in_specs=[
            pl.BlockSpec((tok, d), lambda i, j: (i, 0)),
            tspec, tspec, rspec, rspec,
            pl.BlockSpec((te, d), lambda i, j: (cur(j), 0)),
            pl.BlockSpec((d, te), lambda i, j: (0, prev(j))),
            pl.BlockSpec((tl, bsz, d), lambda i, j: (i, 0, 0)),
            pl.BlockSpec((N_MOD, bsz, d), lambda i, j: (0, 0, 0)),
        ],
        out_specs=pl.BlockSpec((tl, bsz, d), lambda i, j: (i, 0, 0)),
        out_shape=jax.ShapeDtypeStruct((seq, bsz, d), F32),
        scratch_shapes=[
            pltpu.VMEM((d, tok), BF16),
            pltpu.VMEM((d, tok), F32),
            pltpu.VMEM((te, tok), BF16),
            pltpu.VMEM((te, tok), BF16),
        ],
        compiler_params=_params(("arbitrary", "arbitrary")),
    )(h2, *tables, u16, vt16, x1, mod_l)


def _final_body(x_ref, g_ref, o_ref):
    x = x_ref[...]
    o_ref[...] = x * lax.rsqrt(jnp.mean(x * x, axis=-1, keepdims=True) + EPS) * g_ref[...]


def _final_norm(x_tm, g):
    seq, bsz, d = x_tm.shape
    tl = TOKEN_TILE // bsz
    return pl.pallas_call(
        _final_body,
        grid=(seq // tl,),
        in_specs=[pl.BlockSpec((tl, bsz, d), lambda i: (i, 0, 0)),
                  pl.BlockSpec((1, d), lambda i: (0, 0))],
        out_specs=pl.BlockSpec((tl, bsz, d), lambda i: (i, 0, 0)),
        out_shape=jax.ShapeDtypeStruct((seq, bsz, d), F32),
        compiler_params=_params(("arbitrary",)),
    )(x_tm, g.reshape(1, d))


def _block_diag(blocks):
    n, r, c = blocks.shape
    eye = jnp.eye(n, dtype=blocks.dtype)
    return (blocks[:, :, None, :] * eye[:, None, :, None]).reshape(n * r, n * c)


def kernel(x, c, w_mod, b_mod, g_mix, w_in, hgrn_lb_logits, s5_a_re, s5_a_im, s5_b_re, s5_b_im, s5_c_re, s5_c_im, s5_log_dt, s5_d, s5_w_glu, s5_b_glu, lru_conv_w, lru_conv_b, lru_w_a, lru_b_a, lru_w_x, lru_b_x, lru_lambda, g_branch, w_out, g_ffn, peer_w_q, peer_sub_keys, peer_u, peer_v, g_final):
    bsz, seq, d = x.shape
    depth = w_in.shape[0]
    assert bsz == SUBLANES and seq % CHUNK_T == 0 and (seq * bsz) % TOKEN_TILE == 0

    x_tm = x.transpose(1, 0, 2)
    mod = _modulation(c, w_mod, b_mod)
    mod = mod.reshape(depth, bsz, N_MOD, d).transpose(0, 2, 1, 3)
    lb_all, ab_re, ab_im, bb_re, bb_im = _prepare_params(
        hgrn_lb_logits, s5_a_re, s5_a_im, s5_log_dt, s5_b_re, s5_b_im)
    tmat = jnp.asarray(_hgrn_exponent_matrix(CHUNK_T), BF16)
    row = lambda a: a.reshape(1, -1)

    for l in range(depth):
        bbd = jnp.concatenate([
            _block_diag(bb_re[l].reshape(S5_GROUP, S5_GROUPS, S5_STATE).transpose(1, 0, 2)),
            _block_diag(bb_im[l].reshape(S5_GROUP, S5_GROUPS, S5_STATE).transpose(1, 0, 2)),
        ], axis=1).astype(BF16)
        cre = _block_diag(s5_c_re[l].transpose(0, 2, 1)).astype(BF16)
        cim = _block_diag(s5_c_im[l].transpose(0, 2, 1)).astype(BF16)
        s5p = (ab_re[l], ab_im[l], bbd, cre, cim, row(s5_d[l]),
               s5_w_glu[l].astype(BF16), row(s5_b_glu[l]))
        lrup = (lru_conv_w[l], row(lru_conv_b[l]),
                _block_diag(lru_w_a[l]).astype(BF16), row(lru_b_a[l]),
                _block_diag(lru_w_x[l]).astype(BF16), row(lru_b_x[l]), row(lru_lambda[l]))

        pa, pg, pb, pc = _in_projection(x_tm, mod[l], g_mix[l], w_in[l].astype(BF16))
        y = _mixer(pa, pg, pb, pc, bsz, tmat, lb_all[l:l + 1], s5p, lrup, row(g_branch[l]))
        x1, h2, q = _out_projection(x_tm, y, mod[l], g_ffn[l],
                                    w_out[l].astype(BF16), peer_w_q[l].astype(BF16))
        tables = _retrieval(q, peer_sub_keys[l].astype(BF16))
        x_tm = _peer_dense(h2, tables, peer_u[l].astype(BF16),
                           peer_v[l].astype(BF16).T, x1, mod[l])

    out = _final_norm(x_tm, g_final)
    return out.transpose(1, 0, 2)
```

```python
import functools

import numpy as np
import jax
import jax.numpy as jnp
from jax import lax
from jax.experimental import pallas as pl
from jax.experimental.pallas import tpu as pltpu

F32 = jnp.float32
BF16 = jnp.bfloat16

EPS = 1e-6
N_MOD = 6
HGRN_HEADS = 4
HGRN_DK = 128
D_HGRN = HGRN_HEADS * HGRN_DK
S5_GROUP = 16
S5_GROUPS = 16
S5_STATE = 64
D_S5 = S5_GROUP * S5_GROUPS
N_S5_STATE = S5_GROUPS * S5_STATE
LRU_BLOCKS = 4
LRU_BLOCK = 64
D_LRU = LRU_BLOCKS * LRU_BLOCK
CONV_W = 4
LRU_C = 8.0
PEER_HEADS = 8
N_KEYS = 128
PEER_DQ = 256
PEER_TOPK = 16

SUBLANES = 8
LANES = 128
VMEM_LIMIT_BYTES = 56 * 1024 * 1024

CHUNK_T = 64
TOKEN_TILE = 512
EXPERT_TILE = 2048
PEER_SUBTILE = 1024
PEER_TOKEN_TILE = 1024
PEER_PIECE = 256
RETR_TILE = 256
MOD_COL_TILE = 1536


def _sigmoid(x):
    return 1.0 / (1.0 + jnp.exp(-x))


def _silu(x):
    return x * _sigmoid(x)


def _gelu(x):
    return 0.5 * x * (1.0 + jnp.tanh(0.7978845608028654 * (x + 0.044715 * (x * x * x))))


def _dot(a, b):
    return jnp.dot(a, b, preferred_element_type=F32)


def _dot_nt(a, b):
    return lax.dot_general(a, b, (((1,), (1,)), ((), ())), preferred_element_type=F32)


def _dot_tn(a, b):
    return lax.dot_general(a, b, (((0,), (0,)), ((), ())), preferred_element_type=F32)


def _split2(x):
    hi = x.astype(BF16)
    lo = (x - hi.astype(F32)).astype(BF16)
    return hi, lo


def _params(semantics):
    return pltpu.CompilerParams(dimension_semantics=semantics, vmem_limit_bytes=VMEM_LIMIT_BYTES)


def _mod_body(c_ref, w_ref, b_ref, o_ref):
    cond = _silu(c_ref[...])
    c_hi, c_lo = _split2(cond)
    w_hi, w_lo = _split2(w_ref[0])
    o_ref[0] = _dot(c_hi, w_hi) + _dot(c_hi, w_lo) + _dot(c_lo, w_hi) + b_ref[0]


def _modulation(c, w_mod, b_mod):
    depth, d, n = w_mod.shape
    bsz = c.shape[0]
    tn = MOD_COL_TILE
    return pl.pallas_call(
        _mod_body,
        grid=(depth, n // tn),
        in_specs=[
            pl.BlockSpec((bsz, d), lambda l, j: (0, 0)),
            pl.BlockSpec((1, d, tn), lambda l, j: (l, 0, j)),
            pl.BlockSpec((1, 1, tn), lambda l, j: (l, 0, j)),
        ],
        out_specs=pl.BlockSpec((1, bsz, tn), lambda l, j: (l, 0, j)),
        out_shape=jax.ShapeDtypeStruct((depth, bsz, n), F32),
        compiler_params=_params(("arbitrary", "arbitrary")),
    )(c, w_mod, b_mod.reshape(depth, 1, n))


def _prep_body(lbl_ref, are_ref, aim_ref, ldt_ref, bre_ref, bim_ref,
               lb_ref, abr_ref, abi_ref, bbr_ref, bbi_ref):
    logits = lbl_ref[...]
    depth = logits.shape[0]
    rows = [logits[l:l + 1] for l in range(depth)]
    mx = functools.reduce(jnp.maximum, rows)
    ex = [jnp.exp(r - mx) for r in rows]
    den = functools.reduce(lambda a, b: a + b, ex)
    soft = [e / den for e in ex]
    run = soft[0]
    for l in range(depth):
        if l > 0:
            run = run + soft[l]
        lb_ref[l:l + 1, :] = run - soft[0]

    lam_re = jnp.minimum(are_ref[...], -1e-4)
    lam_im = aim_ref[...]
    dt = jnp.exp(ldt_ref[...])
    mag = jnp.exp(lam_re * dt)
    ab_re = mag * jnp.cos(lam_im * dt)
    ab_im = mag * jnp.sin(lam_im * dt)
    den2 = lam_re * lam_re + lam_im * lam_im
    nr = ab_re - 1.0
    z_re = (nr * lam_re + ab_im * lam_im) / den2
    z_im = (ab_im * lam_re - nr * lam_im) / den2
    abr_ref[...] = ab_re
    abi_ref[...] = ab_im
    br = bre_ref[...]
    bi = bim_ref[...]
    bbr_ref[...] = z_re * br - z_im * bi
    bbi_ref[...] = z_re * bi + z_im * br


def _prepare_params(lb_logits, a_re, a_im, log_dt, b_re, b_im):
    depth = lb_logits.shape[0]
    n = N_S5_STATE
    flat = lambda a: a.reshape(depth, 1, n)
    ldt = jnp.broadcast_to(log_dt[:, :, None], (depth, S5_GROUPS, S5_STATE))
    bt = lambda b: b.transpose(0, 3, 1, 2).reshape(depth, S5_GROUP, n)
    out_shape = (
        jax.ShapeDtypeStruct((depth, D_HGRN), F32),
        jax.ShapeDtypeStruct((depth, 1, n), F32),
        jax.ShapeDtypeStruct((depth, 1, n), F32),
        jax.ShapeDtypeStruct((depth, S5_GROUP, n), F32),
        jax.ShapeDtypeStruct((depth, S5_GROUP, n), F32),
    )
    return pl.pallas_call(_prep_body, out_shape=out_shape)(
        lb_logits, flat(a_re), flat(a_im), flat(ldt), bt(b_re), bt(b_im))


def _inproj_body(has_peer, *refs):
    if has_peer:
        (x_ref, po_ref, modp_ref, mod_ref, g_ref, w_ref,
         xo_ref, pa_ref, pg_ref, pb_ref, pc_ref) = refs
        x = x_ref[...] + modp_ref[5] * po_ref[...]
        xo_ref[...] = x
    else:
        x_ref, mod_ref, g_ref, w_ref, pa_ref, pg_ref, pb_ref, pc_ref = refs
        x = x_ref[...]
    tl, bsz, d = x.shape
    ms = jnp.mean(x * x, axis=-1, keepdims=True)
    xn = x * lax.rsqrt(ms + EPS) * g_ref[...]
    h = xn * (1.0 + mod_ref[1]) + mod_ref[0]
    h = h.reshape(tl * bsz, d).astype(BF16)
    na = pa_ref.shape[0] * LANES
    ng = pg_ref.shape[1]
    nb = pb_ref.shape[1]
    pa = _dot(h, w_ref[:, 0:na])
    for cg in range(pa_ref.shape[0]):
        pa_ref[cg] = pa[:, cg * LANES:(cg + 1) * LANES]
    pg_ref[...] = _dot(h, w_ref[:, na:na + ng])
    pb_ref[...] = _dot(h, w_ref[:, na + ng:na + ng + nb])
    pc_ref[...] = _dot(h, w_ref[:, na + ng + nb:])


def _in_projection(x_tm, peer, mod_l, g_mix, w_in16):
    seq, bsz, d = x_tm.shape
    tl = TOKEN_TILE // bsz
    tm = tl * bsz
    rows = seq * bsz
    ncg = 3 * D_HGRN // LANES
    ng, nb, nc = D_HGRN, D_S5, 2 * D_LRU
    xspec = pl.BlockSpec((tl, bsz, d), lambda i: (i, 0, 0))
    mspec = pl.BlockSpec((N_MOD, bsz, d), lambda i: (0, 0, 0))
    has_peer = peer is not None
    return pl.pallas_call(
        functools.partial(_inproj_body, has_peer),
        grid=(seq // tl,),
        in_specs=[xspec] + ([xspec, mspec] if has_peer else []) + [
            mspec,
            pl.BlockSpec((1, d), lambda i: (0, 0)),
            pl.BlockSpec(w_in16.shape, lambda i: (0, 0)),
        ],
        out_specs=([xspec] if has_peer else []) + [
            pl.BlockSpec((ncg, tm, LANES), lambda i: (0, i, 0)),
            pl.BlockSpec((tm, ng), lambda i: (i, 0)),
            pl.BlockSpec((tm, nb), lambda i: (i, 0)),
            pl.BlockSpec((tm, nc), lambda i: (i, 0)),
        ],
        out_shape=tuple(([jax.ShapeDtypeStruct((seq, bsz, d), F32)] if has_peer else []) + [
            jax.ShapeDtypeStruct((ncg, rows, LANES), F32),
            jax.ShapeDtypeStruct((rows, ng), F32),
            jax.ShapeDtypeStruct((rows, nb), F32),
            jax.ShapeDtypeStruct((rows, nc), F32),
        ]),
        compiler_params=_params(("arbitrary",)),
    )(x_tm, *(peer if has_peer else ()), mod_l, g_mix.reshape(1, d), w_in16)


def _hgrn_exponent_matrix(chunk):
    n_levels = int(np.log2(chunk))
    mat = np.zeros((n_levels + 2, chunk, chunk), np.float32)
    u = np.arange(chunk)
    for k in range(n_levels):
        g = chunk >> (k + 1)
        m = 2 * g
        for t in range(chunk):
            r = (t // m) * m + g - 1
            if t % m >= g:
                mat[k, t] = (u > r) & (u <= t)
            else:
                mat[k, t] = (u > t) & (u <= r)
    for t in range(chunk):
        mat[n_levels, t] = u <= t
        mat[n_levels + 1, t] = u > t
    return mat.reshape((n_levels + 2) * chunk, chunk)


def _mixer_body(pa_ref, pg_ref, pb_ref, pc_ref, tmat_ref, lb_ref,
                ar_ref, ai_ref, bbd_ref, cre_ref, cim_ref, dsk_ref, wglu_ref, bglu_ref,
                cw_ref, cb_ref, wa_ref, ba_ref, wx_ref, bx_ref, lam_ref, gbr_ref,
                y_ref,
                st_ref, o_scr, bu_scr, xs_scr, s5_state, lru_state, xext, la_scr, lb_scr, hs_scr):
    rows = pa_ref.shape[1]
    bsz = s5_state.shape[0]
    chunk = rows // bsz
    n_levels = tmat_ref.shape[0] // chunk - 2
    dk = HGRN_DK

    @pl.when(pl.program_id(0) == 0)
    def _():
        st_ref[...] = jnp.zeros_like(st_ref)
        s5_state[...] = jnp.zeros_like(s5_state)
        lru_state[...] = jnp.zeros_like(lru_state)
        xext[0:(CONV_W - 1) * bsz, :] = jnp.zeros(((CONV_W - 1) * bsz, D_LRU), F32)

    ti = lax.broadcasted_iota(jnp.int32, (chunk, chunk), 0)
    si = lax.broadcasted_iota(jnp.int32, (chunk, chunk), 1)
    masks = []
    for k in range(n_levels):
        g = chunk >> (k + 1)
        m = 2 * g
        same = (ti // m) == (si // m)
        masks.append(jnp.where(same, jnp.where((ti % m) >= g, jnp.where((si % m) < g, 1.0, 0.0), 0.0), 0.0))
    diag = jnp.where(ti == si, 1.0, 0.0)
    lb = lb_ref[...]
    tmat = tmat_ref[...]

    def hgrn_batch(b, carry):
        rws = pl.ds(b, chunk, stride=bsz)
        heads = lambda first: jnp.concatenate(
            [pa_ref[first + h, rws, :] for h in range(HGRN_HEADS)], axis=1)
        qraw = heads(0)
        fpre = heads(HGRN_HEADS)
        v16 = heads(2 * HGRN_HEADS).astype(BF16)
        qa = _silu(qraw)
        f = lb + (1.0 - lb) * _sigmoid(fpre)
        logf = jnp.log(f)
        kk = 1.0 - f
        l_hi, l_lo = _split2(logf)
        decay = jnp.exp(_dot(tmat, l_hi) + _dot(tmat, l_lo))
        cum = decay[n_levels * chunk:(n_levels + 1) * chunk]
        qb16 = (qa * cum).astype(BF16)
        ks16 = (kk * decay[(n_levels + 1) * chunk:]).astype(BF16)
        total = cum[chunk - 1:chunk, :]
        qa16 = qa.astype(BF16)
        kk16 = kk.astype(BF16)
        for h in range(HGRN_HEADS):
            sl = slice(h * dk, (h + 1) * dk)
            scores = diag * _dot_nt(qa16[:, sl], kk16[:, sl])
            for k in range(n_levels):
                dl = decay[k * chunk:(k + 1) * chunk, sl]
                ql = (qa[:, sl] * dl).astype(BF16)
                kl = (kk[:, sl] * dl).astype(BF16)
                scores = scores + masks[k] * _dot_nt(ql, kl)
            st = st_ref[b, h]
            o = _dot(scores.astype(BF16), v16[:, sl]) + _dot_nt(qb16[:, sl], st.astype(BF16))
            o_scr[h, rws, :] = o
            st_ref[b, h] = st * total[:, sl] + _dot_tn(v16[:, sl], ks16[:, sl])
        return carry

    lax.fori_loop(0, bsz, hgrn_batch, 0, unroll=2)

    gate = _silu(pg_ref[...])
    ya = []
    ssq = None
    for h in range(HGRN_HEADS):
        sl = slice(h * dk, (h + 1) * dk)
        oh = o_scr[h]
        oh = oh * lax.rsqrt(jnp.mean(oh * oh, axis=-1, keepdims=True) + EPS) * gate[:, sl]
        ya.append(oh)
        s = jnp.sum(oh * oh, axis=-1, keepdims=True)
        ssq = s if ssq is None else ssq + s
    scale = lax.rsqrt(ssq * (1.0 / D_HGRN) + EPS)
    for h in range(HGRN_HEADS):
        sl = slice(h * dk, (h + 1) * dk)
        y_ref[:, sl] = (ya[h] * scale * gbr_ref[:, sl]).astype(y_ref.dtype)

    n = N_S5_STATE
    u = pb_ref[...]
    bu_scr[...] = _dot(u.astype(BF16), bbd_ref[...])
    ar = jnp.broadcast_to(ar_ref[...], (bsz, n))
    ai = jnp.broadcast_to(ai_ref[...], (bsz, n))

    def s5_step(t, st):
        xr, xi = st
        r = pl.ds(pl.multiple_of(t * bsz, bsz), bsz)
        nxr = ar * xr - ai * xi + bu_scr[r, 0:n]
        nxi = ar * xi + ai * xr + bu_scr[r, n:2 * n]
        xs_scr[r, 0:n] = nxr
        xs_scr[r, n:2 * n] = nxi
        return nxr, nxi

    xr, xi = lax.fori_loop(0, chunk, s5_step, (s5_state[:, 0:n], s5_state[:, n:2 * n]))
    s5_state[:, 0:n] = xr
    s5_state[:, n:2 * n] = xi
    yb = (_dot(xs_scr[:, 0:n].astype(BF16), cre_ref[...])
          - _dot(xs_scr[:, n:2 * n].astype(BF16), cim_ref[...]))
    yb = yb + dsk_ref[...] * u
    z = _gelu(yb)
    yb = z * _sigmoid(_dot(z.astype(BF16), wglu_ref[...]) + bglu_ref[...])
    yb = yb * lax.rsqrt(jnp.mean(yb * yb, axis=-1, keepdims=True) + EPS)
    y_ref[:, D_HGRN:D_HGRN + D_S5] = (yb * gbr_ref[:, D_HGRN:D_HGRN + D_S5]).astype(y_ref.dtype)

    tail = (CONV_W - 1) * bsz
    xin = pc_ref[:, 0:D_LRU]
    xext[tail:tail + rows, :] = xin
    xc = cb_ref[...]
    for w in range(CONV_W):
        xc = xc + cw_ref[w:w + 1, :] * xext[w * bsz:w * bsz + rows, :]
    xext[0:tail, :] = xin[rows - tail:rows, :]
    xc16 = xc.astype(BF16)
    rg = _sigmoid(_dot(xc16, wa_ref[...]) + ba_ref[...])
    ig = _sigmoid(_dot(xc16, wx_ref[...]) + bx_ref[...])
    nl = -lam_ref[...]
    softplus = jnp.maximum(nl, 0.0) + jnp.log(1.0 + jnp.exp(-jnp.abs(nl)))
    log_a = (-LRU_C) * rg * softplus
    la_scr[...] = jnp.exp(log_a)
    lb_scr[...] = jnp.sqrt(1.0 - jnp.exp(2.0 * log_a)) * (ig * xc)

    def lru_step(t, hstate):
        r = pl.ds(pl.multiple_of(t * bsz, bsz), bsz)
        hn = la_scr[r, :] * hstate + lb_scr[r, :]
        hs_scr[r, :] = hn
        return hn

    lru_state[...] = lax.fori_loop(0, chunk, lru_step, lru_state[...])
    yc = hs_scr[...] * _gelu(pc_ref[:, D_LRU:2 * D_LRU])
    yc = yc * lax.rsqrt(jnp.mean(yc * yc, axis=-1, keepdims=True) + EPS)
    y_ref[:, D_HGRN + D_S5:] = (yc * gbr_ref[:, D_HGRN + D_S5:]).astype(y_ref.dtype)


def _mixer(pa, pg, pb, pc, bsz, tmat, lb, s5p, lrup, g_branch):
    rows_total = pg.shape[0]
    rows = CHUNK_T * bsz
    d_mix = D_HGRN + D_S5 + D_LRU
    n = N_S5_STATE
    full = lambda a: pl.BlockSpec(a.shape, lambda i, nd=a.ndim: (0,) * nd)
    consts = [tmat, lb, *s5p, *lrup, g_branch]
    return pl.pallas_call(
        _mixer_body,
        grid=(rows_total // rows,),
        in_specs=[
            pl.BlockSpec((pa.shape[0], rows, LANES), lambda i: (0, i, 0)),
            pl.BlockSpec((rows, pg.shape[1]), lambda i: (i, 0)),
            pl.BlockSpec((rows, pb.shape[1]), lambda i: (i, 0)),
            pl.BlockSpec((rows, pc.shape[1]), lambda i: (i, 0)),
        ] + [full(a) for a in consts],
        out_specs=pl.BlockSpec((rows, d_mix), lambda i: (i, 0)),
        out_shape=jax.ShapeDtypeStruct((rows_total, d_mix), BF16),
        scratch_shapes=[
            pltpu.VMEM((bsz, HGRN_HEADS, HGRN_DK, HGRN_DK), F32),
            pltpu.VMEM((HGRN_HEADS, rows, HGRN_DK), F32),
            pltpu.VMEM((rows, 2 * n), F32),
            pltpu.VMEM((rows, 2 * n), F32),
            pltpu.VMEM((bsz, 2 * n), F32),
            pltpu.VMEM((bsz, D_LRU), F32),
            pltpu.VMEM(((CONV_W - 1) * bsz + rows, D_LRU), F32),
            pltpu.VMEM((rows, D_LRU), F32),
            pltpu.VMEM((rows, D_LRU), F32),
            pltpu.VMEM((rows, D_LRU), F32),
        ],
        compiler_params=_params(("arbitrary",)),
    )(pa, pg, pb, pc, *consts)


def _outproj_body(x_ref, y_ref, mod_ref, g_ref, wo_ref, wq_ref, x1_ref, h2_ref, q_ref):
    x = x_ref[...]
    tl, bsz, d = x.shape
    o = _dot(y_ref[...], wo_ref[...]).reshape(tl, bsz, d)
    x1 = x + mod_ref[2] * o
    x1_ref[...] = x1
    ms = jnp.mean(x1 * x1, axis=-1, keepdims=True)
    h2 = x1 * lax.rsqrt(ms + EPS) * g_ref[...]
    h2 = h2 * (1.0 + mod_ref[4]) + mod_ref[3]
    h2 = h2.reshape(tl * bsz, d).astype(BF16)
    h2_ref[...] = h2
    q_ref[...] = _dot(h2, wq_ref[...])


def _out_projection(x_tm, y, mod_l, g_ffn, w_out16, w_q16):
    seq, bsz, d = x_tm.shape
    tl = TOKEN_TILE // bsz
    tm = tl * bsz
    rows = seq * bsz
    nq = w_q16.shape[1]
    return pl.pallas_call(
        _outproj_body,
        grid=(seq // tl,),
        in_specs=[
            pl.BlockSpec((tl, bsz, d), lambda i: (i, 0, 0)),
            pl.BlockSpec((tm, y.shape[1]), lambda i: (i, 0)),
            pl.BlockSpec((N_MOD, bsz, d), lambda i: (0, 0, 0)),
            pl.BlockSpec((1, d), lambda i: (0, 0)),
            pl.BlockSpec(w_out16.shape, lambda i: (0, 0)),
            pl.BlockSpec(w_q16.shape, lambda i: (0, 0)),
        ],
        out_specs=[
            pl.BlockSpec((tl, bsz, d), lambda i: (i, 0, 0)),
            pl.BlockSpec((tm, d), lambda i: (i, 0)),
            pl.BlockSpec((tm, nq), lambda i: (i, 0)),
        ],
        out_shape=(
            jax.ShapeDtypeStruct((seq, bsz, d), F32),
            jax.ShapeDtypeStruct((rows, d), BF16),
            jax.ShapeDtypeStruct((rows, nq), F32),
        ),
        compiler_params=_params(("arbitrary",)),
    )(x_tm, y, mod_l, g_ffn.reshape(1, d), w_out16, w_q16)


def _bitonic_sort_network(n):
    ops = []
    k = 2
    while k <= n:
        j = k // 2
        while j >= 1:
            for i in range(n):
                l = i ^ j
                if l > i:
                    ops.append((i, l, (i & k) == 0))
            j //= 2
        k *= 2
    return ops


def _bitonic_merge_network(n):
    ops = []
    j = n // 2
    while j >= 1:
        for i in range(n):
            l = i ^ j
            if l > i:
                ops.append((i, l, True))
        j //= 2
    return ops


def _apply_network(vals, ops):
    vals = list(vals)
    for i, j, first_max in ops:
        hi = jnp.maximum(vals[i], vals[j])
        lo = jnp.minimum(vals[i], vals[j])
        vals[i], vals[j] = (hi, lo) if first_max else (lo, hi)
    return vals


def _top_sorted(s, k):
    n_keys = s.shape[0]
    n_slabs = n_keys // SUBLANES
    assert n_slabs == k
    slabs = [s[i * SUBLANES:(i + 1) * SUBLANES, :] for i in range(n_slabs)]
    slabs = _apply_network(slabs, _bitonic_sort_network(k))
    merge = _bitonic_merge_network(k)
    shift = SUBLANES // 2
    while shift >= 1:
        rolled = [pltpu.roll(v, shift, 0) for v in slabs]
        slabs = [jnp.maximum(slabs[i], rolled[k - 1 - i]) for i in range(k)]
        slabs = _apply_network(slabs, merge)
        shift //= 2
    return slabs


def _staircase(k):
    return [(x, y) for x in range(k) for y in range(k) if (x + 1) * (y + 1) <= k]


def _select_tree(masks, leaves):
    if not masks:
        return leaves[0]
    half = len(leaves) // 2
    return jnp.where(masks[0], _select_tree(masks[1:], leaves[half:]), _select_tree(masks[1:], leaves[:half]))


def _prefix_count(pred, vals):
    n = len(vals)
    full = pred(vals[n - 1])
    masks = []
    count = None
    stride = n // 2
    while stride >= 1:
        pivot = _select_tree(masks, [vals[base + stride - 1] for base in range(0, n, 2 * stride)])
        hit = pred(pivot)
        masks.append(hit)
        term = jnp.where(hit, float(stride), 0.0)
        count = term if count is None else count + term
        stride //= 2
    return jnp.where(full, float(n), count)


def _retrieval_body(q_ref, keys_ref, r1_ref, e1_ref, n0_ref, c0_ref, s_scr, a_scr, b_scr,
                    afull_scr, bfull_scr):
    k = PEER_TOPK
    half = PEER_DQ // 2
    tok = q_ref.shape[0]
    neg = jnp.full((SUBLANES, tok), -jnp.inf, F32)

    for h in range(PEER_HEADS):
        for p in range(2):
            off = h * PEER_DQ + p * half
            s = _dot_nt(keys_ref[p], q_ref[:, off:off + half].astype(BF16))
            s_scr[h, p] = s
            top = _top_sorted(s, k)
            dst, dst_full = (a_scr, afull_scr) if p == 0 else (b_scr, bfull_scr)
            for x in range(k):
                dst[x, h:h + 1, :] = top[x][0:1, :]
                dst_full[h, x] = top[x]

    a = [a_scr[x] for x in range(k)]
    b = [b_scr[y] for y in range(k)]
    pairs = _staircase(k)
    cands = [a[x] + b[y] for x, y in pairs]
    n_sort = 1
    while n_sort < len(cands):
        n_sort *= 2
    ranked = _apply_network(cands + [neg] * (n_sort - len(cands)), _bitonic_sort_network(n_sort))
    tau = ranked[k - 1]
    best = a[0] + b[0]
    z = jnp.zeros_like(tau)
    for cnd in cands:
        z = z + jnp.where(cnd >= tau, jnp.exp(cnd - best), 0.0)
    inv_z = 1.0 / z

    pack = 2 * SUBLANES
    for h in range(PEER_HEADS):
        tau_h = jnp.broadcast_to(tau[h:h + 1, :], (SUBLANES, tok))
        inv_z_h = jnp.broadcast_to(inv_z[h:h + 1, :], (SUBLANES, tok))
        b_h = [bfull_scr[h, y] for y in range(k)]
        a_first = afull_scr[h, 0]
        a_last = afull_scr[h, k - 1]
        for blk in range(N_KEYS // pack):
            ranks, exps = [], []
            for sub in range(pack // SUBLANES):
                rows = slice(blk * pack + sub * SUBLANES, blk * pack + (sub + 1) * SUBLANES)
                s0 = s_scr[h, 0, rows, :]
                s1 = s_scr[h, 1, rows, :]
                ranks.append(_prefix_count(lambda v: v > s1, b_h))
                cnt = _prefix_count(lambda v: s0 + v >= tau_h, b_h)
                n0_ref[h, rows, :] = jnp.where(s0 >= a_last, cnt, 0.0)
                c0_ref[h, rows, :] = jnp.exp(s0 - a_first) * inv_z_h
                exps.append(jnp.exp(s1 - b_h[0]))
            prow = slice(blk * pack, (blk + 1) * pack)
            r1_ref[h, prow, :] = jnp.concatenate(ranks, axis=0).astype(r1_ref.dtype)
            e1_ref[h, prow, :] = jnp.concatenate(exps, axis=0).astype(e1_ref.dtype)


def _retrieval(q, keys16):
    rows = q.shape[0]
    tok = RETR_TILE
    tab = lambda dt: jax.ShapeDtypeStruct((PEER_HEADS, N_KEYS, rows), dt)
    tspec = pl.BlockSpec((PEER_HEADS, N_KEYS, tok), lambda i: (0, 0, i))
    return pl.pallas_call(
        _retrieval_body,
        grid=(rows // tok,),
        in_specs=[
            pl.BlockSpec((tok, q.shape[1]), lambda i: (i, 0)),
            pl.BlockSpec(keys16.shape, lambda i: (0, 0, 0)),
        ],
        out_specs=[tspec, tspec, tspec, tspec],
        out_shape=(tab(BF16), tab(BF16), tab(F32), tab(F32)),
        scratch_shapes=[
            pltpu.VMEM((PEER_HEADS, 2, N_KEYS, tok), F32),
            pltpu.VMEM((PEER_TOPK, SUBLANES, tok), F32),
            pltpu.VMEM((PEER_TOPK, SUBLANES, tok), F32),
            pltpu.VMEM((PEER_HEADS, PEER_TOPK, SUBLANES, tok), F32),
            pltpu.VMEM((PEER_HEADS, PEER_TOPK, SUBLANES, tok), F32),
        ],
        compiler_params=_params(("arbitrary",)),
    )(q, keys16)


def _peer_body(h2_ref, r1_ref, e1_ref, n0_ref, c0_ref, u_ref, vt_ref,
               po_ref, h2t_scr, acc_scr, a_scr, p_scr):
    j = pl.program_id(1)
    n_j = pl.num_programs(1)
    te = u_ref.shape[0]
    tok = h2_ref.shape[0]
    sub_rows = PEER_SUBTILE // N_KEYS
    pack = 2 * SUBLANES

    @pl.when(j == 0)
    def _():
        h2t_scr[...] = h2_ref[...].astype(F32).T.astype(BF16)
        acc_scr[...] = jnp.zeros_like(acc_scr)
        a_scr[...] = jnp.zeros_like(a_scr)

    zero = jnp.zeros((pack, PEER_PIECE), BF16)
    for s in range(te // PEER_SUBTILE):
        sub = slice(s * PEER_SUBTILE, (s + 1) * PEER_SUBTILE)
        for c in range(tok // PEER_PIECE):
            cols = slice(c * PEER_PIECE, (c + 1) * PEER_PIECE)
            for r in range(sub_rows):
                key_row = s * sub_rows + r
                gates = [zero for _ in range(N_KEYS // pack)]
                for h in range(PEER_HEADS):
                    nb = jnp.broadcast_to(n0_ref[h, key_row:key_row + 1, cols],
                                          (pack, PEER_PIECE)).astype(BF16)
                    cb = jnp.broadcast_to(c0_ref[h, key_row:key_row + 1, cols],
                                          (pack, PEER_PIECE)).astype(BF16)
                    for k in range(N_KEYS // pack):
                        rs = slice(k * pack, (k + 1) * pack)
                        sel = jnp.where(r1_ref[h, rs, cols] < nb, e1_ref[h, rs, cols], zero)
                        gates[k] = gates[k] + sel * cb
                for k in range(N_KEYS // pack):
                    rs = slice(key_row * N_KEYS + k * pack, key_row * N_KEYS + (k + 1) * pack)
                    p_scr[rs, cols] = gates[k] * _gelu(a_scr[rs, cols])
            acc_scr[:, cols] += _dot(vt_ref[:, sub], p_scr[sub, cols])
            a_scr[sub, cols] = _dot(u_ref[sub, :], h2t_scr[:, cols]).astype(BF16)

    @pl.when(j == n_j - 1)
    def _():
        tl, bsz, d = po_ref.shape
        po_ref[...] = acc_scr[...].T.reshape(tl, bsz, d)


def _peer_dense(h2, tables, u16, vt16, bsz):
    rows, d = h2.shape
    seq = rows // bsz
    tok = PEER_TOKEN_TILE
    tl = tok // bsz
    te = EXPERT_TILE
    n_tiles = u16.shape[0] // te
    rows_per_step = te // N_KEYS
    cur = lambda j: jnp.minimum(j, n_tiles - 1)
    prev = lambda j: jnp.maximum(j - 1, 0)
    tspec = pl.BlockSpec((PEER_HEADS, N_KEYS, tok), lambda i, j: (0, 0, i))
    rspec = pl.BlockSpec((PEER_HEADS, rows_per_step, tok), lambda i, j: (0, prev(j), i))
    return pl.pallas_call(
        _peer_body,
        grid=(seq // tl, n_tiles + 1),
        in_specs=[
            pl.BlockSpec((tok, d), lambda i, j: (i, 0)),
            tspec, tspec, rspec, rspec,
            pl.BlockSpec((te, d), lambda i, j: (cur(j), 0)),
            pl.BlockSpec((d, te), lambda i, j: (0, prev(j))),
        ],
        out_specs=pl.BlockSpec((tl, bsz, d), lambda i, j: (i, 0, 0)),
        out_shape=jax.ShapeDtypeStruct((seq, bsz, d), F32),
        scratch_shapes=[
            pltpu.VMEM((d, tok), BF16),
            pltpu.VMEM((d, tok), F32),
            pltpu.VMEM((te, tok), BF16),
            pltpu.VMEM((te, tok), BF16),
        ],
        compiler_params=_params(("arbitrary", "arbitrary")),
    )(h2, *tables, u16, vt16)


def _final_body(x_ref, po_ref, modp_ref, g_ref, o_ref):
    x = x_ref[...] + modp_ref[5] * po_ref[...]
    o_ref[...] = x * lax.rsqrt(jnp.mean(x * x, axis=-1, keepdims=True) + EPS) * g_ref[...]


def _final_norm(x1, peer_out, mod_prev, g):
    seq, bsz, d = x1.shape
    tl = TOKEN_TILE // bsz
    xspec = pl.BlockSpec((tl, bsz, d), lambda i: (i, 0, 0))
    return pl.pallas_call(
        _final_body,
        grid=(seq // tl,),
        in_specs=[xspec, xspec,
                  pl.BlockSpec((N_MOD, bsz, d), lambda i: (0, 0, 0)),
                  pl.BlockSpec((1, d), lambda i: (0, 0))],
        out_specs=xspec,
        out_shape=jax.ShapeDtypeStruct((seq, bsz, d), F32),
        compiler_params=_params(("arbitrary",)),
    )(x1, peer_out, mod_prev, g.reshape(1, d))


def _block_diag(blocks):
    n, r, c = blocks.shape
    eye = jnp.eye(n, dtype=blocks.dtype)
    return (blocks[:, :, None, :] * eye[:, None, :, None]).reshape(n * r, n * c)


def kernel(x, c, w_mod, b_mod, g_mix, w_in, hgrn_lb_logits, s5_a_re, s5_a_im, s5_b_re, s5_b_im, s5_c_re, s5_c_im, s5_log_dt, s5_d, s5_w_glu, s5_b_glu, lru_conv_w, lru_conv_b, lru_w_a, lru_b_a, lru_w_x, lru_b_x, lru_lambda, g_branch, w_out, g_ffn, peer_w_q, peer_sub_keys, peer_u, peer_v, g_final):
    bsz, seq, d = x.shape
    depth = w_in.shape[0]
    assert bsz == SUBLANES and seq % CHUNK_T == 0 and (seq * bsz) % TOKEN_TILE == 0

    x_tm = x.transpose(1, 0, 2)
    mod = _modulation(c, w_mod, b_mod)
    mod = mod.reshape(depth, bsz, N_MOD, d).transpose(0, 2, 1, 3)
    lb_all, ab_re, ab_im, bb_re, bb_im = _prepare_params(
        hgrn_lb_logits, s5_a_re, s5_a_im, s5_log_dt, s5_b_re, s5_b_im)
    tmat = jnp.asarray(_hgrn_exponent_matrix(CHUNK_T), BF16)
    row = lambda a: a.reshape(1, -1)

    for l in range(depth):
        bbd = jnp.concatenate([
            _block_diag(bb_re[l].reshape(S5_GROUP, S5_GROUPS, S5_STATE).transpose(1, 0, 2)),
            _block_diag(bb_im[l].reshape(S5_GROUP, S5_GROUPS, S5_STATE).transpose(1, 0, 2)),
        ], axis=1).astype(BF16)
        cre = _block_diag(s5_c_re[l].transpose(0, 2, 1)).astype(BF16)
        cim = _block_diag(s5_c_im[l].transpose(0, 2, 1)).astype(BF16)
        s5p = (ab_re[l], ab_im[l], bbd, cre, cim, row(s5_d[l]),
               s5_w_glu[l].astype(BF16), row(s5_b_glu[l]))
        lrup = (lru_conv_w[l], row(lru_conv_b[l]),
                _block_diag(lru_w_a[l]).astype(BF16), row(lru_b_a[l]),
                _block_diag(lru_w_x[l]).astype(BF16), row(lru_b_x[l]), row(lru_lambda[l]))

        if l == 0:
            pa, pg, pb, pc = _in_projection(x_tm, None, mod[l], g_mix[l], w_in[l].astype(BF16))
        else:
            x_tm, pa, pg, pb, pc = _in_projection(x1, (peer_out, mod[l - 1]), mod[l], g_mix[l],
                                                  w_in[l].astype(BF16))
        y = _mixer(pa, pg, pb, pc, bsz, tmat, lb_all[l:l + 1], s5p, lrup, row(g_branch[l]))
        x1, h2, q = _out_projection(x_tm, y, mod[l], g_ffn[l],
                                    w_out[l].astype(BF16), peer_w_q[l].astype(BF16))
        tables = _retrieval(q, peer_sub_keys[l].astype(BF16))
        peer_out = _peer_dense(h2, tables, peer_u[l].astype(BF16), peer_v[l].astype(BF16).T, bsz)

    out = _final_norm(x1, peer_out, mod[depth - 1], g_final)
    return out.transpose(1, 0, 2)
```

```python
import functools

import numpy as np
import jax
import jax.numpy as jnp
from jax import lax
from jax.experimental import pallas as pl
from jax.experimental.pallas import tpu as pltpu

F32 = jnp.float32
BF16 = jnp.bfloat16

EPS = 1e-6
N_MOD = 6
HGRN_HEADS = 4
HGRN_DK = 128
D_HGRN = HGRN_HEADS * HGRN_DK
S5_GROUP = 16
S5_GROUPS = 16
S5_STATE = 64
D_S5 = S5_GROUP * S5_GROUPS
N_S5_STATE = S5_GROUPS * S5_STATE
LRU_BLOCKS = 4
LRU_BLOCK = 64
D_LRU = LRU_BLOCKS * LRU_BLOCK
CONV_W = 4
LRU_C = 8.0
PEER_HEADS = 8
N_KEYS = 128
PEER_DQ = 256
PEER_TOPK = 16

SUBLANES = 8
LANES = 128
VMEM_LIMIT_BYTES = 56 * 1024 * 1024

CHUNK_T = 64
HGRN_GROUP = 8
TOKEN_TILE = 512
EXPERT_TILE = 1024
PEER_TOKEN_TILE = 1024
PEER_PIECE = 256
RETR_TILE = 256
MOD_COL_TILE = 1536


def _sigmoid(x):
    return 1.0 / (1.0 + jnp.exp(-x))


def _silu(x):
    return x * _sigmoid(x)


def _gelu(x):
    return 0.5 * x * (1.0 + jnp.tanh(0.7978845608028654 * (x + 0.044715 * (x * x * x))))


def _dot(a, b):
    return jnp.dot(a, b, preferred_element_type=F32)


def _dot_nt(a, b):
    return lax.dot_general(a, b, (((1,), (1,)), ((), ())), preferred_element_type=F32)


def _dot_tn(a, b):
    return lax.dot_general(a, b, (((0,), (0,)), ((), ())), preferred_element_type=F32)


def _split2(x):
    hi = x.astype(BF16)
    lo = (x - hi.astype(F32)).astype(BF16)
    return hi, lo


def _params(semantics):
    return pltpu.CompilerParams(dimension_semantics=semantics, vmem_limit_bytes=VMEM_LIMIT_BYTES)


def _mod_body(c_ref, w_ref, b_ref, o_ref):
    cond = _silu(c_ref[...])
    c_hi, c_lo = _split2(cond)
    w_hi, w_lo = _split2(w_ref[0])
    o_ref[0] = _dot(c_hi, w_hi) + _dot(c_hi, w_lo) + _dot(c_lo, w_hi) + b_ref[0]


def _modulation(c, w_mod, b_mod):
    depth, d, n = w_mod.shape
    bsz = c.shape[0]
    tn = MOD_COL_TILE
    return pl.pallas_call(
        _mod_body,
        grid=(depth, n // tn),
        in_specs=[
            pl.BlockSpec((bsz, d), lambda l, j: (0, 0)),
            pl.BlockSpec((1, d, tn), lambda l, j: (l, 0, j)),
            pl.BlockSpec((1, 1, tn), lambda l, j: (l, 0, j)),
        ],
        out_specs=pl.BlockSpec((1, bsz, tn), lambda l, j: (l, 0, j)),
        out_shape=jax.ShapeDtypeStruct((depth, bsz, n), F32),
        compiler_params=_params(("arbitrary", "arbitrary")),
    )(c, w_mod, b_mod.reshape(depth, 1, n))


def _prep_body(lbl_ref, are_ref, aim_ref, ldt_ref, bre_ref, bim_ref,
               lb_ref, abr_ref, abi_ref, bbr_ref, bbi_ref):
    logits = lbl_ref[...]
    depth = logits.shape[0]
    rows = [logits[l:l + 1] for l in range(depth)]
    mx = functools.reduce(jnp.maximum, rows)
    ex = [jnp.exp(r - mx) for r in rows]
    den = functools.reduce(lambda a, b: a + b, ex)
    soft = [e / den for e in ex]
    run = soft[0]
    for l in range(depth):
        if l > 0:
            run = run + soft[l]
        lb_ref[l:l + 1, :] = run - soft[0]

    lam_re = jnp.minimum(are_ref[...], -1e-4)
    lam_im = aim_ref[...]
    dt = jnp.exp(ldt_ref[...])
    mag = jnp.exp(lam_re * dt)
    ab_re = mag * jnp.cos(lam_im * dt)
    ab_im = mag * jnp.sin(lam_im * dt)
    den2 = lam_re * lam_re + lam_im * lam_im
    nr = ab_re - 1.0
    z_re = (nr * lam_re + ab_im * lam_im) / den2
    z_im = (ab_im * lam_re - nr * lam_im) / den2
    abr_ref[...] = ab_re
    abi_ref[...] = ab_im
    br = bre_ref[...]
    bi = bim_ref[...]
    bbr_ref[...] = z_re * br - z_im * bi
    bbi_ref[...] = z_re * bi + z_im * br


def _prepare_params(lb_logits, a_re, a_im, log_dt, b_re, b_im):
    depth = lb_logits.shape[0]
    n = N_S5_STATE
    flat = lambda a: a.reshape(depth, 1, n)
    ldt = jnp.broadcast_to(log_dt[:, :, None], (depth, S5_GROUPS, S5_STATE))
    bt = lambda b: b.transpose(0, 3, 1, 2).reshape(depth, S5_GROUP, n)
    out_shape = (
        jax.ShapeDtypeStruct((depth, D_HGRN), F32),
        jax.ShapeDtypeStruct((depth, 1, n), F32),
        jax.ShapeDtypeStruct((depth, 1, n), F32),
        jax.ShapeDtypeStruct((depth, S5_GROUP, n), F32),
        jax.ShapeDtypeStruct((depth, S5_GROUP, n), F32),
    )
    return pl.pallas_call(_prep_body, out_shape=out_shape)(
        lb_logits, flat(a_re), flat(a_im), flat(ldt), bt(b_re), bt(b_im))


def _inproj_body(x_ref, mod_ref, g_ref, w_ref, pa_ref, pg_ref, pb_ref, pc_ref):
    x = x_ref[...]
    tl, bsz, d = x.shape
    ms = jnp.mean(x * x, axis=-1, keepdims=True)
    xn = x * lax.rsqrt(ms + EPS) * g_ref[...]
    h = xn * (1.0 + mod_ref[1]) + mod_ref[0]
    h = h.reshape(tl * bsz, d).astype(BF16)
    na = pa_ref.shape[0] * LANES
    ng = pg_ref.shape[1]
    nb = pb_ref.shape[1]
    pa = _dot(h, w_ref[:, 0:na])
    for cg in range(pa_ref.shape[0]):
        pa_ref[cg] = pa[:, cg * LANES:(cg + 1) * LANES]
    pg_ref[...] = _dot(h, w_ref[:, na:na + ng])
    pb_ref[...] = _dot(h, w_ref[:, na + ng:na + ng + nb])
    pc_ref[...] = _dot(h, w_ref[:, na + ng + nb:])


def _in_projection(x_tm, mod_l, g_mix, w_in16):
    seq, bsz, d = x_tm.shape
    tl = TOKEN_TILE // bsz
    tm = tl * bsz
    rows = seq * bsz
    ncg = 3 * D_HGRN // LANES
    ng, nb, nc = D_HGRN, D_S5, 2 * D_LRU
    return pl.pallas_call(
        _inproj_body,
        grid=(seq // tl,),
        in_specs=[
            pl.BlockSpec((tl, bsz, d), lambda i: (i, 0, 0)),
            pl.BlockSpec((N_MOD, bsz, d), lambda i: (0, 0, 0)),
            pl.BlockSpec((1, d), lambda i: (0, 0)),
            pl.BlockSpec(w_in16.shape, lambda i: (0, 0)),
        ],
        out_specs=[
            pl.BlockSpec((ncg, tm, LANES), lambda i: (0, i, 0)),
            pl.BlockSpec((tm, ng), lambda i: (i, 0)),
            pl.BlockSpec((tm, nb), lambda i: (i, 0)),
            pl.BlockSpec((tm, nc), lambda i: (i, 0)),
        ],
        out_shape=(
            jax.ShapeDtypeStruct((ncg, rows, LANES), F32),
            jax.ShapeDtypeStruct((rows, ng), F32),
            jax.ShapeDtypeStruct((rows, nb), F32),
            jax.ShapeDtypeStruct((rows, nc), F32),
        ),
        compiler_params=_params(("arbitrary",)),
    )(x_tm, mod_l, g_mix.reshape(1, d), w_in16)


def _hgrn_exponent_matrix(chunk):
    n_levels = int(np.log2(chunk))
    mat = np.zeros((n_levels + 2, chunk, chunk), np.float32)
    u = np.arange(chunk)
    for k in range(n_levels):
        g = chunk >> (k + 1)
        m = 2 * g
        for t in range(chunk):
            r = (t // m) * m + g - 1
            if t % m >= g:
                mat[k, t] = (u > r) & (u <= t)
            else:
                mat[k, t] = (u > t) & (u <= r)
    for t in range(chunk):
        mat[n_levels, t] = u <= t
        mat[n_levels + 1, t] = u > t
    return mat.reshape((n_levels + 2) * chunk, chunk)


def _mixer_body(pa_ref, pg_ref, pb_ref, pc_ref, tmat_ref, lb_ref,
                ar_ref, ai_ref, bbd_ref, cre_ref, cim_ref, dsk_ref, wglu_ref, bglu_ref,
                cw_ref, cb_ref, wa_ref, ba_ref, wx_ref, bx_ref, lam_ref, gbr_ref,
                y_ref,
                st_ref, o_scr, bu_scr, xs_scr, s5_state, lru_state, xext, la_scr, lb_scr, hs_scr):
    rows = pa_ref.shape[1]
    bsz = s5_state.shape[0]
    chunk = rows // bsz
    n_levels = tmat_ref.shape[0] // chunk - 2
    dk = HGRN_DK

    @pl.when(pl.program_id(0) == 0)
    def _():
        st_ref[...] = jnp.zeros_like(st_ref)
        s5_state[...] = jnp.zeros_like(s5_state)
        lru_state[...] = jnp.zeros_like(lru_state)
        xext[0:(CONV_W - 1) * bsz, :] = jnp.zeros(((CONV_W - 1) * bsz, D_LRU), F32)

    ti = lax.broadcasted_iota(jnp.int32, (chunk, chunk), 0)
    si = lax.broadcasted_iota(jnp.int32, (chunk, chunk), 1)
    masks = []
    for k in range(n_levels):
        g = chunk >> (k + 1)
        m = 2 * g
        same = (ti // m) == (si // m)
        masks.append(jnp.where(same, jnp.where((ti % m) >= g, jnp.where((si % m) < g, 1.0, 0.0), 0.0), 0.0))
    diag = jnp.where(ti == si, 1.0, 0.0)
    lb = lb_ref[...]
    tmat = tmat_ref[...]

    def hgrn_group(g, carry):
        grp = range(HGRN_GROUP)
        bs = [g * HGRN_GROUP + e for e in grp]
        rws = [pl.ds(b, chunk, stride=bsz) for b in bs]
        heads = lambda e, first: jnp.concatenate(
            [pa_ref[first + h, rws[e], :] for h in range(HGRN_HEADS)], axis=1)
        qa = [_silu(heads(e, 0)) for e in grp]
        f = [lb + (1.0 - lb) * _sigmoid(heads(e, HGRN_HEADS)) for e in grp]
        v16 = [heads(e, 2 * HGRN_HEADS).astype(BF16) for e in grp]
        kk = [1.0 - f[e] for e in grp]
        parts = [_split2(jnp.log(f[e])) for e in grp]
        decay = [jnp.exp(_dot(tmat, parts[e][0]) + _dot(tmat, parts[e][1])) for e in grp]
        cum = [decay[e][n_levels * chunk:(n_levels + 1) * chunk] for e in grp]
        qb16 = [(qa[e] * cum[e]).astype(BF16) for e in grp]
        ks16 = [(kk[e] * decay[e][(n_levels + 1) * chunk:]).astype(BF16) for e in grp]
        total = [cum[e][chunk - 1:chunk, :] for e in grp]
        qa16 = [qa[e].astype(BF16) for e in grp]
        kk16 = [kk[e].astype(BF16) for e in grp]
        for h in range(HGRN_HEADS):
            sl = slice(h * dk, (h + 1) * dk)
            scores = [diag * _dot_nt(qa16[e][:, sl], kk16[e][:, sl]) for e in grp]
            for k in range(n_levels):
                for e in grp:
                    dl = decay[e][k * chunk:(k + 1) * chunk, sl]
                    ql = (qa[e][:, sl] * dl).astype(BF16)
                    kl = (kk[e][:, sl] * dl).astype(BF16)
                    scores[e] = scores[e] + masks[k] * _dot_nt(ql, kl)
            for e in grp:
                st = st_ref[bs[e], h]
                o = (_dot(scores[e].astype(BF16), v16[e][:, sl])
                     + _dot_nt(qb16[e][:, sl], st.astype(BF16)))
                o_scr[h, rws[e], :] = o
                st_ref[bs[e], h] = st * total[e][:, sl] + _dot_tn(v16[e][:, sl], ks16[e][:, sl])
        return carry

    lax.fori_loop(0, bsz // HGRN_GROUP, hgrn_group, 0)

    gate = _silu(pg_ref[...])
    ya = []
    ssq = None
    for h in range(HGRN_HEADS):
        sl = slice(h * dk, (h + 1) * dk)
        oh = o_scr[h]
        oh = oh * lax.rsqrt(jnp.mean(oh * oh, axis=-1, keepdims=True) + EPS) * gate[:, sl]
        ya.append(oh)
        s = jnp.sum(oh * oh, axis=-1, keepdims=True)
        ssq = s if ssq is None else ssq + s
    scale = lax.rsqrt(ssq * (1.0 / D_HGRN) + EPS)
    for h in range(HGRN_HEADS):
        sl = slice(h * dk, (h + 1) * dk)
        y_ref[:, sl] = (ya[h] * scale * gbr_ref[:, sl]).astype(y_ref.dtype)

    n = N_S5_STATE
    u = pb_ref[...]
    bu_scr[...] = _dot(u.astype(BF16), bbd_ref[...])
    ar = jnp.broadcast_to(ar_ref[...], (bsz, n))
    ai = jnp.broadcast_to(ai_ref[...], (bsz, n))

    def s5_step(t, st):
        xr, xi = st
        r = pl.ds(pl.multiple_of(t * bsz, bsz), bsz)
        nxr = ar * xr - ai * xi + bu_scr[r, 0:n]
        nxi = ar * xi + ai * xr + bu_scr[r, n:2 * n]
        xs_scr[r, 0:n] = nxr
        xs_scr[r, n:2 * n] = nxi
        return nxr, nxi

    xr, xi = lax.fori_loop(0, chunk, s5_step, (s5_state[:, 0:n], s5_state[:, n:2 * n]))
    s5_state[:, 0:n] = xr
    s5_state[:, n:2 * n] = xi
    yb = (_dot(xs_scr[:, 0:n].astype(BF16), cre_ref[...])
          - _dot(xs_scr[:, n:2 * n].astype(BF16), cim_ref[...]))
    yb = yb + dsk_ref[...] * u
    z = _gelu(yb)
    yb = z * _sigmoid(_dot(z.astype(BF16), wglu_ref[...]) + bglu_ref[...])
    yb = yb * lax.rsqrt(jnp.mean(yb * yb, axis=-1, keepdims=True) + EPS)
    y_ref[:, D_HGRN:D_HGRN + D_S5] = (yb * gbr_ref[:, D_HGRN:D_HGRN + D_S5]).astype(y_ref.dtype)

    tail = (CONV_W - 1) * bsz
    xin = pc_ref[:, 0:D_LRU]
    xext[tail:tail + rows, :] = xin
    xc = cb_ref[...]
    for w in range(CONV_W):
        xc = xc + cw_ref[w:w + 1, :] * xext[w * bsz:w * bsz + rows, :]
    xext[0:tail, :] = xin[rows - tail:rows, :]
    xc16 = xc.astype(BF16)
    rg = _sigmoid(_dot(xc16, wa_ref[...]) + ba_ref[...])
    ig = _sigmoid(_dot(xc16, wx_ref[...]) + bx_ref[...])
    nl = -lam_ref[...]
    softplus = jnp.maximum(nl, 0.0) + jnp.log(1.0 + jnp.exp(-jnp.abs(nl)))
    log_a = (-LRU_C) * rg * softplus
    la_scr[...] = jnp.exp(log_a)
    lb_scr[...] = jnp.sqrt(1.0 - jnp.exp(2.0 * log_a)) * (ig * xc)

    def lru_step(t, hstate):
        r = pl.ds(pl.multiple_of(t * bsz, bsz), bsz)
        hn = la_scr[r, :] * hstate + lb_scr[r, :]
        hs_scr[r, :] = hn
        return hn

    lru_state[...] = lax.fori_loop(0, chunk, lru_step, lru_state[...])
    yc = hs_scr[...] * _gelu(pc_ref[:, D_LRU:2 * D_LRU])
    yc = yc * lax.rsqrt(jnp.mean(yc * yc, axis=-1, keepdims=True) + EPS)
    y_ref[:, D_HGRN + D_S5:] = (yc * gbr_ref[:, D_HGRN + D_S5:]).astype(y_ref.dtype)


def _mixer(pa, pg, pb, pc, bsz, tmat, lb, s5p, lrup, g_branch):
    rows_total = pg.shape[0]
    rows = CHUNK_T * bsz
    d_mix = D_HGRN + D_S5 + D_LRU
    n = N_S5_STATE
    full = lambda a: pl.BlockSpec(a.shape, lambda i, nd=a.ndim: (0,) * nd)
    consts = [tmat, lb, *s5p, *lrup, g_branch]
    return pl.pallas_call(
        _mixer_body,
        grid=(rows_total // rows,),
        in_specs=[
            pl.BlockSpec((pa.shape[0], rows, LANES), lambda i: (0, i, 0)),
            pl.BlockSpec((rows, pg.shape[1]), lambda i: (i, 0)),
            pl.BlockSpec((rows, pb.shape[1]), lambda i: (i, 0)),
            pl.BlockSpec((rows, pc.shape[1]), lambda i: (i, 0)),
        ] + [full(a) for a in consts],
        out_specs=pl.BlockSpec((rows, d_mix), lambda i: (i, 0)),
        out_shape=jax.ShapeDtypeStruct((rows_total, d_mix), BF16),
        scratch_shapes=[
            pltpu.VMEM((bsz, HGRN_HEADS, HGRN_DK, HGRN_DK), F32),
            pltpu.VMEM((HGRN_HEADS, rows, HGRN_DK), F32),
            pltpu.VMEM((rows, 2 * n), F32),
            pltpu.VMEM((rows, 2 * n), F32),
            pltpu.VMEM((bsz, 2 * n), F32),
            pltpu.VMEM((bsz, D_LRU), F32),
            pltpu.VMEM(((CONV_W - 1) * bsz + rows, D_LRU), F32),
            pltpu.VMEM((rows, D_LRU), F32),
            pltpu.VMEM((rows, D_LRU), F32),
            pltpu.VMEM((rows, D_LRU), F32),
        ],
        compiler_params=_params(("arbitrary",)),
    )(pa, pg, pb, pc, *consts)


def _outproj_body(x_ref, y_ref, mod_ref, g_ref, wo_ref, wq_ref, x1_ref, h2_ref, q_ref):
    x = x_ref[...]
    tl, bsz, d = x.shape
    o = _dot(y_ref[...], wo_ref[...]).reshape(tl, bsz, d)
    x1 = x + mod_ref[2] * o
    x1_ref[...] = x1
    ms = jnp.mean(x1 * x1, axis=-1, keepdims=True)
    h2 = x1 * lax.rsqrt(ms + EPS) * g_ref[...]
    h2 = h2 * (1.0 + mod_ref[4]) + mod_ref[3]
    h2 = h2.reshape(tl * bsz, d).astype(BF16)
    h2_ref[...] = h2
    q_ref[...] = _dot(h2, wq_ref[...])


def _out_projection(x_tm, y, mod_l, g_ffn, w_out16, w_q16):
    seq, bsz, d = x_tm.shape
    tl = TOKEN_TILE // bsz
    tm = tl * bsz
    rows = seq * bsz
    nq = w_q16.shape[1]
    return pl.pallas_call(
        _outproj_body,
        grid=(seq // tl,),
        in_specs=[
            pl.BlockSpec((tl, bsz, d), lambda i: (i, 0, 0)),
            pl.BlockSpec((tm, y.shape[1]), lambda i: (i, 0)),
            pl.BlockSpec((N_MOD, bsz, d), lambda i: (0, 0, 0)),
            pl.BlockSpec((1, d), lambda i: (0, 0)),
            pl.BlockSpec(w_out16.shape, lambda i: (0, 0)),
            pl.BlockSpec(w_q16.shape, lambda i: (0, 0)),
        ],
        out_specs=[
            pl.BlockSpec((tl, bsz, d), lambda i: (i, 0, 0)),
            pl.BlockSpec((tm, d), lambda i: (i, 0)),
            pl.BlockSpec((tm, nq), lambda i: (i, 0)),
        ],
        out_shape=(
            jax.ShapeDtypeStruct((seq, bsz, d), F32),
            jax.ShapeDtypeStruct((rows, d), BF16),
            jax.ShapeDtypeStruct((rows, nq), F32),
        ),
        compiler_params=_params(("arbitrary",)),
    )(x_tm, y, mod_l, g_ffn.reshape(1, d), w_out16, w_q16)


def _bitonic_sort_network(n):
    ops = []
    k = 2
    while k <= n:
        j = k // 2
        while j >= 1:
            for i in range(n):
                l = i ^ j
                if l > i:
                    ops.append((i, l, (i & k) == 0))
            j //= 2
        k *= 2
    return ops


def _bitonic_merge_network(n):
    ops = []
    j = n // 2
    while j >= 1:
        for i in range(n):
            l = i ^ j
            if l > i:
                ops.append((i, l, True))
        j //= 2
    return ops


def _apply_network(vals, ops):
    vals = list(vals)
    for i, j, first_max in ops:
        hi = jnp.maximum(vals[i], vals[j])
        lo = jnp.minimum(vals[i], vals[j])
        vals[i], vals[j] = (hi, lo) if first_max else (lo, hi)
    return vals


def _top_sorted(s, k):
    n_keys = s.shape[0]
    n_slabs = n_keys // SUBLANES
    assert n_slabs == k
    slabs = [s[i * SUBLANES:(i + 1) * SUBLANES, :] for i in range(n_slabs)]
    slabs = _apply_network(slabs, _bitonic_sort_network(k))
    merge = _bitonic_merge_network(k)
    shift = SUBLANES // 2
    while shift >= 1:
        rolled = [pltpu.roll(v, shift, 0) for v in slabs]
        slabs = [jnp.maximum(slabs[i], rolled[k - 1 - i]) for i in range(k)]
        slabs = _apply_network(slabs, merge)
        shift //= 2
    return slabs


def _staircase(k):
    return [(x, y) for x in range(k) for y in range(k) if (x + 1) * (y + 1) <= k]


def _select_tree(masks, leaves):
    if not masks:
        return leaves[0]
    half = len(leaves) // 2
    return jnp.where(masks[0], _select_tree(masks[1:], leaves[half:]), _select_tree(masks[1:], leaves[:half]))


def _prefix_count(pred, vals):
    n = len(vals)
    full = pred(vals[n - 1])
    masks = []
    count = None
    stride = n // 2
    while stride >= 1:
        pivot = _select_tree(masks, [vals[base + stride - 1] for base in range(0, n, 2 * stride)])
        hit = pred(pivot)
        masks.append(hit)
        term = jnp.where(hit, float(stride), 0.0)
        count = term if count is None else count + term
        stride //= 2
    return jnp.where(full, float(n), count)


def _retrieval_body(q_ref, keys_ref, r1_ref, e1_ref, n0_ref, c0_ref, s_scr, a_scr, b_scr,
                    afull_scr, bfull_scr):
    k = PEER_TOPK
    half = PEER_DQ // 2
    tok = q_ref.shape[0]
    neg = jnp.full((SUBLANES, tok), -jnp.inf, F32)

    for h in range(PEER_HEADS):
        for p in range(2):
            off = h * PEER_DQ + p * half
            s = _dot_nt(keys_ref[p], q_ref[:, off:off + half].astype(BF16))
            s_scr[h, p] = s
            top = _top_sorted(s, k)
            dst, dst_full = (a_scr, afull_scr) if p == 0 else (b_scr, bfull_scr)
            for x in range(k):
                dst[x, h:h + 1, :] = top[x][0:1, :]
                dst_full[h, x] = top[x]

    a = [a_scr[x] for x in range(k)]
    b = [b_scr[y] for y in range(k)]
    pairs = _staircase(k)
    cands = [a[x] + b[y] for x, y in pairs]
    n_sort = 1
    while n_sort < len(cands):
        n_sort *= 2
    ranked = _apply_network(cands + [neg] * (n_sort - len(cands)), _bitonic_sort_network(n_sort))
    tau = ranked[k - 1]
    best = a[0] + b[0]
    z = jnp.zeros_like(tau)
    for cnd in cands:
        z = z + jnp.where(cnd >= tau, jnp.exp(cnd - best), 0.0)
    inv_z = 1.0 / z

    pack = 2 * SUBLANES
    for h in range(PEER_HEADS):
        tau_h = jnp.broadcast_to(tau[h:h + 1, :], (SUBLANES, tok))
        inv_z_h = jnp.broadcast_to(inv_z[h:h + 1, :], (SUBLANES, tok))
        b_h = [bfull_scr[h, y] for y in range(k)]
        a_first = afull_scr[h, 0]
        a_last = afull_scr[h, k - 1]
        for blk in range(N_KEYS // pack):
            ranks, exps = [], []
            for sub in range(pack // SUBLANES):
                rows = slice(blk * pack + sub * SUBLANES, blk * pack + (sub + 1) * SUBLANES)
                s0 = s_scr[h, 0, rows, :]
                s1 = s_scr[h, 1, rows, :]
                ranks.append(_prefix_count(lambda v: v > s1, b_h))
                cnt = _prefix_count(lambda v: s0 + v >= tau_h, b_h)
                n0_ref[h, rows, :] = jnp.where(s0 >= a_last, cnt, 0.0)
                c0_ref[h, rows, :] = jnp.exp(s0 - a_first) * inv_z_h
                exps.append(jnp.exp(s1 - b_h[0]))
            prow = slice(blk * pack, (blk + 1) * pack)
            r1_ref[h, prow, :] = jnp.concatenate(ranks, axis=0).astype(r1_ref.dtype)
            e1_ref[h, prow, :] = jnp.concatenate(exps, axis=0).astype(e1_ref.dtype)


def _retrieval(q, keys16):
    rows = q.shape[0]
    tok = RETR_TILE
    tab = lambda dt: jax.ShapeDtypeStruct((PEER_HEADS, N_KEYS, rows), dt)
    tspec = pl.BlockSpec((PEER_HEADS, N_KEYS, tok), lambda i: (0, 0, i))
    return pl.pallas_call(
        _retrieval_body,
        grid=(rows // tok,),
        in_specs=[
            pl.BlockSpec((tok, q.shape[1]), lambda i: (i, 0)),
            pl.BlockSpec(keys16.shape, lambda i: (0, 0, 0)),
        ],
        out_specs=[tspec, tspec, tspec, tspec],
        out_shape=(tab(BF16), tab(BF16), tab(F32), tab(F32)),
        scratch_shapes=[
            pltpu.VMEM((PEER_HEADS, 2, N_KEYS, tok), F32),
            pltpu.VMEM((PEER_TOPK, SUBLANES, tok), F32),
            pltpu.VMEM((PEER_TOPK, SUBLANES, tok), F32),
            pltpu.VMEM((PEER_HEADS, PEER_TOPK, SUBLANES, tok), F32),
            pltpu.VMEM((PEER_HEADS, PEER_TOPK, SUBLANES, tok), F32),
        ],
        compiler_params=_params(("arbitrary",)),
    )(q, keys16)


def _peer_body(h2_ref, r1_ref, e1_ref, n0_ref, c0_ref, u_ref, vt_ref, x1_ref, mod_ref,
               x2_ref, h2t_scr, acc_scr, a_scr, p_scr):
    j = pl.program_id(1)
    n_j = pl.num_programs(1)
    te = u_ref.shape[0]
    tok = h2_ref.shape[0]
    rows_per_step = te // N_KEYS
    pack = 2 * SUBLANES

    @pl.when(j == 0)
    def _():
        h2t_scr[...] = h2_ref[...].astype(F32).T.astype(BF16)
        acc_scr[...] = jnp.zeros_like(acc_scr)
        a_scr[...] = jnp.zeros_like(a_scr)

    zero = jnp.zeros((pack, PEER_PIECE), BF16)
    for c in range(tok // PEER_PIECE):
        cols = slice(c * PEER_PIECE, (c + 1) * PEER_PIECE)
        for r in range(rows_per_step):
            gates = [zero for _ in range(N_KEYS // pack)]
            for h in range(PEER_HEADS):
                nb = jnp.broadcast_to(n0_ref[h, r:r + 1, cols], (pack, PEER_PIECE)).astype(BF16)
                cb = jnp.broadcast_to(c0_ref[h, r:r + 1, cols], (pack, PEER_PIECE)).astype(BF16)
                for k in range(N_KEYS // pack):
                    rs = slice(k * pack, (k + 1) * pack)
                    sel = jnp.where(r1_ref[h, rs, cols] < nb, e1_ref[h, rs, cols], zero)
                    gates[k] = gates[k] + sel * cb
            for k in range(N_KEYS // pack):
                rs = slice(r * N_KEYS + k * pack, r * N_KEYS + (k + 1) * pack)
                p_scr[rs, cols] = gates[k] * _gelu(a_scr[rs, cols])
        acc_scr[:, cols] += _dot(vt_ref[...], p_scr[:, cols])
        a_scr[:, cols] = _dot(u_ref[...], h2t_scr[:, cols]).astype(BF16)

    @pl.when(j == n_j - 1)
    def _():
        tl, bsz, d = x1_ref.shape
        out = acc_scr[...].T.reshape(tl, bsz, d)
        x2_ref[...] = x1_ref[...] + mod_ref[5] * out


def _peer_dense(h2, tables, u16, vt16, x1, mod_l):
    seq, bsz, d = x1.shape
    tok = PEER_TOKEN_TILE
    tl = tok // bsz
    te = EXPERT_TILE
    n_tiles = u16.shape[0] // te
    rows_per_step = te // N_KEYS
    cur = lambda j: jnp.minimum(j, n_tiles - 1)
    prev = lambda j: jnp.maximum(j - 1, 0)
    tspec = pl.BlockSpec((PEER_HEADS, N_KEYS, tok), lambda i, j: (0, 0, i))
    rspec = pl.BlockSpec((PEER_HEADS, rows_per_step, tok), lambda i, j: (0, prev(j), i))
    return pl.pallas_call(
        _peer_body,
        grid=(seq // tl, n_tiles + 1),
        in_specs=[
            pl.BlockSpec((tok, d), lambda i, j: (i, 0)),
            tspec, tspec, rspec, rspec,
            pl.BlockSpec((te, d), lambda i, j: (cur(j), 0)),
            pl.BlockSpec((d, te), lambda i, j: (0, prev(j))),
            pl.BlockSpec((tl, bsz, d), lambda i, j: (i, 0, 0)),
            pl.BlockSpec((N_MOD, bsz, d), lambda i, j: (0, 0, 0)),
        ],
        out_specs=pl.BlockSpec((tl, bsz, d), lambda i, j: (i, 0, 0)),
        out_shape=jax.ShapeDtypeStruct((seq, bsz, d), F32),
        scratch_shapes=[
            pltpu.VMEM((d, tok), BF16),
            pltpu.VMEM((d, tok), F32),
            pltpu.VMEM((te, tok), BF16),
            pltpu.VMEM((te, tok), BF16),
        ],
        compiler_params=_params(("arbitrary", "arbitrary")),
    )(h2, *tables, u16, vt16, x1, mod_l)


def _final_body(x_ref, g_ref, o_ref):
    x = x_ref[...]
    o_ref[...] = x * lax.rsqrt(jnp.mean(x * x, axis=-1, keepdims=True) + EPS) * g_ref[...]


def _final_norm(x_tm, g):
    seq, bsz, d = x_tm.shape
    tl = TOKEN_TILE // bsz
    return pl.pallas_call(
        _final_body,
        grid=(seq // tl,),
        in_specs=[pl.BlockSpec((tl, bsz, d), lambda i: (i, 0, 0)),
                  pl.BlockSpec((1, d), lambda i: (0, 0))],
        out_specs=pl.BlockSpec((tl, bsz, d), lambda i: (i, 0, 0)),
        out_shape=jax.ShapeDtypeStruct((seq, bsz, d), F32),
        compiler_params=_params(("arbitrary",)),
    )(x_tm, g.reshape(1, d))


def _block_diag(blocks):
    n, r, c = blocks.shape
    eye = jnp.eye(n, dtype=blocks.dtype)
    return (blocks[:, :, None, :] * eye[:, None, :, None]).reshape(n * r, n * c)


def kernel(x, c, w_mod, b_mod, g_mix, w_in, hgrn_lb_logits, s5_a_re, s5_a_im, s5_b_re, s5_b_im, s5_c_re, s5_c_im, s5_log_dt, s5_d, s5_w_glu, s5_b_glu, lru_conv_w, lru_conv_b, lru_w_a, lru_b_a, lru_w_x, lru_b_x, lru_lambda, g_branch, w_out, g_ffn, peer_w_q, peer_sub_keys, peer_u, peer_v, g_final):
    bsz, seq, d = x.shape
    depth = w_in.shape[0]
    assert bsz == SUBLANES and seq % CHUNK_T == 0 and (seq * bsz) % TOKEN_TILE == 0

    x_tm = x.transpose(1, 0, 2)
    mod = _modulation(c, w_mod, b_mod)
    mod = mod.reshape(depth, bsz, N_MOD, d).transpose(0, 2, 1, 3)
    lb_all, ab_re, ab_im, bb_re, bb_im = _prepare_params(
        hgrn_lb_logits, s5_a_re, s5_a_im, s5_log_dt, s5_b_re, s5_b_im)
    tmat = jnp.asarray(_hgrn_exponent_matrix(CHUNK_T), BF16)
    row = lambda a: a.reshape(1, -1)

    for l in range(depth):
        bbd = jnp.concatenate([
            _block_diag(bb_re[l].reshape(S5_GROUP, S5_GROUPS, S5_STATE).transpose(1, 0, 2)),
            _block_diag(bb_im[l].reshape(S5_GROUP, S5_GROUPS, S5_STATE).transpose(1, 0, 2)),
        ], axis=1).astype(BF16)
        cre = _block_diag(s5_c_re[l].transpose(0, 2, 1)).astype(BF16)
        cim = _block_diag(s5_c_im[l].transpose(0, 2, 1)).astype(BF16)
        s5p = (ab_re[l], ab_im[l], bbd, cre, cim, row(s5_d[l]),
               s5_w_glu[l].astype(BF16), row(s5_b_glu[l]))
        lrup = (lru_conv_w[l], row(lru_conv_b[l]),
                _block_diag(lru_w_a[l]).astype(BF16), row(lru_b_a[l]),
                _block_diag(lru_w_x[l]).astype(BF16), row(lru_b_x[l]), row(lru_lambda[l]))

        pa, pg, pb, pc = _in_projection(x_tm, mod[l], g_mix[l], w_in[l].astype(BF16))
        y = _mixer(pa, pg, pb, pc, bsz, tmat, lb_all[l:l + 1], s5p, lrup, row(g_branch[l]))
        x1, h2, q = _out_projection(x_tm, y, mod[l], g_ffn[l],
                                    w_out[l].astype(BF16), peer_w_q[l].astype(BF16))
        tables = _retrieval(q, peer_sub_keys[l].astype(BF16))
        x_tm = _peer_dense(h2, tables, peer_u[l].astype(BF16),
                           peer_v[l].astype(BF16).T, x1, mod[l])

    out = _final_norm(x_tm, g_final)
    return out.transpose(1, 0, 2)
```

```python
import functools

import numpy as np
import jax
import jax.numpy as jnp
from jax import lax
from jax.experimental import pallas as pl
from jax.experimental.pallas import tpu as pltpu

F32 = jnp.float32
BF16 = jnp.bfloat16

EPS = 1e-6
N_MOD = 6
HGRN_HEADS = 4
HGRN_DK = 128
D_HGRN = HGRN_HEADS * HGRN_DK
S5_GROUP = 16
S5_GROUPS = 16
S5_STATE = 64
D_S5 = S5_GROUP * S5_GROUPS
N_S5_STATE = S5_GROUPS * S5_STATE
LRU_BLOCKS = 4
LRU_BLOCK = 64
D_LRU = LRU_BLOCKS * LRU_BLOCK
CONV_W = 4
LRU_C = 8.0
PEER_HEADS = 8
N_KEYS = 128
PEER_DQ = 256
PEER_TOPK = 16

SUBLANES = 8
LANES = 128
VMEM_LIMIT_BYTES = 56 * 1024 * 1024

CHUNK_T = 64
HGRN_GROUP = 8
TOKEN_TILE = 512
EXPERT_TILE = 1024
PEER_TOKEN_TILE = 1024
PEER_PIECE = 256
RETR_TILE = 256
MOD_COL_TILE = 1536


def _sigmoid(x):
    return 1.0 / (1.0 + jnp.exp(-x))


def _silu(x):
    return x * _sigmoid(x)


def _gelu(x):
    return 0.5 * x * (1.0 + jnp.tanh(0.7978845608028654 * (x + 0.044715 * (x * x * x))))


def _dot(a, b):
    return jnp.dot(a, b, preferred_element_type=F32)


def _dot_nt(a, b):
    return lax.dot_general(a, b, (((1,), (1,)), ((), ())), preferred_element_type=F32)


def _dot_tn(a, b):
    return lax.dot_general(a, b, (((0,), (0,)), ((), ())), preferred_element_type=F32)


def _split2(x):
    hi = x.astype(BF16)
    lo = (x - hi.astype(F32)).astype(BF16)
    return hi, lo


def _params(semantics):
    return pltpu.CompilerParams(dimension_semantics=semantics, vmem_limit_bytes=VMEM_LIMIT_BYTES)


def _mod_body(c_ref, w_ref, b_ref, o_ref):
    cond = _silu(c_ref[...])
    c_hi, c_lo = _split2(cond)
    w_hi, w_lo = _split2(w_ref[0])
    o_ref[0] = _dot(c_hi, w_hi) + _dot(c_hi, w_lo) + _dot(c_lo, w_hi) + b_ref[0]


def _modulation(c, w_mod, b_mod):
    depth, d, n = w_mod.shape
    bsz = c.shape[0]
    tn = MOD_COL_TILE
    return pl.pallas_call(
        _mod_body,
        grid=(depth, n // tn),
        in_specs=[
            pl.BlockSpec((bsz, d), lambda l, j: (0, 0)),
            pl.BlockSpec((1, d, tn), lambda l, j: (l, 0, j)),
            pl.BlockSpec((1, 1, tn), lambda l, j: (l, 0, j)),
        ],
        out_specs=pl.BlockSpec((1, bsz, tn), lambda l, j: (l, 0, j)),
        out_shape=jax.ShapeDtypeStruct((depth, bsz, n), F32),
        compiler_params=_params(("arbitrary", "arbitrary")),
    )(c, w_mod, b_mod.reshape(depth, 1, n))


def _prep_body(lbl_ref, are_ref, aim_ref, ldt_ref, bre_ref, bim_ref,
               lb_ref, abr_ref, abi_ref, bbr_ref, bbi_ref):
    logits = lbl_ref[...]
    depth = logits.shape[0]
    rows = [logits[l:l + 1] for l in range(depth)]
    mx = functools.reduce(jnp.maximum, rows)
    ex = [jnp.exp(r - mx) for r in rows]
    den = functools.reduce(lambda a, b: a + b, ex)
    soft = [e / den for e in ex]
    run = soft[0]
    for l in range(depth):
        if l > 0:
            run = run + soft[l]
        lb_ref[l:l + 1, :] = run - soft[0]

    lam_re = jnp.minimum(are_ref[...], -1e-4)
    lam_im = aim_ref[...]
    dt = jnp.exp(ldt_ref[...])
    mag = jnp.exp(lam_re * dt)
    ab_re = mag * jnp.cos(lam_im * dt)
    ab_im = mag * jnp.sin(lam_im * dt)
    den2 = lam_re * lam_re + lam_im * lam_im
    nr = ab_re - 1.0
    z_re = (nr * lam_re + ab_im * lam_im) / den2
    z_im = (ab_im * lam_re - nr * lam_im) / den2
    abr_ref[...] = ab_re
    abi_ref[...] = ab_im
    br = bre_ref[...]
    bi = bim_ref[...]
    bbr_ref[...] = z_re * br - z_im * bi
    bbi_ref[...] = z_re * bi + z_im * br


def _prepare_params(lb_logits, a_re, a_im, log_dt, b_re, b_im):
    depth = lb_logits.shape[0]
    n = N_S5_STATE
    flat = lambda a: a.reshape(depth, 1, n)
    ldt = jnp.broadcast_to(log_dt[:, :, None], (depth, S5_GROUPS, S5_STATE))
    bt = lambda b: b.transpose(0, 3, 1, 2).reshape(depth, S5_GROUP, n)
    out_shape = (
        jax.ShapeDtypeStruct((depth, D_HGRN), F32),
        jax.ShapeDtypeStruct((depth, 1, n), F32),
        jax.ShapeDtypeStruct((depth, 1, n), F32),
        jax.ShapeDtypeStruct((depth, S5_GROUP, n), F32),
        jax.ShapeDtypeStruct((depth, S5_GROUP, n), F32),
    )
    return pl.pallas_call(_prep_body, out_shape=out_shape)(
        lb_logits, flat(a_re), flat(a_im), flat(ldt), bt(b_re), bt(b_im))


def _inproj_body(x_ref, mod_ref, g_ref, w_ref, pa_ref, pg_ref, pb_ref, pc_ref):
    x = x_ref[...]
    tl, bsz, d = x.shape
    ms = jnp.mean(x * x, axis=-1, keepdims=True)
    xn = x * lax.rsqrt(ms + EPS) * g_ref[...]
    h = xn * (1.0 + mod_ref[1]) + mod_ref[0]
    h = h.reshape(tl * bsz, d).astype(BF16)
    na = pa_ref.shape[0] * LANES
    ng = pg_ref.shape[1]
    nb = pb_ref.shape[1]
    pa = _dot(h, w_ref[:, 0:na])
    for cg in range(pa_ref.shape[0]):
        pa_ref[cg] = pa[:, cg * LANES:(cg + 1) * LANES]
    pg_ref[...] = _dot(h, w_ref[:, na:na + ng])
    pb_ref[...] = _dot(h, w_ref[:, na + ng:na + ng + nb])
    pc_ref[...] = _dot(h, w_ref[:, na + ng + nb:])


def _in_projection(x_tm, mod_l, g_mix, w_in16):
    seq, bsz, d = x_tm.shape
    tl = TOKEN_TILE // bsz
    tm = tl * bsz
    rows = seq * bsz
    ncg = 3 * D_HGRN // LANES
    ng, nb, nc = D_HGRN, D_S5, 2 * D_LRU
    return pl.pallas_call(
        _inproj_body,
        grid=(seq // tl,),
        in_specs=[
            pl.BlockSpec((tl, bsz, d), lambda i: (i, 0, 0)),
            pl.BlockSpec((N_MOD, bsz, d), lambda i: (0, 0, 0)),
            pl.BlockSpec((1, d), lambda i: (0, 0)),
            pl.BlockSpec(w_in16.shape, lambda i: (0, 0)),
        ],
        out_specs=[
            pl.BlockSpec((ncg, tm, LANES), lambda i: (0, i, 0)),
            pl.BlockSpec((tm, ng), lambda i: (i, 0)),
            pl.BlockSpec((tm, nb), lambda i: (i, 0)),
            pl.BlockSpec((tm, nc), lambda i: (i, 0)),
        ],
        out_shape=(
            jax.ShapeDtypeStruct((ncg, rows, LANES), F32),
            jax.ShapeDtypeStruct((rows, ng), F32),
            jax.ShapeDtypeStruct((rows, nb), F32),
            jax.ShapeDtypeStruct((rows, nc), F32),
        ),
        compiler_params=_params(("arbitrary",)),
    )(x_tm, mod_l, g_mix.reshape(1, d), w_in16)


def _hgrn_exponent_matrix(chunk):
    n_levels = int(np.log2(chunk))
    mat = np.zeros((n_levels + 2, chunk, chunk), np.float32)
    u = np.arange(chunk)
    for k in range(n_levels):
        g = chunk >> (k + 1)
        m = 2 * g
        for t in range(chunk):
            r = (t // m) * m + g - 1
            if t % m >= g:
                mat[k, t] = (u > r) & (u <= t)
            else:
                mat[k, t] = (u > t) & (u <= r)
    for t in range(chunk):
        mat[n_levels, t] = u <= t
        mat[n_levels + 1, t] = u > t
    return mat.reshape((n_levels + 2) * chunk, chunk)


def _mixer_body(pa_ref, pg_ref, pb_ref, pc_ref, tmat_ref, lb_ref,
                ar_ref, ai_ref, bbd_ref, cre_ref, cim_ref, dsk_ref, wglu_ref, bglu_ref,
                cw_ref, cb_ref, wa_ref, ba_ref, wx_ref, bx_ref, lam_ref, gbr_ref,
                y_ref,
                st_ref, o_scr, bu_scr, xs_scr, s5_state, lru_state, xext, la_scr, lb_scr, hs_scr):
    rows = pa_ref.shape[1]
    bsz = s5_state.shape[0]
    chunk = rows // bsz
    n_levels = tmat_ref.shape[0] // chunk - 2
    dk = HGRN_DK

    @pl.when(pl.program_id(0) == 0)
    def _():
        st_ref[...] = jnp.zeros_like(st_ref)
        s5_state[...] = jnp.zeros_like(s5_state)
        lru_state[...] = jnp.zeros_like(lru_state)
        xext[0:(CONV_W - 1) * bsz, :] = jnp.zeros(((CONV_W - 1) * bsz, D_LRU), F32)

    ti = lax.broadcasted_iota(jnp.int32, (chunk, chunk), 0)
    si = lax.broadcasted_iota(jnp.int32, (chunk, chunk), 1)
    masks = []
    for k in range(n_levels):
        g = chunk >> (k + 1)
        m = 2 * g
        same = (ti // m) == (si // m)
        masks.append(jnp.where(same, jnp.where((ti % m) >= g, jnp.where((si % m) < g, 1.0, 0.0), 0.0), 0.0))
    diag = jnp.where(ti == si, 1.0, 0.0)
    lb = lb_ref[...]
    tmat = tmat_ref[...]

    def hgrn_group(g, carry):
        grp = range(HGRN_GROUP)
        bs = [g * HGRN_GROUP + e for e in grp]
        rws = [pl.ds(b, chunk, stride=bsz) for b in bs]
        heads = lambda e, first: jnp.concatenate(
            [pa_ref[first + h, rws[e], :] for h in range(HGRN_HEADS)], axis=1)
        qa = [_silu(heads(e, 0)) for e in grp]
        f = [lb + (1.0 - lb) * _sigmoid(heads(e, HGRN_HEADS)) for e in grp]
        v16 = [heads(e, 2 * HGRN_HEADS).astype(BF16) for e in grp]
        kk = [1.0 - f[e] for e in grp]
        parts = [_split2(jnp.log(f[e])) for e in grp]
        decay = [jnp.exp(_dot(tmat, parts[e][0]) + _dot(tmat, parts[e][1])) for e in grp]
        cum = [decay[e][n_levels * chunk:(n_levels + 1) * chunk] for e in grp]
        qb16 = [(qa[e] * cum[e]).astype(BF16) for e in grp]
        ks16 = [(kk[e] * decay[e][(n_levels + 1) * chunk:]).astype(BF16) for e in grp]
        total = [cum[e][chunk - 1:chunk, :] for e in grp]
        qa16 = [qa[e].astype(BF16) for e in grp]
        kk16 = [kk[e].astype(BF16) for e in grp]
        for h in range(HGRN_HEADS):
            sl = slice(h * dk, (h + 1) * dk)
            scores = [diag * _dot_nt(qa16[e][:, sl], kk16[e][:, sl]) for e in grp]
            for k in range(n_levels):
                for e in grp:
                    dl = decay[e][k * chunk:(k + 1) * chunk, sl]
                    ql = (qa[e][:, sl] * dl).astype(BF16)
                    kl = (kk[e][:, sl] * dl).astype(BF16)
                    scores[e] = scores[e] + masks[k] * _dot_nt(ql, kl)
            for e in grp:
                st = st_ref[bs[e], h]
                o = (_dot(scores[e].astype(BF16), v16[e][:, sl])
                     + _dot_nt(qb16[e][:, sl], st.astype(BF16)))
                o_scr[h, rws[e], :] = o
                st_ref[bs[e], h] = st * total[e][:, sl] + _dot_tn(v16[e][:, sl], ks16[e][:, sl])
        return carry

    lax.fori_loop(0, bsz // HGRN_GROUP, hgrn_group, 0)

    gate = _silu(pg_ref[...])
    ya = []
    ssq = None
    for h in range(HGRN_HEADS):
        sl = slice(h * dk, (h + 1) * dk)
        oh = o_scr[h]
        oh = oh * lax.rsqrt(jnp.mean(oh * oh, axis=-1, keepdims=True) + EPS) * gate[:, sl]
        ya.append(oh)
        s = jnp.sum(oh * oh, axis=-1, keepdims=True)
        ssq = s if ssq is None else ssq + s
    scale = lax.rsqrt(ssq * (1.0 / D_HGRN) + EPS)
    for h in range(HGRN_HEADS):
        sl = slice(h * dk, (h + 1) * dk)
        y_ref[:, sl] = (ya[h] * scale * gbr_ref[:, sl]).astype(y_ref.dtype)

    n = N_S5_STATE
    u = pb_ref[...]
    bu_scr[...] = _dot(u.astype(BF16), bbd_ref[...])
    ar = jnp.broadcast_to(ar_ref[...], (bsz, n))
    ai = jnp.broadcast_to(ai_ref[...], (bsz, n))

    def s5_step(t, st):
        xr, xi = st
        r = pl.ds(pl.multiple_of(t * bsz, bsz), bsz)
        nxr = ar * xr - ai * xi + bu_scr[r, 0:n]
        nxi = ar * xi + ai * xr + bu_scr[r, n:2 * n]
        xs_scr[r, 0:n] = nxr
        xs_scr[r, n:2 * n] = nxi
        return nxr, nxi

    xr, xi = lax.fori_loop(0, chunk, s5_step, (s5_state[:, 0:n], s5_state[:, n:2 * n]))
    s5_state[:, 0:n] = xr
    s5_state[:, n:2 * n] = xi
    yb = (_dot(xs_scr[:, 0:n].astype(BF16), cre_ref[...])
          - _dot(xs_scr[:, n:2 * n].astype(BF16), cim_ref[...]))
    yb = yb + dsk_ref[...] * u
    z = _gelu(yb)
    yb = z * _sigmoid(_dot(z.astype(BF16), wglu_ref[...]) + bglu_ref[...])
    yb = yb * lax.rsqrt(jnp.mean(yb * yb, axis=-1, keepdims=True) + EPS)
    y_ref[:, D_HGRN:D_HGRN + D_S5] = (yb * gbr_ref[:, D_HGRN:D_HGRN + D_S5]).astype(y_ref.dtype)

    tail = (CONV_W - 1) * bsz
    xin = pc_ref[:, 0:D_LRU]
    xext[tail:tail + rows, :] = xin
    xc = cb_ref[...]
    for w in range(CONV_W):
        xc = xc + cw_ref[w:w + 1, :] * xext[w * bsz:w * bsz + rows, :]
    xext[0:tail, :] = xin[rows - tail:rows, :]
    xc16 = xc.astype(BF16)
    rg = _sigmoid(_dot(xc16, wa_ref[...]) + ba_ref[...])
    ig = _sigmoid(_dot(xc16, wx_ref[...]) + bx_ref[...])
    nl = -lam_ref[...]
    softplus = jnp.maximum(nl, 0.0) + jnp.log(1.0 + jnp.exp(-jnp.abs(nl)))
    log_a = (-LRU_C) * rg * softplus
    la_scr[...] = jnp.exp(log_a)
    lb_scr[...] = jnp.sqrt(1.0 - jnp.exp(2.0 * log_a)) * (ig * xc)

    def lru_step(t, hstate):
        r = pl.ds(pl.multiple_of(t * bsz, bsz), bsz)
        hn = la_scr[r, :] * hstate + lb_scr[r, :]
        hs_scr[r, :] = hn
        return hn

    lru_state[...] = lax.fori_loop(0, chunk, lru_step, lru_state[...])
    yc = hs_scr[...] * _gelu(pc_ref[:, D_LRU:2 * D_LRU])
    yc = yc * lax.rsqrt(jnp.mean(yc * yc, axis=-1, keepdims=True) + EPS)
    y_ref[:, D_HGRN + D_S5:] = (yc * gbr_ref[:, D_HGRN + D_S5:]).astype(y_ref.dtype)


def _mixer(pa, pg, pb, pc, bsz, tmat, lb, s5p, lrup, g_branch):
    rows_total = pg.shape[0]
    rows = CHUNK_T * bsz
    d_mix = D_HGRN + D_S5 + D_LRU
    n = N_S5_STATE
    full = lambda a: pl.BlockSpec(a.shape, lambda i, nd=a.ndim: (0,) * nd)
    consts = [tmat, lb, *s5p, *lrup, g_branch]
    return pl.pallas_call(
        _mixer_body,
        grid=(rows_total // rows,),
        in_specs=[
            pl.BlockSpec((pa.shape[0], rows, LANES), lambda i: (0, i, 0)),
            pl.BlockSpec((rows, pg.shape[1]), lambda i: (i, 0)),
            pl.BlockSpec((rows, pb.shape[1]), lambda i: (i, 0)),
            pl.BlockSpec((rows, pc.shape[1]), lambda i: (i, 0)),
        ] + [full(a) for a in consts],
        out_specs=pl.BlockSpec((rows, d_mix), lambda i: (i, 0)),
        out_shape=jax.ShapeDtypeStruct((rows_total, d_mix), BF16),
        scratch_shapes=[
            pltpu.VMEM((bsz, HGRN_HEADS, HGRN_DK, HGRN_DK), F32),
            pltpu.VMEM((HGRN_HEADS, rows, HGRN_DK), F32),
            pltpu.VMEM((rows, 2 * n), F32),
            pltpu.VMEM((rows, 2 * n), F32),
            pltpu.VMEM((bsz, 2 * n), F32),
            pltpu.VMEM((bsz, D_LRU), F32),
            pltpu.VMEM(((CONV_W - 1) * bsz + rows, D_LRU), F32),
            pltpu.VMEM((rows, D_LRU), F32),
            pltpu.VMEM((rows, D_LRU), F32),
            pltpu.VMEM((rows, D_LRU), F32),
        ],
        compiler_params=_params(("arbitrary",)),
    )(pa, pg, pb, pc, *consts)


def _outproj_body(x_ref, y_ref, mod_ref, g_ref, wo_ref, wq_ref, x1_ref, h2t_ref, q_ref):
    x = x_ref[...]
    tl, bsz, d = x.shape
    o = _dot(y_ref[...], wo_ref[...]).reshape(tl, bsz, d)
    x1 = x + mod_ref[2] * o
    x1_ref[...] = x1
    ms = jnp.mean(x1 * x1, axis=-1, keepdims=True)
    h2 = x1 * lax.rsqrt(ms + EPS) * g_ref[...]
    h2 = h2 * (1.0 + mod_ref[4]) + mod_ref[3]
    h2 = h2.reshape(tl * bsz, d)
    h2t_ref[...] = h2.T.astype(BF16)
    q_ref[...] = _dot(h2.astype(BF16), wq_ref[...])


def _out_projection(x_tm, y, mod_l, g_ffn, w_out16, w_q16):
    seq, bsz, d = x_tm.shape
    tl = TOKEN_TILE // bsz
    tm = tl * bsz
    rows = seq * bsz
    nq = w_q16.shape[1]
    return pl.pallas_call(
        _outproj_body,
        grid=(seq // tl,),
        in_specs=[
            pl.BlockSpec((tl, bsz, d), lambda i: (i, 0, 0)),
            pl.BlockSpec((tm, y.shape[1]), lambda i: (i, 0)),
            pl.BlockSpec((N_MOD, bsz, d), lambda i: (0, 0, 0)),
            pl.BlockSpec((1, d), lambda i: (0, 0)),
            pl.BlockSpec(w_out16.shape, lambda i: (0, 0)),
            pl.BlockSpec(w_q16.shape, lambda i: (0, 0)),
        ],
        out_specs=[
            pl.BlockSpec((tl, bsz, d), lambda i: (i, 0, 0)),
            pl.BlockSpec((d, tm), lambda i: (0, i)),
            pl.BlockSpec((tm, nq), lambda i: (i, 0)),
        ],
        out_shape=(
            jax.ShapeDtypeStruct((seq, bsz, d), F32),
            jax.ShapeDtypeStruct((d, rows), BF16),
            jax.ShapeDtypeStruct((rows, nq), F32),
        ),
        compiler_params=_params(("arbitrary",)),
    )(x_tm, y, mod_l, g_ffn.reshape(1, d), w_out16, w_q16)


def _bitonic_sort_network(n):
    ops = []
    k = 2
    while k <= n:
        j = k // 2
        while j >= 1:
            for i in range(n):
                l = i ^ j
                if l > i:
                    ops.append((i, l, (i & k) == 0))
            j //= 2
        k *= 2
    return ops


def _bitonic_merge_network(n):
    ops = []
    j = n // 2
    while j >= 1:
        for i in range(n):
            l = i ^ j
            if l > i:
                ops.append((i, l, True))
        j //= 2
    return ops


def _apply_network(vals, ops):
    vals = list(vals)
    for i, j, first_max in ops:
        hi = jnp.maximum(vals[i], vals[j])
        lo = jnp.minimum(vals[i], vals[j])
        vals[i], vals[j] = (hi, lo) if first_max else (lo, hi)
    return vals


def _top_sorted(s, k):
    n_keys = s.shape[0]
    n_slabs = n_keys // SUBLANES
    assert n_slabs == k
    slabs = [s[i * SUBLANES:(i + 1) * SUBLANES, :] for i in range(n_slabs)]
    slabs = _apply_network(slabs, _bitonic_sort_network(k))
    merge = _bitonic_merge_network(k)
    shift = SUBLANES // 2
    while shift >= 1:
        rolled = [pltpu.roll(v, shift, 0) for v in slabs]
        slabs = [jnp.maximum(slabs[i], rolled[k - 1 - i]) for i in range(k)]
        slabs = _apply_network(slabs, merge)
        shift //= 2
    return slabs


def _staircase(k):
    return [(x, y) for x in range(k) for y in range(k) if (x + 1) * (y + 1) <= k]


def _select_tree(masks, leaves):
    if not masks:
        return leaves[0]
    half = len(leaves) // 2
    return jnp.where(masks[0], _select_tree(masks[1:], leaves[half:]), _select_tree(masks[1:], leaves[:half]))


def _prefix_count(pred, vals):
    n = len(vals)
    full = pred(vals[n - 1])
    masks = []
    count = None
    stride = n // 2
    while stride >= 1:
        pivot = _select_tree(masks, [vals[base + stride - 1] for base in range(0, n, 2 * stride)])
        hit = pred(pivot)
        masks.append(hit)
        term = jnp.where(hit, float(stride), 0.0)
        count = term if count is None else count + term
        stride //= 2
    return jnp.where(full, float(n), count)


def _retrieval_body(q_ref, keys_ref, r1_ref, e1_ref, n0_ref, c0_ref, s_scr, a_scr, b_scr,
                    afull_scr, bfull_scr):
    k = PEER_TOPK
    half = PEER_DQ // 2
    tok = q_ref.shape[0]
    neg = jnp.full((SUBLANES, tok), -jnp.inf, F32)

    for h in range(PEER_HEADS):
        for p in range(2):
            off = h * PEER_DQ + p * half
            s = _dot_nt(keys_ref[p], q_ref[:, off:off + half].astype(BF16))
            s_scr[h, p] = s
            top = _top_sorted(s, k)
            dst, dst_full = (a_scr, afull_scr) if p == 0 else (b_scr, bfull_scr)
            for x in range(k):
                dst[x, h:h + 1, :] = top[x][0:1, :]
                dst_full[h, x] = top[x]

    a = [a_scr[x] for x in range(k)]
    b = [b_scr[y] for y in range(k)]
    pairs = _staircase(k)
    cands = [a[x] + b[y] for x, y in pairs]
    n_sort = 1
    while n_sort < len(cands):
        n_sort *= 2
    ranked = _apply_network(cands + [neg] * (n_sort - len(cands)), _bitonic_sort_network(n_sort))
    tau = ranked[k - 1]
    best = a[0] + b[0]
    z = jnp.zeros_like(tau)
    for cnd in cands:
        z = z + jnp.where(cnd >= tau, jnp.exp(cnd - best), 0.0)
    inv_z = 1.0 / z

    pack = 2 * SUBLANES
    for h in range(PEER_HEADS):
        tau_h = jnp.broadcast_to(tau[h:h + 1, :], (SUBLANES, tok))
        inv_z_h = jnp.broadcast_to(inv_z[h:h + 1, :], (SUBLANES, tok))
        b_h = [bfull_scr[h, y] for y in range(k)]
        a_first = afull_scr[h, 0]
        a_last = afull_scr[h, k - 1]
        for blk in range(N_KEYS // pack):
            ranks, exps = [], []
            for sub in range(pack // SUBLANES):
                rows = slice(blk * pack + sub * SUBLANES, blk * pack + (sub + 1) * SUBLANES)
                s0 = s_scr[h, 0, rows, :]
                s1 = s_scr[h, 1, rows, :]
                ranks.append(_prefix_count(lambda v: v > s1, b_h))
                cnt = _prefix_count(lambda v: s0 + v >= tau_h, b_h)
                n0_ref[h, rows, :] = jnp.where(s0 >= a_last, cnt, 0.0)
                c0_ref[h, rows, :] = jnp.exp(s0 - a_first) * inv_z_h
                exps.append(jnp.exp(s1 - b_h[0]))
            prow = slice(blk * pack, (blk + 1) * pack)
            r1_ref[h, prow, :] = jnp.concatenate(ranks, axis=0).astype(r1_ref.dtype)
            e1_ref[h, prow, :] = jnp.concatenate(exps, axis=0).astype(e1_ref.dtype)


def _retrieval(q, keys16):
    rows = q.shape[0]
    tok = RETR_TILE
    tab = lambda dt: jax.ShapeDtypeStruct((PEER_HEADS, N_KEYS, rows), dt)
    tspec = pl.BlockSpec((PEER_HEADS, N_KEYS, tok), lambda i: (0, 0, i))
    return pl.pallas_call(
        _retrieval_body,
        grid=(rows // tok,),
        in_specs=[
            pl.BlockSpec((tok, q.shape[1]), lambda i: (i, 0)),
            pl.BlockSpec(keys16.shape, lambda i: (0, 0, 0)),
        ],
        out_specs=[tspec, tspec, tspec, tspec],
        out_shape=(tab(BF16), tab(BF16), tab(F32), tab(F32)),
        scratch_shapes=[
            pltpu.VMEM((PEER_HEADS, 2, N_KEYS, tok), F32),
            pltpu.VMEM((PEER_TOPK, SUBLANES, tok), F32),
            pltpu.VMEM((PEER_TOPK, SUBLANES, tok), F32),
            pltpu.VMEM((PEER_HEADS, PEER_TOPK, SUBLANES, tok), F32),
            pltpu.VMEM((PEER_HEADS, PEER_TOPK, SUBLANES, tok), F32),
        ],
        compiler_params=_params(("arbitrary",)),
    )(q, keys16)


def _peer_body(h2t_ref, r1_ref, e1_ref, n0_ref, c0_ref, u_ref, vt_ref, x1_ref, mod_ref,
               x2_ref, acc_scr, a_scr, p_scr):
    j = pl.program_id(1)
    n_j = pl.num_programs(1)
    te = u_ref.shape[0]
    tok = h2t_ref.shape[1]
    rows_per_step = te // N_KEYS
    pack = 2 * SUBLANES

    @pl.when(j == 0)
    def _():
        acc_scr[...] = jnp.zeros_like(acc_scr)
        a_scr[...] = _dot(u_ref[...], h2t_ref[...]).astype(BF16)

    @pl.when(j > 0)
    def _():
        zero = jnp.zeros((pack, PEER_PIECE), BF16)
        for c in range(tok // PEER_PIECE):
            cols = slice(c * PEER_PIECE, (c + 1) * PEER_PIECE)
            for r in range(rows_per_step):
                gates = [zero for _ in range(N_KEYS // pack)]
                for h in range(PEER_HEADS):
                    nb = jnp.broadcast_to(n0_ref[h, r:r + 1, cols], (pack, PEER_PIECE)).astype(BF16)
                    cb = jnp.broadcast_to(c0_ref[h, r:r + 1, cols], (pack, PEER_PIECE)).astype(BF16)
                    for k in range(N_KEYS // pack):
                        rs = slice(k * pack, (k + 1) * pack)
                        sel = jnp.where(r1_ref[h, rs, cols] < nb, e1_ref[h, rs, cols], zero)
                        gates[k] = gates[k] + sel * cb
                for k in range(N_KEYS // pack):
                    rs = slice(r * N_KEYS + k * pack, r * N_KEYS + (k + 1) * pack)
                    p_scr[rs, cols] = gates[k] * _gelu(a_scr[rs, cols])
            acc_scr[:, cols] += _dot(vt_ref[...], p_scr[:, cols])
            a_scr[:, cols] = _dot(u_ref[...], h2t_ref[:, cols]).astype(BF16)

    @pl.when(j == n_j - 1)
    def _():
        tl, bsz, d = x1_ref.shape
        out = acc_scr[...].T.reshape(tl, bsz, d)
        x2_ref[...] = x1_ref[...] + mod_ref[5] * out


def _peer_dense(h2t, tables, u16, vt16, x1, mod_l):
    seq, bsz, d = x1.shape
    tok = PEER_TOKEN_TILE
    tl = tok // bsz
    te = EXPERT_TILE
    n_tiles = u16.shape[0] // te
    rows_per_step = te // N_KEYS
    cur = lambda j: jnp.minimum(j, n_tiles - 1)
    prev = lambda j: jnp.maximum(j - 1, 0)
    tspec = pl.BlockSpec((PEER_HEADS, N_KEYS, tok), lambda i, j: (0, 0, i))
    rspec = pl.BlockSpec((PEER_HEADS, rows_per_step, tok), lambda i, j: (0, prev(j), i))
    return pl.pallas_call(
        _peer_body,
        grid=(seq // tl, n_tiles + 1),
        in_specs=[
            pl.BlockSpec((d, tok), lambda i, j: (0, i)),
            tspec, tspec, rspec, rspec,
            pl.BlockSpec((te, d), lambda i, j: (cur(j), 0)),
            pl.BlockSpec((d, te), lambda i, j: (0, prev(j))),
            pl.BlockSpec((tl, bsz, d), lambda i, j: (i, 0, 0)),
            pl.BlockSpec((N_MOD, bsz, d), lambda i, j: (0, 0, 0)),
        ],
        out_specs=pl.BlockSpec((tl, bsz, d), lambda i, j: (i, 0, 0)),
        out_shape=jax.ShapeDtypeStruct((seq, bsz, d), F32),
        scratch_shapes=[
            pltpu.VMEM((d, tok), F32),
            pltpu.VMEM((te, tok), BF16),
            pltpu.VMEM((te, tok), BF16),
        ],
        compiler_params=_params(("arbitrary", "arbitrary")),
    )(h2t, *tables, u16, vt16, x1, mod_l)


def _final_body(x_ref, g_ref, o_ref):
    x = x_ref[...]
    o_ref[...] = x * lax.rsqrt(jnp.mean(x * x, axis=-1, keepdims=True) + EPS) * g_ref[...]


def _final_norm(x_tm, g):
    seq, bsz, d = x_tm.shape
    tl = TOKEN_TILE // bsz
    return pl.pallas_call(
        _final_body,
        grid=(seq // tl,),
        in_specs=[pl.BlockSpec((tl, bsz, d), lambda i: (i, 0, 0)),
                  pl.BlockSpec((1, d), lambda i: (0, 0))],
        out_specs=pl.BlockSpec((tl, bsz, d), lambda i: (i, 0, 0)),
        out_shape=jax.ShapeDtypeStruct((seq, bsz, d), F32),
        compiler_params=_params(("arbitrary",)),
    )(x_tm, g.reshape(1, d))


def _block_diag(blocks):
    n, r, c = blocks.shape
    eye = jnp.eye(n, dtype=blocks.dtype)
    return (blocks[:, :, None, :] * eye[:, None, :, None]).reshape(n * r, n * c)


def kernel(x, c, w_mod, b_mod, g_mix, w_in, hgrn_lb_logits, s5_a_re, s5_a_im, s5_b_re, s5_b_im, s5_c_re, s5_c_im, s5_log_dt, s5_d, s5_w_glu, s5_b_glu, lru_conv_w, lru_conv_b, lru_w_a, lru_b_a, lru_w_x, lru_b_x, lru_lambda, g_branch, w_out, g_ffn, peer_w_q, peer_sub_keys, peer_u, peer_v, g_final):
    bsz, seq, d = x.shape
    depth = w_in.shape[0]
    assert bsz == SUBLANES and seq % CHUNK_T == 0 and (seq * bsz) % TOKEN_TILE == 0

    x_tm = x.transpose(1, 0, 2)
    mod = _modulation(c, w_mod, b_mod)
    mod = mod.reshape(depth, bsz, N_MOD, d).transpose(0, 2, 1, 3)
    lb_all, ab_re, ab_im, bb_re, bb_im = _prepare_params(
        hgrn_lb_logits, s5_a_re, s5_a_im, s5_log_dt, s5_b_re, s5_b_im)
    tmat = jnp.asarray(_hgrn_exponent_matrix(CHUNK_T), BF16)
    row = lambda a: a.reshape(1, -1)

    for l in range(depth):
        bbd = jnp.concatenate([
            _block_diag(bb_re[l].reshape(S5_GROUP, S5_GROUPS, S5_STATE).transpose(1, 0, 2)),
            _block_diag(bb_im[l].reshape(S5_GROUP, S5_GROUPS, S5_STATE).transpose(1, 0, 2)),
        ], axis=1).astype(BF16)
        cre = _block_diag(s5_c_re[l].transpose(0, 2, 1)).astype(BF16)
        cim = _block_diag(s5_c_im[l].transpose(0, 2, 1)).astype(BF16)
        s5p = (ab_re[l], ab_im[l], bbd, cre, cim, row(s5_d[l]),
               s5_w_glu[l].astype(BF16), row(s5_b_glu[l]))
        lrup = (lru_conv_w[l], row(lru_conv_b[l]),
                _block_diag(lru_w_a[l]).astype(BF16), row(lru_b_a[l]),
                _block_diag(lru_w_x[l]).astype(BF16), row(lru_b_x[l]), row(lru_lambda[l]))

        pa, pg, pb, pc = _in_projection(x_tm, mod[l], g_mix[l], w_in[l].astype(BF16))
        y = _mixer(pa, pg, pb, pc, bsz, tmat, lb_all[l:l + 1], s5p, lrup, row(g_branch[l]))
        x1, h2t, q = _out_projection(x_tm, y, mod[l], g_ffn[l],
                                    w_out[l].astype(BF16), peer_w_q[l].astype(BF16))
        tables = _retrieval(q, peer_sub_keys[l].astype(BF16))
        x_tm = _peer_dense(h2t, tables, peer_u[l].astype(BF16),
                           peer_v[l].astype(BF16).T, x1, mod[l])

    out = _final_norm(x_tm, g_final)
    return out.transpose(1, 0, 2)
```

```python
import functools

import numpy as np
import jax
import jax.numpy as jnp
from jax import lax
from jax.experimental import pallas as pl
from jax.experimental.pallas import tpu as pltpu

F32 = jnp.float32
BF16 = jnp.bfloat16

EPS = 1e-6
N_MOD = 6
HGRN_HEADS = 4
HGRN_DK = 128
D_HGRN = HGRN_HEADS * HGRN_DK
S5_GROUP = 16
S5_GROUPS = 16
S5_STATE = 64
D_S5 = S5_GROUP * S5_GROUPS
N_S5_STATE = S5_GROUPS * S5_STATE
LRU_BLOCKS = 4
LRU_BLOCK = 64
D_LRU = LRU_BLOCKS * LRU_BLOCK
CONV_W = 4
LRU_C = 8.0
PEER_HEADS = 8
N_KEYS = 128
PEER_DQ = 256
PEER_TOPK = 16

SUBLANES = 8
LANES = 128
VMEM_LIMIT_BYTES = 56 * 1024 * 1024

CHUNK_T = 64
HGRN_GROUP = 8
TOKEN_TILE = 512
ROW_PARTS = 2
EXPERT_TILE = 1024
PEER_TOKEN_TILE = 1024
PEER_PIECE = 256
RETR_TILE = 256
MOD_COL_TILE = 1536


def _sigmoid(x):
    return 1.0 / (1.0 + jnp.exp(-x))


def _silu(x):
    return x * _sigmoid(x)


def _gelu(x):
    return 0.5 * x * (1.0 + jnp.tanh(0.7978845608028654 * (x + 0.044715 * (x * x * x))))


def _dot(a, b):
    return jnp.dot(a, b, preferred_element_type=F32)


def _dot_nt(a, b):
    return lax.dot_general(a, b, (((1,), (1,)), ((), ())), preferred_element_type=F32)


def _dot_tn(a, b):
    return lax.dot_general(a, b, (((0,), (0,)), ((), ())), preferred_element_type=F32)


def _split2(x):
    hi = x.astype(BF16)
    lo = (x - hi.astype(F32)).astype(BF16)
    return hi, lo


def _params(semantics):
    return pltpu.CompilerParams(dimension_semantics=semantics, vmem_limit_bytes=VMEM_LIMIT_BYTES)


def _mod_body(c_ref, w_ref, b_ref, o_ref):
    cond = _silu(c_ref[...])
    c_hi, c_lo = _split2(cond)
    w_hi, w_lo = _split2(w_ref[0])
    o_ref[0] = _dot(c_hi, w_hi) + _dot(c_hi, w_lo) + _dot(c_lo, w_hi) + b_ref[0]


def _modulation(c, w_mod, b_mod):
    depth, d, n = w_mod.shape
    bsz = c.shape[0]
    tn = MOD_COL_TILE
    return pl.pallas_call(
        _mod_body,
        grid=(depth, n // tn),
        in_specs=[
            pl.BlockSpec((bsz, d), lambda l, j: (0, 0)),
            pl.BlockSpec((1, d, tn), lambda l, j: (l, 0, j)),
            pl.BlockSpec((1, 1, tn), lambda l, j: (l, 0, j)),
        ],
        out_specs=pl.BlockSpec((1, bsz, tn), lambda l, j: (l, 0, j)),
        out_shape=jax.ShapeDtypeStruct((depth, bsz, n), F32),
        compiler_params=_params(("arbitrary", "arbitrary")),
    )(c, w_mod, b_mod.reshape(depth, 1, n))


def _prep_body(lbl_ref, are_ref, aim_ref, ldt_ref, bre_ref, bim_ref,
               lb_ref, abr_ref, abi_ref, bbr_ref, bbi_ref):
    logits = lbl_ref[...]
    depth = logits.shape[0]
    rows = [logits[l:l + 1] for l in range(depth)]
    mx = functools.reduce(jnp.maximum, rows)
    ex = [jnp.exp(r - mx) for r in rows]
    den = functools.reduce(lambda a, b: a + b, ex)
    soft = [e / den for e in ex]
    run = soft[0]
    for l in range(depth):
        if l > 0:
            run = run + soft[l]
        lb_ref[l:l + 1, :] = run - soft[0]

    lam_re = jnp.minimum(are_ref[...], -1e-4)
    lam_im = aim_ref[...]
    dt = jnp.exp(ldt_ref[...])
    mag = jnp.exp(lam_re * dt)
    ab_re = mag * jnp.cos(lam_im * dt)
    ab_im = mag * jnp.sin(lam_im * dt)
    den2 = lam_re * lam_re + lam_im * lam_im
    nr = ab_re - 1.0
    z_re = (nr * lam_re + ab_im * lam_im) / den2
    z_im = (ab_im * lam_re - nr * lam_im) / den2
    abr_ref[...] = ab_re
    abi_ref[...] = ab_im
    br = bre_ref[...]
    bi = bim_ref[...]
    bbr_ref[...] = z_re * br - z_im * bi
    bbi_ref[...] = z_re * bi + z_im * br


def _prepare_params(lb_logits, a_re, a_im, log_dt, b_re, b_im):
    depth = lb_logits.shape[0]
    n = N_S5_STATE
    flat = lambda a: a.reshape(depth, 1, n)
    ldt = jnp.broadcast_to(log_dt[:, :, None], (depth, S5_GROUPS, S5_STATE))
    bt = lambda b: b.transpose(0, 3, 1, 2).reshape(depth, S5_GROUP, n)
    out_shape = (
        jax.ShapeDtypeStruct((depth, D_HGRN), F32),
        jax.ShapeDtypeStruct((depth, 1, n), F32),
        jax.ShapeDtypeStruct((depth, 1, n), F32),
        jax.ShapeDtypeStruct((depth, S5_GROUP, n), F32),
        jax.ShapeDtypeStruct((depth, S5_GROUP, n), F32),
    )
    return pl.pallas_call(_prep_body, out_shape=out_shape)(
        lb_logits, flat(a_re), flat(a_im), flat(ldt), bt(b_re), bt(b_im))


def _inproj_body(x_ref, mod_ref, g_ref, w_ref, pa_ref, pg_ref, pb_ref, pc_ref):
    tl, bsz, d = x_ref.shape
    na = pa_ref.shape[0] * LANES
    ng = pg_ref.shape[1]
    nb = pb_ref.shape[1]
    part = tl // ROW_PARTS
    for s in range(ROW_PARTS):
        x = x_ref[s * part:(s + 1) * part]
        ms = jnp.mean(x * x, axis=-1, keepdims=True)
        xn = x * lax.rsqrt(ms + EPS) * g_ref[...]
        h = xn * (1.0 + mod_ref[1]) + mod_ref[0]
        h = h.reshape(part * bsz, d).astype(BF16)
        rows = slice(s * part * bsz, (s + 1) * part * bsz)
        pa = _dot(h, w_ref[:, 0:na])
        for cg in range(pa_ref.shape[0]):
            pa_ref[cg, rows, :] = pa[:, cg * LANES:(cg + 1) * LANES]
        pg_ref[rows, :] = _dot(h, w_ref[:, na:na + ng])
        pb_ref[rows, :] = _dot(h, w_ref[:, na + ng:na + ng + nb])
        pc_ref[rows, :] = _dot(h, w_ref[:, na + ng + nb:])


def _in_projection(x_tm, mod_l, g_mix, w_in16):
    seq, bsz, d = x_tm.shape
    tl = TOKEN_TILE // bsz
    tm = tl * bsz
    rows = seq * bsz
    ncg = 3 * D_HGRN // LANES
    ng, nb, nc = D_HGRN, D_S5, 2 * D_LRU
    return pl.pallas_call(
        _inproj_body,
        grid=(seq // tl,),
        in_specs=[
            pl.BlockSpec((tl, bsz, d), lambda i: (i, 0, 0)),
            pl.BlockSpec((N_MOD, bsz, d), lambda i: (0, 0, 0)),
            pl.BlockSpec((1, d), lambda i: (0, 0)),
            pl.BlockSpec(w_in16.shape, lambda i: (0, 0)),
        ],
        out_specs=[
            pl.BlockSpec((ncg, tm, LANES), lambda i: (0, i, 0)),
            pl.BlockSpec((tm, ng), lambda i: (i, 0)),
            pl.BlockSpec((tm, nb), lambda i: (i, 0)),
            pl.BlockSpec((tm, nc), lambda i: (i, 0)),
        ],
        out_shape=(
            jax.ShapeDtypeStruct((ncg, rows, LANES), F32),
            jax.ShapeDtypeStruct((rows, ng), F32),
            jax.ShapeDtypeStruct((rows, nb), F32),
            jax.ShapeDtypeStruct((rows, nc), F32),
        ),
        compiler_params=_params(("arbitrary",)),
    )(x_tm, mod_l, g_mix.reshape(1, d), w_in16)


def _hgrn_exponent_matrix(chunk):
    n_levels = int(np.log2(chunk))
    mat = np.zeros((n_levels + 2, chunk, chunk), np.float32)
    u = np.arange(chunk)
    for k in range(n_levels):
        g = chunk >> (k + 1)
        m = 2 * g
        for t in range(chunk):
            r = (t // m) * m + g - 1
            if t % m >= g:
                mat[k, t] = (u > r) & (u <= t)
            else:
                mat[k, t] = (u > t) & (u <= r)
    for t in range(chunk):
        mat[n_levels, t] = u <= t
        mat[n_levels + 1, t] = u > t
    return mat.reshape((n_levels + 2) * chunk, chunk)


def _mixer_body(pa_ref, pg_ref, pb_ref, pc_ref, tmat_ref, lb_ref,
                ar_ref, ai_ref, bbd_ref, cre_ref, cim_ref, dsk_ref, wglu_ref, bglu_ref,
                cw_ref, cb_ref, wa_ref, ba_ref, wx_ref, bx_ref, lam_ref, gbr_ref,
                y_ref,
                st_ref, o_scr, bu_scr, xs_scr, s5_state, lru_state, xext, la_scr, lb_scr, hs_scr):
    rows = pa_ref.shape[1]
    bsz = s5_state.shape[0]
    chunk = rows // bsz
    n_levels = tmat_ref.shape[0] // chunk - 2
    dk = HGRN_DK

    @pl.when(pl.program_id(0) == 0)
    def _():
        st_ref[...] = jnp.zeros_like(st_ref)
        s5_state[...] = jnp.zeros_like(s5_state)
        lru_state[...] = jnp.zeros_like(lru_state)
        xext[0:(CONV_W - 1) * bsz, :] = jnp.zeros(((CONV_W - 1) * bsz, D_LRU), F32)

    ti = lax.broadcasted_iota(jnp.int32, (chunk, chunk), 0)
    si = lax.broadcasted_iota(jnp.int32, (chunk, chunk), 1)
    masks = []
    for k in range(n_levels):
        g = chunk >> (k + 1)
        m = 2 * g
        same = (ti // m) == (si // m)
        masks.append(jnp.where(same, jnp.where((ti % m) >= g, jnp.where((si % m) < g, 1.0, 0.0), 0.0), 0.0))
    diag = jnp.where(ti == si, 1.0, 0.0)
    lb = lb_ref[...]
    tmat = tmat_ref[...]

    def hgrn_group(g, carry):
        grp = range(HGRN_GROUP)
        bs = [g * HGRN_GROUP + e for e in grp]
        rws = [pl.ds(b, chunk, stride=bsz) for b in bs]
        heads = lambda e, first: jnp.concatenate(
            [pa_ref[first + h, rws[e], :] for h in range(HGRN_HEADS)], axis=1)
        qa = [_silu(heads(e, 0)) for e in grp]
        f = [lb + (1.0 - lb) * _sigmoid(heads(e, HGRN_HEADS)) for e in grp]
        v16 = [heads(e, 2 * HGRN_HEADS).astype(BF16) for e in grp]
        kk = [1.0 - f[e] for e in grp]
        parts = [_split2(jnp.log(f[e])) for e in grp]
        decay = [jnp.exp(_dot(tmat, parts[e][0]) + _dot(tmat, parts[e][1])) for e in grp]
        cum = [decay[e][n_levels * chunk:(n_levels + 1) * chunk] for e in grp]
        qb16 = [(qa[e] * cum[e]).astype(BF16) for e in grp]
        ks16 = [(kk[e] * decay[e][(n_levels + 1) * chunk:]).astype(BF16) for e in grp]
        total = [cum[e][chunk - 1:chunk, :] for e in grp]
        qa16 = [qa[e].astype(BF16) for e in grp]
        kk16 = [kk[e].astype(BF16) for e in grp]
        for h in range(HGRN_HEADS):
            sl = slice(h * dk, (h + 1) * dk)
            scores = [diag * _dot_nt(qa16[e][:, sl], kk16[e][:, sl]) for e in grp]
            for k in range(n_levels):
                for e in grp:
                    dl = decay[e][k * chunk:(k + 1) * chunk, sl]
                    ql = (qa[e][:, sl] * dl).astype(BF16)
                    kl = (kk[e][:, sl] * dl).astype(BF16)
                    scores[e] = scores[e] + masks[k] * _dot_nt(ql, kl)
            for e in grp:
                st = st_ref[bs[e], h]
                o = (_dot(scores[e].astype(BF16), v16[e][:, sl])
                     + _dot_nt(qb16[e][:, sl], st.astype(BF16)))
                o_scr[h, rws[e], :] = o
                st_ref[bs[e], h] = st * total[e][:, sl] + _dot_tn(v16[e][:, sl], ks16[e][:, sl])
        return carry

    lax.fori_loop(0, bsz // HGRN_GROUP, hgrn_group, 0)

    gate = _silu(pg_ref[...])
    ya = []
    ssq = None
    for h in range(HGRN_HEADS):
        sl = slice(h * dk, (h + 1) * dk)
        oh = o_scr[h]
        oh = oh * lax.rsqrt(jnp.mean(oh * oh, axis=-1, keepdims=True) + EPS) * gate[:, sl]
        ya.append(oh)
        s = jnp.sum(oh * oh, axis=-1, keepdims=True)
        ssq = s if ssq is None else ssq + s
    scale = lax.rsqrt(ssq * (1.0 / D_HGRN) + EPS)
    for h in range(HGRN_HEADS):
        sl = slice(h * dk, (h + 1) * dk)
        y_ref[:, sl] = (ya[h] * scale * gbr_ref[:, sl]).astype(y_ref.dtype)

    n = N_S5_STATE
    u = pb_ref[...]
    bu_scr[...] = _dot(u.astype(BF16), bbd_ref[...])
    ar = jnp.broadcast_to(ar_ref[...], (bsz, n))
    ai = jnp.broadcast_to(ai_ref[...], (bsz, n))

    tail = (CONV_W - 1) * bsz
    xin = pc_ref[:, 0:D_LRU]
    xext[tail:tail + rows, :] = xin
    xc = cb_ref[...]
    for w in range(CONV_W):
        xc = xc + cw_ref[w:w + 1, :] * xext[w * bsz:w * bsz + rows, :]
    xext[0:tail, :] = xin[rows - tail:rows, :]
    xc16 = xc.astype(BF16)
    rg = _sigmoid(_dot(xc16, wa_ref[...]) + ba_ref[...])
    ig = _sigmoid(_dot(xc16, wx_ref[...]) + bx_ref[...])
    nl = -lam_ref[...]
    softplus = jnp.maximum(nl, 0.0) + jnp.log(1.0 + jnp.exp(-jnp.abs(nl)))
    log_a = (-LRU_C) * rg * softplus
    la_scr[...] = jnp.exp(log_a)
    lb_scr[...] = jnp.sqrt(1.0 - jnp.exp(2.0 * log_a)) * (ig * xc)

    def scan_step(t, st):
        xr, xi, hstate = st
        r = pl.ds(pl.multiple_of(t * bsz, bsz), bsz)
        nxr = ar * xr - ai * xi + bu_scr[r, 0:n]
        nxi = ar * xi + ai * xr + bu_scr[r, n:2 * n]
        xs_scr[r, 0:n] = nxr
        xs_scr[r, n:2 * n] = nxi
        hn = la_scr[r, :] * hstate + lb_scr[r, :]
        hs_scr[r, :] = hn
        return nxr, nxi, hn

    xr, xi, hlast = lax.fori_loop(
        0, chunk, scan_step, (s5_state[:, 0:n], s5_state[:, n:2 * n], lru_state[...]))
    s5_state[:, 0:n] = xr
    s5_state[:, n:2 * n] = xi
    lru_state[...] = hlast

    yb = (_dot(xs_scr[:, 0:n].astype(BF16), cre_ref[...])
          - _dot(xs_scr[:, n:2 * n].astype(BF16), cim_ref[...]))
    yb = yb + dsk_ref[...] * u
    z = _gelu(yb)
    yb = z * _sigmoid(_dot(z.astype(BF16), wglu_ref[...]) + bglu_ref[...])
    yb = yb * lax.rsqrt(jnp.mean(yb * yb, axis=-1, keepdims=True) + EPS)
    y_ref[:, D_HGRN:D_HGRN + D_S5] = (yb * gbr_ref[:, D_HGRN:D_HGRN + D_S5]).astype(y_ref.dtype)

    yc = hs_scr[...] * _gelu(pc_ref[:, D_LRU:2 * D_LRU])
    yc = yc * lax.rsqrt(jnp.mean(yc * yc, axis=-1, keepdims=True) + EPS)
    y_ref[:, D_HGRN + D_S5:] = (yc * gbr_ref[:, D_HGRN + D_S5:]).astype(y_ref.dtype)


def _mixer(pa, pg, pb, pc, bsz, tmat, lb, s5p, lrup, g_branch):
    rows_total = pg.shape[0]
    rows = CHUNK_T * bsz
    d_mix = D_HGRN + D_S5 + D_LRU
    n = N_S5_STATE
    full = lambda a: pl.BlockSpec(a.shape, lambda i, nd=a.ndim: (0,) * nd)
    consts = [tmat, lb, *s5p, *lrup, g_branch]
    return pl.pallas_call(
        _mixer_body,
        grid=(rows_total // rows,),
        in_specs=[
            pl.BlockSpec((pa.shape[0], rows, LANES), lambda i: (0, i, 0)),
            pl.BlockSpec((rows, pg.shape[1]), lambda i: (i, 0)),
            pl.BlockSpec((rows, pb.shape[1]), lambda i: (i, 0)),
            pl.BlockSpec((rows, pc.shape[1]), lambda i: (i, 0)),
        ] + [full(a) for a in consts],
        out_specs=pl.BlockSpec((rows, d_mix), lambda i: (i, 0)),
        out_shape=jax.ShapeDtypeStruct((rows_total, d_mix), BF16),
        scratch_shapes=[
            pltpu.VMEM((bsz, HGRN_HEADS, HGRN_DK, HGRN_DK), F32),
            pltpu.VMEM((HGRN_HEADS, rows, HGRN_DK), F32),
            pltpu.VMEM((rows, 2 * n), F32),
            pltpu.VMEM((rows, 2 * n), F32),
            pltpu.VMEM((bsz, 2 * n), F32),
            pltpu.VMEM((bsz, D_LRU), F32),
            pltpu.VMEM(((CONV_W - 1) * bsz + rows, D_LRU), F32),
            pltpu.VMEM((rows, D_LRU), F32),
            pltpu.VMEM((rows, D_LRU), F32),
            pltpu.VMEM((rows, D_LRU), F32),
        ],
        compiler_params=_params(("arbitrary",)),
    )(pa, pg, pb, pc, *consts)


def _outproj_body(x_ref, y_ref, mod_ref, g_ref, wo_ref, wq_ref, x1_ref, h2t_ref, q_ref):
    tl, bsz, d = x_ref.shape
    part = tl // ROW_PARTS
    parts = range(ROW_PARTS)
    rows = [slice(s * part * bsz, (s + 1) * part * bsz) for s in parts]
    o = [_dot(y_ref[rows[s], :], wo_ref[...]).reshape(part, bsz, d) for s in parts]
    x1 = [x_ref[s * part:(s + 1) * part] + mod_ref[2] * o[s] for s in parts]
    h2 = []
    for s in parts:
        x1_ref[s * part:(s + 1) * part] = x1[s]
        ms = jnp.mean(x1[s] * x1[s], axis=-1, keepdims=True)
        hs = x1[s] * lax.rsqrt(ms + EPS) * g_ref[...]
        hs = hs * (1.0 + mod_ref[4]) + mod_ref[3]
        h2.append(hs.reshape(part * bsz, d))
    for s in parts:
        q_ref[rows[s], :] = _dot(h2[s].astype(BF16), wq_ref[...])
        h2t_ref[:, rows[s]] = h2[s].T.astype(BF16)


def _out_projection(x_tm, y, mod_l, g_ffn, w_out16, w_q16):
    seq, bsz, d = x_tm.shape
    tl = TOKEN_TILE // bsz
    tm = tl * bsz
    rows = seq * bsz
    nq = w_q16.shape[1]
    return pl.pallas_call(
        _outproj_body,
        grid=(seq // tl,),
        in_specs=[
            pl.BlockSpec((tl, bsz, d), lambda i: (i, 0, 0)),
            pl.BlockSpec((tm, y.shape[1]), lambda i: (i, 0)),
            pl.BlockSpec((N_MOD, bsz, d), lambda i: (0, 0, 0)),
            pl.BlockSpec((1, d), lambda i: (0, 0)),
            pl.BlockSpec(w_out16.shape, lambda i: (0, 0)),
            pl.BlockSpec(w_q16.shape, lambda i: (0, 0)),
        ],
        out_specs=[
            pl.BlockSpec((tl, bsz, d), lambda i: (i, 0, 0)),
            pl.BlockSpec((d, tm), lambda i: (0, i)),
            pl.BlockSpec((tm, nq), lambda i: (i, 0)),
        ],
        out_shape=(
            jax.ShapeDtypeStruct((seq, bsz, d), F32),
            jax.ShapeDtypeStruct((d, rows), BF16),
            jax.ShapeDtypeStruct((rows, nq), F32),
        ),
        compiler_params=_params(("arbitrary",)),
    )(x_tm, y, mod_l, g_ffn.reshape(1, d), w_out16, w_q16)


def _bitonic_sort_network(n):
    ops = []
    k = 2
    while k <= n:
        j = k // 2
        while j >= 1:
            for i in range(n):
                l = i ^ j
                if l > i:
                    ops.append((i, l, (i & k) == 0))
            j //= 2
        k *= 2
    return ops


def _bitonic_merge_network(n):
    ops = []
    j = n // 2
    while j >= 1:
        for i in range(n):
            l = i ^ j
            if l > i:
                ops.append((i, l, True))
        j //= 2
    return ops


def _apply_network(vals, ops):
    vals = list(vals)
    for i, j, first_max in ops:
        hi = jnp.maximum(vals[i], vals[j])
        lo = jnp.minimum(vals[i], vals[j])
        vals[i], vals[j] = (hi, lo) if first_max else (lo, hi)
    return vals


def _top_sorted(s, k):
    n_keys = s.shape[0]
    n_slabs = n_keys // SUBLANES
    assert n_slabs == k
    slabs = [s[i * SUBLANES:(i + 1) * SUBLANES, :] for i in range(n_slabs)]
    slabs = _apply_network(slabs, _bitonic_sort_network(k))
    merge = _bitonic_merge_network(k)
    shift = SUBLANES // 2
    while shift >= 1:
        rolled = [pltpu.roll(v, shift, 0) for v in slabs]
        slabs = [jnp.maximum(slabs[i], rolled[k - 1 - i]) for i in range(k)]
        slabs = _apply_network(slabs, merge)
        shift //= 2
    return slabs


def _staircase(k):
    return [(x, y) for x in range(k) for y in range(k) if (x + 1) * (y + 1) <= k]


def _select_tree(masks, leaves):
    if not masks:
        return leaves[0]
    half = len(leaves) // 2
    return jnp.where(masks[0], _select_tree(masks[1:], leaves[half:]), _select_tree(masks[1:], leaves[:half]))


def _prefix_count(pred, vals):
    n = len(vals)
    full = pred(vals[n - 1])
    masks = []
    count = None
    stride = n // 2
    while stride >= 1:
        pivot = _select_tree(masks, [vals[base + stride - 1] for base in range(0, n, 2 * stride)])
        hit = pred(pivot)
        masks.append(hit)
        term = jnp.where(hit, float(stride), 0.0)
        count = term if count is None else count + term
        stride //= 2
    return jnp.where(full, float(n), count)


def _retrieval_body(q_ref, keys_ref, r1_ref, e1_ref, n0_ref, c0_ref, s_scr, a_scr, b_scr,
                    afull_scr, bfull_scr):
    k = PEER_TOPK
    half = PEER_DQ // 2
    tok = q_ref.shape[0]
    neg = jnp.full((SUBLANES, tok), -jnp.inf, F32)

    for h in range(PEER_HEADS):
        for p in range(2):
            off = h * PEER_DQ + p * half
            s = _dot_nt(keys_ref[p], q_ref[:, off:off + half].astype(BF16))
            s_scr[h, p] = s
            top = _top_sorted(s, k)
            dst, dst_full = (a_scr, afull_scr) if p == 0 else (b_scr, bfull_scr)
            for x in range(k):
                dst[x, h:h + 1, :] = top[x][0:1, :]
                dst_full[h, x] = top[x]

    a = [a_scr[x] for x in range(k)]
    b = [b_scr[y] for y in range(k)]
    pairs = _staircase(k)
    cands = [a[x] + b[y] for x, y in pairs]
    n_sort = 1
    while n_sort < len(cands):
        n_sort *= 2
    ranked = _apply_network(cands + [neg] * (n_sort - len(cands)), _bitonic_sort_network(n_sort))
    tau = ranked[k - 1]
    best = a[0] + b[0]
    z = jnp.zeros_like(tau)
    for cnd in cands:
        z = z + jnp.where(cnd >= tau, jnp.exp(cnd - best), 0.0)
    inv_z = 1.0 / z

    pack = 2 * SUBLANES
    for h in range(PEER_HEADS):
        tau_h = jnp.broadcast_to(tau[h:h + 1, :], (SUBLANES, tok))
        inv_z_h = jnp.broadcast_to(inv_z[h:h + 1, :], (SUBLANES, tok))
        b_h = [bfull_scr[h, y] for y in range(k)]
        a_first = afull_scr[h, 0]
        a_last = afull_scr[h, k - 1]
        for blk in range(N_KEYS // pack):
            ranks, exps = [], []
            for sub in range(pack // SUBLANES):
                rows = slice(blk * pack + sub * SUBLANES, blk * pack + (sub + 1) * SUBLANES)
                s0 = s_scr[h, 0, rows, :]
                s1 = s_scr[h, 1, rows, :]
                ranks.append(_prefix_count(lambda v: v > s1, b_h))
                cnt = _prefix_count(lambda v: s0 + v >= tau_h, b_h)
                n0_ref[h, rows, :] = jnp.where(s0 >= a_last, cnt, 0.0)
                c0_ref[h, rows, :] = jnp.exp(s0 - a_first) * inv_z_h
                exps.append(jnp.exp(s1 - b_h[0]))
            prow = slice(blk * pack, (blk + 1) * pack)
            r1_ref[h, prow, :] = jnp.concatenate(ranks, axis=0).astype(r1_ref.dtype)
            e1_ref[h, prow, :] = jnp.concatenate(exps, axis=0).astype(e1_ref.dtype)


def _retrieval(q, keys16):
    rows = q.shape[0]
    tok = RETR_TILE
    tab = lambda dt: jax.ShapeDtypeStruct((PEER_HEADS, N_KEYS, rows), dt)
    tspec = pl.BlockSpec((PEER_HEADS, N_KEYS, tok), lambda i: (0, 0, i))
    return pl.pallas_call(
        _retrieval_body,
        grid=(rows // tok,),
        in_specs=[
            pl.BlockSpec((tok, q.shape[1]), lambda i: (i, 0)),
            pl.BlockSpec(keys16.shape, lambda i: (0, 0, 0)),
        ],
        out_specs=[tspec, tspec, tspec, tspec],
        out_shape=(tab(BF16), tab(BF16), tab(F32), tab(F32)),
        scratch_shapes=[
            pltpu.VMEM((PEER_HEADS, 2, N_KEYS, tok), F32),
            pltpu.VMEM((PEER_TOPK, SUBLANES, tok), F32),
            pltpu.VMEM((PEER_TOPK, SUBLANES, tok), F32),
            pltpu.VMEM((PEER_HEADS, PEER_TOPK, SUBLANES, tok), F32),
            pltpu.VMEM((PEER_HEADS, PEER_TOPK, SUBLANES, tok), F32),
        ],
        compiler_params=_params(("arbitrary",)),
    )(q, keys16)


def _peer_body(h2t_ref, r1_ref, e1_ref, n0_ref, c0_ref, u_ref, vt_ref, x1_ref, mod_ref,
               x2_ref, acc_scr, a_scr, p_scr):
    j = pl.program_id(1)
    n_j = pl.num_programs(1)
    te = u_ref.shape[0]
    tok = h2t_ref.shape[1]
    rows_per_step = te // N_KEYS
    pack = 2 * SUBLANES

    @pl.when(j == 0)
    def _():
        acc_scr[...] = jnp.zeros_like(acc_scr)
        a_scr[...] = _dot(u_ref[...], h2t_ref[...]).astype(BF16)

    @pl.when(j > 0)
    def _():
        zero = jnp.zeros((pack, PEER_PIECE), BF16)
        for c in range(tok // PEER_PIECE):
            cols = slice(c * PEER_PIECE, (c + 1) * PEER_PIECE)
            for r in range(rows_per_step):
                gates = [zero for _ in range(N_KEYS // pack)]
                for h in range(PEER_HEADS):
                    nb = jnp.broadcast_to(n0_ref[h, r:r + 1, cols], (pack, PEER_PIECE)).astype(BF16)
                    cb = jnp.broadcast_to(c0_ref[h, r:r + 1, cols], (pack, PEER_PIECE)).astype(BF16)
                    for k in range(N_KEYS // pack):
                        rs = slice(k * pack, (k + 1) * pack)
                        sel = jnp.where(r1_ref[h, rs, cols] < nb, e1_ref[h, rs, cols], zero)
                        gates[k] = gates[k] + sel * cb
                for k in range(N_KEYS // pack):
                    rs = slice(r * N_KEYS + k * pack, r * N_KEYS + (k + 1) * pack)
                    p_scr[rs, cols] = gates[k] * _gelu(a_scr[rs, cols])
            acc_scr[:, cols] += _dot(vt_ref[...], p_scr[:, cols])
            a_scr[:, cols] = _dot(u_ref[...], h2t_ref[:, cols]).astype(BF16)

    @pl.when(j == n_j - 1)
    def _():
        tl, bsz, d = x1_ref.shape
        out = acc_scr[...].T.reshape(tl, bsz, d)
        x2_ref[...] = x1_ref[...] + mod_ref[5] * out


def _peer_dense(h2t, tables, u16, vt16, x1, mod_l):
    seq, bsz, d = x1.shape
    tok = PEER_TOKEN_TILE
    tl = tok // bsz
    te = EXPERT_TILE
    n_tiles = u16.shape[0] // te
    rows_per_step = te // N_KEYS
    cur = lambda j: jnp.minimum(j, n_tiles - 1)
    prev = lambda j: jnp.maximum(j - 1, 0)
    tspec = pl.BlockSpec((PEER_HEADS, N_KEYS, tok), lambda i, j: (0, 0, i))
    rspec = pl.BlockSpec((PEER_HEADS, rows_per_step, tok), lambda i, j: (0, prev(j), i))
    return pl.pallas_call(
        _peer_body,
        grid=(seq // tl, n_tiles + 1),
        in_specs=[
            pl.BlockSpec((d, tok), lambda i, j: (0, i)),
            tspec, tspec, rspec, rspec,
            pl.BlockSpec((te, d), lambda i, j: (cur(j), 0)),
            pl.BlockSpec((d, te), lambda i, j: (0, prev(j))),
            pl.BlockSpec((tl, bsz, d), lambda i, j: (i, 0, 0)),
            pl.BlockSpec((N_MOD, bsz, d), lambda i, j: (0, 0, 0)),
        ],
        out_specs=pl.BlockSpec((tl, bsz, d), lambda i, j: (i, 0, 0)),
        out_shape=jax.ShapeDtypeStruct((seq, bsz, d), F32),
        scratch_shapes=[
            pltpu.VMEM((d, tok), F32),
            pltpu.VMEM((te, tok), BF16),
            pltpu.VMEM((te, tok), BF16),
        ],
        compiler_params=_params(("arbitrary", "arbitrary")),
    )(h2t, *tables, u16, vt16, x1, mod_l)


def _final_body(x_ref, g_ref, o_ref):
    x = x_ref[...]
    o_ref[...] = x * lax.rsqrt(jnp.mean(x * x, axis=-1, keepdims=True) + EPS) * g_ref[...]


def _final_norm(x_tm, g):
    seq, bsz, d = x_tm.shape
    tl = TOKEN_TILE // bsz
    return pl.pallas_call(
        _final_body,
        grid=(seq // tl,),
        in_specs=[pl.BlockSpec((tl, bsz, d), lambda i: (i, 0, 0)),
                  pl.BlockSpec((1, d), lambda i: (0, 0))],
        out_specs=pl.BlockSpec((tl, bsz, d), lambda i: (i, 0, 0)),
        out_shape=jax.ShapeDtypeStruct((seq, bsz, d), F32),
        compiler_params=_params(("arbitrary",)),
    )(x_tm, g.reshape(1, d))


def _block_diag(blocks):
    n, r, c = blocks.shape
    eye = jnp.eye(n, dtype=blocks.dtype)
    return (blocks[:, :, None, :] * eye[:, None, :, None]).reshape(n * r, n * c)


def kernel(x, c, w_mod, b_mod, g_mix, w_in, hgrn_lb_logits, s5_a_re, s5_a_im, s5_b_re, s5_b_im, s5_c_re, s5_c_im, s5_log_dt, s5_d, s5_w_glu, s5_b_glu, lru_conv_w, lru_conv_b, lru_w_a, lru_b_a, lru_w_x, lru_b_x, lru_lambda, g_branch, w_out, g_ffn, peer_w_q, peer_sub_keys, peer_u, peer_v, g_final):
    bsz, seq, d = x.shape
    depth = w_in.shape[0]
    assert bsz == SUBLANES and seq % CHUNK_T == 0 and (seq * bsz) % TOKEN_TILE == 0

    x_tm = x.transpose(1, 0, 2)
    mod = _modulation(c, w_mod, b_mod)
    mod = mod.reshape(depth, bsz, N_MOD, d).transpose(0, 2, 1, 3)
    lb_all, ab_re, ab_im, bb_re, bb_im = _prepare_params(
        hgrn_lb_logits, s5_a_re, s5_a_im, s5_log_dt, s5_b_re, s5_b_im)
    tmat = jnp.asarray(_hgrn_exponent_matrix(CHUNK_T), BF16)
    row = lambda a: a.reshape(1, -1)

    for l in range(depth):
        bbd = jnp.concatenate([
            _block_diag(bb_re[l].reshape(S5_GROUP, S5_GROUPS, S5_STATE).transpose(1, 0, 2)),
            _block_diag(bb_im[l].reshape(S5_GROUP, S5_GROUPS, S5_STATE).transpose(1, 0, 2)),
        ], axis=1).astype(BF16)
        cre = _block_diag(s5_c_re[l].transpose(0, 2, 1)).astype(BF16)
        cim = _block_diag(s5_c_im[l].transpose(0, 2, 1)).astype(BF16)
        s5p = (ab_re[l], ab_im[l], bbd, cre, cim, row(s5_d[l]),
               s5_w_glu[l].astype(BF16), row(s5_b_glu[l]))
        lrup = (lru_conv_w[l], row(lru_conv_b[l]),
                _block_diag(lru_w_a[l]).astype(BF16), row(lru_b_a[l]),
                _block_diag(lru_w_x[l]).astype(BF16), row(lru_b_x[l]), row(lru_lambda[l]))

        pa, pg, pb, pc = _in_projection(x_tm, mod[l], g_mix[l], w_in[l].astype(BF16))
        y = _mixer(pa, pg, pb, pc, bsz, tmat, lb_all[l:l + 1], s5p, lrup, row(g_branch[l]))
        x1, h2t, q = _out_projection(x_tm, y, mod[l], g_ffn[l],
                                    w_out[l].astype(BF16), peer_w_q[l].astype(BF16))
        tables = _retrieval(q, peer_sub_keys[l].astype(BF16))
        x_tm = _peer_dense(h2t, tables, peer_u[l].astype(BF16),
                           peer_v[l].astype(BF16).T, x1, mod[l])

    out = _final_norm(x_tm, g_final)
    return out.transpose(1, 0, 2)
```

```python
import functools

import numpy as np
import jax
import jax.numpy as jnp
from jax import lax
from jax.experimental import pallas as pl
from jax.experimental.pallas import tpu as pltpu

F32 = jnp.float32
BF16 = jnp.bfloat16

EPS = 1e-6
N_MOD = 6
HGRN_HEADS = 4
HGRN_DK = 128
D_HGRN = HGRN_HEADS * HGRN_DK
S5_GROUP = 16
S5_GROUPS = 16
S5_STATE = 64
D_S5 = S5_GROUP * S5_GROUPS
N_S5_STATE = S5_GROUPS * S5_STATE
LRU_BLOCKS = 4
LRU_BLOCK = 64
D_LRU = LRU_BLOCKS * LRU_BLOCK
CONV_W = 4
LRU_C = 8.0
PEER_HEADS = 8
N_KEYS = 128
PEER_DQ = 256
PEER_TOPK = 16

SUBLANES = 8
LANES = 128
VMEM_LIMIT_BYTES = 56 * 1024 * 1024

CHUNK_T = 64
HGRN_GROUP = 8
TOKEN_TILE = 512
ROW_PARTS = 2
EXPERT_TILE = 1024
PEER_TOKEN_TILE = 1024
PEER_PIECE = 256
RETR_TILE = 256
MOD_COL_TILE = 1536


def _sigmoid(x):
    return 1.0 / (1.0 + jnp.exp(-x))


def _silu(x):
    return x * _sigmoid(x)


def _gelu(x):
    return 0.5 * x * (1.0 + jnp.tanh(0.7978845608028654 * (x + 0.044715 * (x * x * x))))


def _dot(a, b):
    return jnp.dot(a, b, preferred_element_type=F32)


def _dot_nt(a, b):
    return lax.dot_general(a, b, (((1,), (1,)), ((), ())), preferred_element_type=F32)


def _dot_tn(a, b):
    return lax.dot_general(a, b, (((0,), (0,)), ((), ())), preferred_element_type=F32)


def _split2(x):
    hi = x.astype(BF16)
    lo = (x - hi.astype(F32)).astype(BF16)
    return hi, lo


def _params(semantics):
    return pltpu.CompilerParams(dimension_semantics=semantics, vmem_limit_bytes=VMEM_LIMIT_BYTES)


def _mod_body(c_ref, w_ref, b_ref, o_ref):
    cond = _silu(c_ref[...])
    c_hi, c_lo = _split2(cond)
    w_hi, w_lo = _split2(w_ref[0])
    o_ref[0] = _dot(c_hi, w_hi) + _dot(c_hi, w_lo) + _dot(c_lo, w_hi) + b_ref[0]


def _modulation(c, w_mod, b_mod):
    depth, d, n = w_mod.shape
    bsz = c.shape[0]
    tn = MOD_COL_TILE
    return pl.pallas_call(
        _mod_body,
        grid=(depth, n // tn),
        in_specs=[
            pl.BlockSpec((bsz, d), lambda l, j: (0, 0)),
            pl.BlockSpec((1, d, tn), lambda l, j: (l, 0, j)),
            pl.BlockSpec((1, 1, tn), lambda l, j: (l, 0, j)),
        ],
        out_specs=pl.BlockSpec((1, bsz, tn), lambda l, j: (l, 0, j)),
        out_shape=jax.ShapeDtypeStruct((depth, bsz, n), F32),
        compiler_params=_params(("arbitrary", "arbitrary")),
    )(c, w_mod, b_mod.reshape(depth, 1, n))


def _prep_body(lbl_ref, are_ref, aim_ref, ldt_ref, bre_ref, bim_ref,
               lb_ref, abr_ref, abi_ref, bbr_ref, bbi_ref):
    logits = lbl_ref[...]
    depth = logits.shape[0]
    rows = [logits[l:l + 1] for l in range(depth)]
    mx = functools.reduce(jnp.maximum, rows)
    ex = [jnp.exp(r - mx) for r in rows]
    den = functools.reduce(lambda a, b: a + b, ex)
    soft = [e / den for e in ex]
    run = soft[0]
    for l in range(depth):
        if l > 0:
            run = run + soft[l]
        lb_ref[l:l + 1, :] = run - soft[0]

    lam_re = jnp.minimum(are_ref[...], -1e-4)
    lam_im = aim_ref[...]
    dt = jnp.exp(ldt_ref[...])
    mag = jnp.exp(lam_re * dt)
    ab_re = mag * jnp.cos(lam_im * dt)
    ab_im = mag * jnp.sin(lam_im * dt)
    den2 = lam_re * lam_re + lam_im * lam_im
    nr = ab_re - 1.0
    z_re = (nr * lam_re + ab_im * lam_im) / den2
    z_im = (ab_im * lam_re - nr * lam_im) / den2
    abr_ref[...] = ab_re
    abi_ref[...] = ab_im
    br = bre_ref[...]
    bi = bim_ref[...]
    bbr_ref[...] = z_re * br - z_im * bi
    bbi_ref[...] = z_re * bi + z_im * br


def _prepare_params(lb_logits, a_re, a_im, log_dt, b_re, b_im):
    depth = lb_logits.shape[0]
    n = N_S5_STATE
    flat = lambda a: a.reshape(depth, 1, n)
    ldt = jnp.broadcast_to(log_dt[:, :, None], (depth, S5_GROUPS, S5_STATE))
    bt = lambda b: b.transpose(0, 3, 1, 2).reshape(depth, S5_GROUP, n)
    out_shape = (
        jax.ShapeDtypeStruct((depth, D_HGRN), F32),
        jax.ShapeDtypeStruct((depth, 1, n), F32),
        jax.ShapeDtypeStruct((depth, 1, n), F32),
        jax.ShapeDtypeStruct((depth, S5_GROUP, n), F32),
        jax.ShapeDtypeStruct((depth, S5_GROUP, n), F32),
    )
    return pl.pallas_call(_prep_body, out_shape=out_shape)(
        lb_logits, flat(a_re), flat(a_im), flat(ldt), bt(b_re), bt(b_im))


def _project_inputs(x_ref, mod_ref, g_ref, w_ref, pa_ref, pg_ref, pb_ref, pc_ref):
    tl, bsz, d = x_ref.shape
    na = pa_ref.shape[0] * LANES
    ng = pg_ref.shape[1]
    nb = pb_ref.shape[1]
    part = tl // ROW_PARTS
    for s in range(ROW_PARTS):
        x = x_ref[s * part:(s + 1) * part]
        ms = jnp.mean(x * x, axis=-1, keepdims=True)
        xn = x * lax.rsqrt(ms + EPS) * g_ref[...]
        h = xn * (1.0 + mod_ref[1]) + mod_ref[0]
        h = h.reshape(part * bsz, d).astype(BF16)
        rows = slice(s * part * bsz, (s + 1) * part * bsz)
        pa = _dot(h, w_ref[:, 0:na])
        for cg in range(pa_ref.shape[0]):
            pa_ref[cg, rows, :] = pa[:, cg * LANES:(cg + 1) * LANES]
        pg_ref[rows, :] = _dot(h, w_ref[:, na:na + ng])
        pb_ref[rows, :] = _dot(h, w_ref[:, na + ng:na + ng + nb])
        pc_ref[rows, :] = _dot(h, w_ref[:, na + ng + nb:])


def _hgrn_exponent_matrix(chunk):
    n_levels = int(np.log2(chunk))
    mat = np.zeros((n_levels + 2, chunk, chunk), np.float32)
    u = np.arange(chunk)
    for k in range(n_levels):
        g = chunk >> (k + 1)
        m = 2 * g
        for t in range(chunk):
            r = (t // m) * m + g - 1
            if t % m >= g:
                mat[k, t] = (u > r) & (u <= t)
            else:
                mat[k, t] = (u > t) & (u <= r)
    for t in range(chunk):
        mat[n_levels, t] = u <= t
        mat[n_levels + 1, t] = u > t
    return mat.reshape((n_levels + 2) * chunk, chunk)


def _mixer_body(x_ref, mod_ref, g_ref, w_ref, tmat_ref, lb_ref,
                ar_ref, ai_ref, bbd_ref, cre_ref, cim_ref, dsk_ref, wglu_ref, bglu_ref,
                cw_ref, cb_ref, wa_ref, ba_ref, wx_ref, bx_ref, lam_ref, gbr_ref,
                y_ref,
                pa_ref, pg_ref, pb_ref, pc_ref,
                st_ref, o_scr, bu_scr, xs_scr, s5_state, lru_state, xext, la_scr, lb_scr, hs_scr):
    _project_inputs(x_ref, mod_ref, g_ref, w_ref, pa_ref, pg_ref, pb_ref, pc_ref)
    rows = pa_ref.shape[1]
    bsz = s5_state.shape[0]
    chunk = rows // bsz
    n_levels = tmat_ref.shape[0] // chunk - 2
    dk = HGRN_DK

    @pl.when(pl.program_id(0) == 0)
    def _():
        st_ref[...] = jnp.zeros_like(st_ref)
        s5_state[...] = jnp.zeros_like(s5_state)
        lru_state[...] = jnp.zeros_like(lru_state)
        xext[0:(CONV_W - 1) * bsz, :] = jnp.zeros(((CONV_W - 1) * bsz, D_LRU), F32)

    ti = lax.broadcasted_iota(jnp.int32, (chunk, chunk), 0)
    si = lax.broadcasted_iota(jnp.int32, (chunk, chunk), 1)
    masks = []
    for k in range(n_levels):
        g = chunk >> (k + 1)
        m = 2 * g
        same = (ti // m) == (si // m)
        masks.append(jnp.where(same, jnp.where((ti % m) >= g, jnp.where((si % m) < g, 1.0, 0.0), 0.0), 0.0))
    diag = jnp.where(ti == si, 1.0, 0.0)
    lb = lb_ref[...]
    tmat = tmat_ref[...]

    def hgrn_group(g, carry):
        grp = range(HGRN_GROUP)
        bs = [g * HGRN_GROUP + e for e in grp]
        rws = [pl.ds(b, chunk, stride=bsz) for b in bs]
        heads = lambda e, first: jnp.concatenate(
            [pa_ref[first + h, rws[e], :] for h in range(HGRN_HEADS)], axis=1)
        qa = [_silu(heads(e, 0)) for e in grp]
        f = [lb + (1.0 - lb) * _sigmoid(heads(e, HGRN_HEADS)) for e in grp]
        v16 = [heads(e, 2 * HGRN_HEADS).astype(BF16) for e in grp]
        kk = [1.0 - f[e] for e in grp]
        parts = [_split2(jnp.log(f[e])) for e in grp]
        decay = [jnp.exp(_dot(tmat, parts[e][0]) + _dot(tmat, parts[e][1])) for e in grp]
        cum = [decay[e][n_levels * chunk:(n_levels + 1) * chunk] for e in grp]
        qb16 = [(qa[e] * cum[e]).astype(BF16) for e in grp]
        ks16 = [(kk[e] * decay[e][(n_levels + 1) * chunk:]).astype(BF16) for e in grp]
        total = [cum[e][chunk - 1:chunk, :] for e in grp]
        qa16 = [qa[e].astype(BF16) for e in grp]
        kk16 = [kk[e].astype(BF16) for e in grp]
        for h in range(HGRN_HEADS):
            sl = slice(h * dk, (h + 1) * dk)
            scores = [diag * _dot_nt(qa16[e][:, sl], kk16[e][:, sl]) for e in grp]
            for k in range(n_levels):
                for e in grp:
                    dl = decay[e][k * chunk:(k + 1) * chunk, sl]
                    ql = (qa[e][:, sl] * dl).astype(BF16)
                    kl = (kk[e][:, sl] * dl).astype(BF16)
                    scores[e] = scores[e] + masks[k] * _dot_nt(ql, kl)
            for e in grp:
                st = st_ref[bs[e], h]
                o = (_dot(scores[e].astype(BF16), v16[e][:, sl])
                     + _dot_nt(qb16[e][:, sl], st.astype(BF16)))
                o_scr[h, rws[e], :] = o
                st_ref[bs[e], h] = st * total[e][:, sl] + _dot_tn(v16[e][:, sl], ks16[e][:, sl])
        return carry

    lax.fori_loop(0, bsz // HGRN_GROUP, hgrn_group, 0)

    gate = _silu(pg_ref[...])
    ya = []
    ssq = None
    for h in range(HGRN_HEADS):
        sl = slice(h * dk, (h + 1) * dk)
        oh = o_scr[h]
        oh = oh * lax.rsqrt(jnp.mean(oh * oh, axis=-1, keepdims=True) + EPS) * gate[:, sl]
        ya.append(oh)
        s = jnp.sum(oh * oh, axis=-1, keepdims=True)
        ssq = s if ssq is None else ssq + s
    scale = lax.rsqrt(ssq * (1.0 / D_HGRN) + EPS)
    for h in range(HGRN_HEADS):
        sl = slice(h * dk, (h + 1) * dk)
        y_ref[:, sl] = (ya[h] * scale * gbr_ref[:, sl]).astype(y_ref.dtype)

    n = N_S5_STATE
    u = pb_ref[...]
    bu_scr[...] = _dot(u.astype(BF16), bbd_ref[...])
    ar = jnp.broadcast_to(ar_ref[...], (bsz, n))
    ai = jnp.broadcast_to(ai_ref[...], (bsz, n))

    tail = (CONV_W - 1) * bsz
    xin = pc_ref[:, 0:D_LRU]
    xext[tail:tail + rows, :] = xin
    xc = cb_ref[...]
    for w in range(CONV_W):
        xc = xc + cw_ref[w:w + 1, :] * xext[w * bsz:w * bsz + rows, :]
    xext[0:tail, :] = xin[rows - tail:rows, :]
    xc16 = xc.astype(BF16)
    rg = _sigmoid(_dot(xc16, wa_ref[...]) + ba_ref[...])
    ig = _sigmoid(_dot(xc16, wx_ref[...]) + bx_ref[...])
    nl = -lam_ref[...]
    softplus = jnp.maximum(nl, 0.0) + jnp.log(1.0 + jnp.exp(-jnp.abs(nl)))
    log_a = (-LRU_C) * rg * softplus
    la_scr[...] = jnp.exp(log_a)
    lb_scr[...] = jnp.sqrt(1.0 - jnp.exp(2.0 * log_a)) * (ig * xc)

    def scan_step(t, st):
        xr, xi, hstate = st
        r = pl.ds(pl.multiple_of(t * bsz, bsz), bsz)
        nxr = ar * xr - ai * xi + bu_scr[r, 0:n]
        nxi = ar * xi + ai * xr + bu_scr[r, n:2 * n]
        xs_scr[r, 0:n] = nxr
        xs_scr[r, n:2 * n] = nxi
        hn = la_scr[r, :] * hstate + lb_scr[r, :]
        hs_scr[r, :] = hn
        return nxr, nxi, hn

    xr, xi, hlast = lax.fori_loop(
        0, chunk, scan_step, (s5_state[:, 0:n], s5_state[:, n:2 * n], lru_state[...]))
    s5_state[:, 0:n] = xr
    s5_state[:, n:2 * n] = xi
    lru_state[...] = hlast

    yb = (_dot(xs_scr[:, 0:n].astype(BF16), cre_ref[...])
          - _dot(xs_scr[:, n:2 * n].astype(BF16), cim_ref[...]))
    yb = yb + dsk_ref[...] * u
    z = _gelu(yb)
    yb = z * _sigmoid(_dot(z.astype(BF16), wglu_ref[...]) + bglu_ref[...])
    yb = yb * lax.rsqrt(jnp.mean(yb * yb, axis=-1, keepdims=True) + EPS)
    y_ref[:, D_HGRN:D_HGRN + D_S5] = (yb * gbr_ref[:, D_HGRN:D_HGRN + D_S5]).astype(y_ref.dtype)

    yc = hs_scr[...] * _gelu(pc_ref[:, D_LRU:2 * D_LRU])
    yc = yc * lax.rsqrt(jnp.mean(yc * yc, axis=-1, keepdims=True) + EPS)
    y_ref[:, D_HGRN + D_S5:] = (yc * gbr_ref[:, D_HGRN + D_S5:]).astype(y_ref.dtype)


def _mixer(x_tm, mod_l, g_mix, w_in16, tmat, lb, s5p, lrup, g_branch):
    seq, bsz, d = x_tm.shape
    rows = CHUNK_T * bsz
    d_mix = D_HGRN + D_S5 + D_LRU
    n = N_S5_STATE
    full = lambda a: pl.BlockSpec(a.shape, lambda i, nd=a.ndim: (0,) * nd)
    consts = [mod_l, g_mix.reshape(1, d), w_in16, tmat, lb, *s5p, *lrup, g_branch]
    return pl.pallas_call(
        _mixer_body,
        grid=(seq // CHUNK_T,),
        in_specs=[pl.BlockSpec((CHUNK_T, bsz, d), lambda i: (i, 0, 0))] + [full(a) for a in consts],
        out_specs=pl.BlockSpec((rows, d_mix), lambda i: (i, 0)),
        out_shape=jax.ShapeDtypeStruct((seq * bsz, d_mix), BF16),
        scratch_shapes=[
            pltpu.VMEM((3 * D_HGRN // LANES, rows, LANES), F32),
            pltpu.VMEM((rows, D_HGRN), F32),
            pltpu.VMEM((rows, D_S5), F32),
            pltpu.VMEM((rows, 2 * D_LRU), F32),
            pltpu.VMEM((bsz, HGRN_HEADS, HGRN_DK, HGRN_DK), F32),
            pltpu.VMEM((HGRN_HEADS, rows, HGRN_DK), F32),
            pltpu.VMEM((rows, 2 * n), F32),
            pltpu.VMEM((rows, 2 * n), F32),
            pltpu.VMEM((bsz, 2 * n), F32),
            pltpu.VMEM((bsz, D_LRU), F32),
            pltpu.VMEM(((CONV_W - 1) * bsz + rows, D_LRU), F32),
            pltpu.VMEM((rows, D_LRU), F32),
            pltpu.VMEM((rows, D_LRU), F32),
            pltpu.VMEM((rows, D_LRU), F32),
        ],
        compiler_params=_params(("arbitrary",)),
    )(x_tm, *consts)


def _outproj_body(x_ref, y_ref, mod_ref, g_ref, wo_ref, wq_ref, x1_ref, h2t_ref, q_ref):
    tl, bsz, d = x_ref.shape
    part = tl // ROW_PARTS
    parts = range(ROW_PARTS)
    rows = [slice(s * part * bsz, (s + 1) * part * bsz) for s in parts]
    o = [_dot(y_ref[rows[s], :], wo_ref[...]).reshape(part, bsz, d) for s in parts]
    x1 = [x_ref[s * part:(s + 1) * part] + mod_ref[2] * o[s] for s in parts]
    h2 = []
    for s in parts:
        x1_ref[s * part:(s + 1) * part] = x1[s]
        ms = jnp.mean(x1[s] * x1[s], axis=-1, keepdims=True)
        hs = x1[s] * lax.rsqrt(ms + EPS) * g_ref[...]
        hs = hs * (1.0 + mod_ref[4]) + mod_ref[3]
        h2.append(hs.reshape(part * bsz, d))
    for s in parts:
        q_ref[rows[s], :] = _dot(h2[s].astype(BF16), wq_ref[...])
        h2t_ref[:, rows[s]] = h2[s].T.astype(BF16)


def _out_projection(x_tm, y, mod_l, g_ffn, w_out16, w_q16):
    seq, bsz, d = x_tm.shape
    tl = TOKEN_TILE // bsz
    tm = tl * bsz
    rows = seq * bsz
    nq = w_q16.shape[1]
    return pl.pallas_call(
        _outproj_body,
        grid=(seq // tl,),
        in_specs=[
            pl.BlockSpec((tl, bsz, d), lambda i: (i, 0, 0)),
            pl.BlockSpec((tm, y.shape[1]), lambda i: (i, 0)),
            pl.BlockSpec((N_MOD, bsz, d), lambda i: (0, 0, 0)),
            pl.BlockSpec((1, d), lambda i: (0, 0)),
            pl.BlockSpec(w_out16.shape, lambda i: (0, 0)),
            pl.BlockSpec(w_q16.shape, lambda i: (0, 0)),
        ],
        out_specs=[
            pl.BlockSpec((tl, bsz, d), lambda i: (i, 0, 0)),
            pl.BlockSpec((d, tm), lambda i: (0, i)),
            pl.BlockSpec((tm, nq), lambda i: (i, 0)),
        ],
        out_shape=(
            jax.ShapeDtypeStruct((seq, bsz, d), F32),
            jax.ShapeDtypeStruct((d, rows), BF16),
            jax.ShapeDtypeStruct((rows, nq), F32),
        ),
        compiler_params=_params(("arbitrary",)),
    )(x_tm, y, mod_l, g_ffn.reshape(1, d), w_out16, w_q16)


def _bitonic_sort_network(n):
    ops = []
    k = 2
    while k <= n:
        j = k // 2
        while j >= 1:
            for i in range(n):
                l = i ^ j
                if l > i:
                    ops.append((i, l, (i & k) == 0))
            j //= 2
        k *= 2
    return ops


def _bitonic_merge_network(n):
    ops = []
    j = n // 2
    while j >= 1:
        for i in range(n):
            l = i ^ j
            if l > i:
                ops.append((i, l, True))
        j //= 2
    return ops


def _apply_network(vals, ops):
    vals = list(vals)
    for i, j, first_max in ops:
        hi = jnp.maximum(vals[i], vals[j])
        lo = jnp.minimum(vals[i], vals[j])
        vals[i], vals[j] = (hi, lo) if first_max else (lo, hi)
    return vals


def _top_sorted(s, k):
    n_keys = s.shape[0]
    n_slabs = n_keys // SUBLANES
    assert n_slabs == k
    slabs = [s[i * SUBLANES:(i + 1) * SUBLANES, :] for i in range(n_slabs)]
    slabs = _apply_network(slabs, _bitonic_sort_network(k))
    merge = _bitonic_merge_network(k)
    shift = SUBLANES // 2
    while shift >= 1:
        rolled = [pltpu.roll(v, shift, 0) for v in slabs]
        slabs = [jnp.maximum(slabs[i], rolled[k - 1 - i]) for i in range(k)]
        slabs = _apply_network(slabs, merge)
        shift //= 2
    return slabs


def _staircase(k):
    return [(x, y) for x in range(k) for y in range(k) if (x + 1) * (y + 1) <= k]


def _select_tree(masks, leaves):
    if not masks:
        return leaves[0]
    half = len(leaves) // 2
    return jnp.where(masks[0], _select_tree(masks[1:], leaves[half:]), _select_tree(masks[1:], leaves[:half]))


def _prefix_count(pred, vals):
    n = len(vals)
    full = pred(vals[n - 1])
    masks = []
    count = None
    stride = n // 2
    while stride >= 1:
        pivot = _select_tree(masks, [vals[base + stride - 1] for base in range(0, n, 2 * stride)])
        hit = pred(pivot)
        masks.append(hit)
        term = jnp.where(hit, float(stride), 0.0)
        count = term if count is None else count + term
        stride //= 2
    return jnp.where(full, float(n), count)


def _retrieval_body(q_ref, keys_ref, r1_ref, e1_ref, n0_ref, c0_ref, s_scr, a_scr, b_scr,
                    afull_scr, bfull_scr):
    k = PEER_TOPK
    half = PEER_DQ // 2
    tok = q_ref.shape[0]
    neg = jnp.full((SUBLANES, tok), -jnp.inf, F32)

    for h in range(PEER_HEADS):
        for p in range(2):
            off = h * PEER_DQ + p * half
            s = _dot_nt(keys_ref[p], q_ref[:, off:off + half].astype(BF16))
            s_scr[h, p] = s
            top = _top_sorted(s, k)
            dst, dst_full = (a_scr, afull_scr) if p == 0 else (b_scr, bfull_scr)
            for x in range(k):
                dst[x, h:h + 1, :] = top[x][0:1, :]
                dst_full[h, x] = top[x]

    a = [a_scr[x] for x in range(k)]
    b = [b_scr[y] for y in range(k)]
    pairs = _staircase(k)
    cands = [a[x] + b[y] for x, y in pairs]
    n_sort = 1
    while n_sort < len(cands):
        n_sort *= 2
    ranked = _apply_network(cands + [neg] * (n_sort - len(cands)), _bitonic_sort_network(n_sort))
    tau = ranked[k - 1]
    best = a[0] + b[0]
    z = jnp.zeros_like(tau)
    for cnd in cands:
        z = z + jnp.where(cnd >= tau, jnp.exp(cnd - best), 0.0)
    inv_z = 1.0 / z

    pack = 2 * SUBLANES
    for h in range(PEER_HEADS):
        tau_h = jnp.broadcast_to(tau[h:h + 1, :], (SUBLANES, tok))
        inv_z_h = jnp.broadcast_to(inv_z[h:h + 1, :], (SUBLANES, tok))
        b_h = [bfull_scr[h, y] for y in range(k)]
        a_first = afull_scr[h, 0]
        a_last = afull_scr[h, k - 1]
        for blk in range(N_KEYS // pack):
            ranks, exps = [], []
            for sub in range(pack // SUBLANES):
                rows = slice(blk * pack + sub * SUBLANES, blk * pack + (sub + 1) * SUBLANES)
                s0 = s_scr[h, 0, rows, :]
                s1 = s_scr[h, 1, rows, :]
                ranks.append(_prefix_count(lambda v: v > s1, b_h))
                cnt = _prefix_count(lambda v: s0 + v >= tau_h, b_h)
                n0_ref[h, rows, :] = jnp.where(s0 >= a_last, cnt, 0.0)
                c0_ref[h, rows, :] = jnp.exp(s0 - a_first) * inv_z_h
                exps.append(jnp.exp(s1 - b_h[0]))
            prow = slice(blk * pack, (blk + 1) * pack)
            r1_ref[h, prow, :] = jnp.concatenate(ranks, axis=0).astype(r1_ref.dtype)
            e1_ref[h, prow, :] = jnp.concatenate(exps, axis=0).astype(e1_ref.dtype)


def _retrieval(q, keys16):
    rows = q.shape[0]
    tok = RETR_TILE
    tab = lambda dt: jax.ShapeDtypeStruct((PEER_HEADS, N_KEYS, rows), dt)
    tspec = pl.BlockSpec((PEER_HEADS, N_KEYS, tok), lambda i: (0, 0, i))
    return pl.pallas_call(
        _retrieval_body,
        grid=(rows // tok,),
        in_specs=[
            pl.BlockSpec((tok, q.shape[1]), lambda i: (i, 0)),
            pl.BlockSpec(keys16.shape, lambda i: (0, 0, 0)),
        ],
        out_specs=[tspec, tspec, tspec, tspec],
        out_shape=(tab(BF16), tab(BF16), tab(F32), tab(F32)),
        scratch_shapes=[
            pltpu.VMEM((PEER_HEADS, 2, N_KEYS, tok), F32),
            pltpu.VMEM((PEER_TOPK, SUBLANES, tok), F32),
            pltpu.VMEM((PEER_TOPK, SUBLANES, tok), F32),
            pltpu.VMEM((PEER_HEADS, PEER_TOPK, SUBLANES, tok), F32),
            pltpu.VMEM((PEER_HEADS, PEER_TOPK, SUBLANES, tok), F32),
        ],
        compiler_params=_params(("arbitrary",)),
    )(q, keys16)


def _peer_body(h2t_ref, r1_ref, e1_ref, n0_ref, c0_ref, u_ref, vt_ref, x1_ref, mod_ref,
               x2_ref, acc_scr, a_scr, p_scr):
    j = pl.program_id(1)
    n_j = pl.num_programs(1)
    te = u_ref.shape[0]
    tok = h2t_ref.shape[1]
    rows_per_step = te // N_KEYS
    pack = 2 * SUBLANES

    @pl.when(j == 0)
    def _():
        acc_scr[...] = jnp.zeros_like(acc_scr)
        a_scr[...] = _dot(u_ref[...], h2t_ref[...]).astype(BF16)

    @pl.when(j > 0)
    def _():
        zero = jnp.zeros((pack, PEER_PIECE), BF16)
        for c in range(tok // PEER_PIECE):
            cols = slice(c * PEER_PIECE, (c + 1) * PEER_PIECE)
            for r in range(rows_per_step):
                gates = [zero for _ in range(N_KEYS // pack)]
                for h in range(PEER_HEADS):
                    nb = jnp.broadcast_to(n0_ref[h, r:r + 1, cols], (pack, PEER_PIECE)).astype(BF16)
                    cb = jnp.broadcast_to(c0_ref[h, r:r + 1, cols], (pack, PEER_PIECE)).astype(BF16)
                    for k in range(N_KEYS // pack):
                        rs = slice(k * pack, (k + 1) * pack)
                        sel = jnp.where(r1_ref[h, rs, cols] < nb, e1_ref[h, rs, cols], zero)
                        gates[k] = gates[k] + sel * cb
                for k in range(N_KEYS // pack):
                    rs = slice(r * N_KEYS + k * pack, r * N_KEYS + (k + 1) * pack)
                    p_scr[rs, cols] = gates[k] * _gelu(a_scr[rs, cols])
            acc_scr[:, cols] += _dot(vt_ref[...], p_scr[:, cols])
            a_scr[:, cols] = _dot(u_ref[...], h2t_ref[:, cols]).astype(BF16)

    @pl.when(j == n_j - 1)
    def _():
        tl, bsz, d = x1_ref.shape
        out = acc_scr[...].T.reshape(tl, bsz, d)
        x2_ref[...] = x1_ref[...] + mod_ref[5] * out


def _peer_dense(h2t, tables, u16, vt16, x1, mod_l):
    seq, bsz, d = x1.shape
    tok = PEER_TOKEN_TILE
    tl = tok // bsz
    te = EXPERT_TILE
    n_tiles = u16.shape[0] // te
    rows_per_step = te // N_KEYS
    cur = lambda j: jnp.minimum(j, n_tiles - 1)
    prev = lambda j: jnp.maximum(j - 1, 0)
    tspec = pl.BlockSpec((PEER_HEADS, N_KEYS, tok), lambda i, j: (0, 0, i))
    rspec = pl.BlockSpec((PEER_HEADS, rows_per_step, tok), lambda i, j: (0, prev(j), i))
    return pl.pallas_call(
        _peer_body,
        grid=(seq // tl, n_tiles + 1),
        in_specs=[
            pl.BlockSpec((d, tok), lambda i, j: (0, i)),
            tspec, tspec, rspec, rspec,
            pl.BlockSpec((te, d), lambda i, j: (cur(j), 0)),
            pl.BlockSpec((d, te), lambda i, j: (0, prev(j))),
            pl.BlockSpec((tl, bsz, d), lambda i, j: (i, 0, 0)),
            pl.BlockSpec((N_MOD, bsz, d), lambda i, j: (0, 0, 0)),
        ],
        out_specs=pl.BlockSpec((tl, bsz, d), lambda i, j: (i, 0, 0)),
        out_shape=jax.ShapeDtypeStruct((seq, bsz, d), F32),
        scratch_shapes=[
            pltpu.VMEM((d, tok), F32),
            pltpu.VMEM((te, tok), BF16),
            pltpu.VMEM((te, tok), BF16),
        ],
        compiler_params=_params(("arbitrary", "arbitrary")),
    )(h2t, *tables, u16, vt16, x1, mod_l)


def _final_body(x_ref, g_ref, o_ref):
    x = x_ref[...]
    o_ref[...] = x * lax.rsqrt(jnp.mean(x * x, axis=-1, keepdims=True) + EPS) * g_ref[...]


def _final_norm(x_tm, g):
    seq, bsz, d = x_tm.shape
    tl = TOKEN_TILE // bsz
    return pl.pallas_call(
        _final_body,
        grid=(seq // tl,),
        in_specs=[pl.BlockSpec((tl, bsz, d), lambda i: (i, 0, 0)),
                  pl.BlockSpec((1, d), lambda i: (0, 0))],
        out_specs=pl.BlockSpec((tl, bsz, d), lambda i: (i, 0, 0)),
        out_shape=jax.ShapeDtypeStruct((seq, bsz, d), F32),
        compiler_params=_params(("arbitrary",)),
    )(x_tm, g.reshape(1, d))


def _block_diag(blocks):
    n, r, c = blocks.shape
    eye = jnp.eye(n, dtype=blocks.dtype)
    return (blocks[:, :, None, :] * eye[:, None, :, None]).reshape(n * r, n * c)


def kernel(x, c, w_mod, b_mod, g_mix, w_in, hgrn_lb_logits, s5_a_re, s5_a_im, s5_b_re, s5_b_im, s5_c_re, s5_c_im, s5_log_dt, s5_d, s5_w_glu, s5_b_glu, lru_conv_w, lru_conv_b, lru_w_a, lru_b_a, lru_w_x, lru_b_x, lru_lambda, g_branch, w_out, g_ffn, peer_w_q, peer_sub_keys, peer_u, peer_v, g_final):
    bsz, seq, d = x.shape
    depth = w_in.shape[0]
    assert bsz == SUBLANES and seq % CHUNK_T == 0 and (seq * bsz) % TOKEN_TILE == 0

    x_tm = x.transpose(1, 0, 2)
    mod = _modulation(c, w_mod, b_mod)
    mod = mod.reshape(depth, bsz, N_MOD, d).transpose(0, 2, 1, 3)
    lb_all, ab_re, ab_im, bb_re, bb_im = _prepare_params(
        hgrn_lb_logits, s5_a_re, s5_a_im, s5_log_dt, s5_b_re, s5_b_im)
    tmat = jnp.asarray(_hgrn_exponent_matrix(CHUNK_T), BF16)
    row = lambda a: a.reshape(1, -1)

    for l in range(depth):
        bbd = jnp.concatenate([
            _block_diag(bb_re[l].reshape(S5_GROUP, S5_GROUPS, S5_STATE).transpose(1, 0, 2)),
            _block_diag(bb_im[l].reshape(S5_GROUP, S5_GROUPS, S5_STATE).transpose(1, 0, 2)),
        ], axis=1).astype(BF16)
        cre = _block_diag(s5_c_re[l].transpose(0, 2, 1)).astype(BF16)
        cim = _block_diag(s5_c_im[l].transpose(0, 2, 1)).astype(BF16)
        s5p = (ab_re[l], ab_im[l], bbd, cre, cim, row(s5_d[l]),
               s5_w_glu[l].astype(BF16), row(s5_b_glu[l]))
        lrup = (lru_conv_w[l], row(lru_conv_b[l]),
                _block_diag(lru_w_a[l]).astype(BF16), row(lru_b_a[l]),
                _block_diag(lru_w_x[l]).astype(BF16), row(lru_b_x[l]), row(lru_lambda[l]))

        y = _mixer(x_tm, mod[l], g_mix[l], w_in[l].astype(BF16), tmat, lb_all[l:l + 1],
                   s5p, lrup, row(g_branch[l]))
        x1, h2t, q = _out_projection(x_tm, y, mod[l], g_ffn[l],
                                    w_out[l].astype(BF16), peer_w_q[l].astype(BF16))
        tables = _retrieval(q, peer_sub_keys[l].astype(BF16))
        x_tm = _peer_dense(h2t, tables, peer_u[l].astype(BF16),
                           peer_v[l].astype(BF16).T, x1, mod[l])

    out = _final_norm(x_tm, g_final)
    return out.transpose(1, 0, 2)
```

```python
import functools

import numpy as np
import jax
import jax.numpy as jnp
from jax import lax
from jax.experimental import pallas as pl
from jax.experimental.pallas import tpu as pltpu

F32 = jnp.float32
BF16 = jnp.bfloat16

EPS = 1e-6
N_MOD = 6
HGRN_HEADS = 4
HGRN_DK = 128
D_HGRN = HGRN_HEADS * HGRN_DK
S5_GROUP = 16
S5_GROUPS = 16
S5_STATE = 64
D_S5 = S5_GROUP * S5_GROUPS
N_S5_STATE = S5_GROUPS * S5_STATE
LRU_BLOCKS = 4
LRU_BLOCK = 64
D_LRU = LRU_BLOCKS * LRU_BLOCK
CONV_W = 4
LRU_C = 8.0
PEER_HEADS = 8
N_KEYS = 128
PEER_DQ = 256
PEER_TOPK = 16

SUBLANES = 8
LANES = 128
VMEM_LIMIT_BYTES = 56 * 1024 * 1024

CHUNK_T = 64
HGRN_GROUP = 8
TOKEN_TILE = 512
ROW_PARTS = 2
EXPERT_TILE = 1024
PEER_TOKEN_TILE = 1024
PEER_PIECE = 256
RETR_TILE = 256
MOD_COL_TILE = 1536


def _sigmoid(x):
    return 1.0 / (1.0 + jnp.exp(-x))


def _silu(x):
    return x * _sigmoid(x)


def _gelu(x):
    return 0.5 * x * (1.0 + jnp.tanh(0.7978845608028654 * (x + 0.044715 * (x * x * x))))


def _dot(a, b):
    return jnp.dot(a, b, preferred_element_type=F32)


def _dot_nt(a, b):
    return lax.dot_general(a, b, (((1,), (1,)), ((), ())), preferred_element_type=F32)


def _dot_tn(a, b):
    return lax.dot_general(a, b, (((0,), (0,)), ((), ())), preferred_element_type=F32)


def _split2(x):
    hi = x.astype(BF16)
    lo = (x - hi.astype(F32)).astype(BF16)
    return hi, lo


def _params(semantics):
    return pltpu.CompilerParams(dimension_semantics=semantics, vmem_limit_bytes=VMEM_LIMIT_BYTES)


def _mod_body(c_ref, w_ref, b_ref, o_ref):
    cond = _silu(c_ref[...])
    c_hi, c_lo = _split2(cond)
    w_hi, w_lo = _split2(w_ref[0])
    o_ref[0] = _dot(c_hi, w_hi) + _dot(c_hi, w_lo) + _dot(c_lo, w_hi) + b_ref[0]


def _modulation(c, w_mod, b_mod):
    depth, d, n = w_mod.shape
    bsz = c.shape[0]
    tn = MOD_COL_TILE
    return pl.pallas_call(
        _mod_body,
        grid=(depth, n // tn),
        in_specs=[
            pl.BlockSpec((bsz, d), lambda l, j: (0, 0)),
            pl.BlockSpec((1, d, tn), lambda l, j: (l, 0, j)),
            pl.BlockSpec((1, 1, tn), lambda l, j: (l, 0, j)),
        ],
        out_specs=pl.BlockSpec((1, bsz, tn), lambda l, j: (l, 0, j)),
        out_shape=jax.ShapeDtypeStruct((depth, bsz, n), F32),
        compiler_params=_params(("arbitrary", "arbitrary")),
    )(c, w_mod, b_mod.reshape(depth, 1, n))


def _prep_body(lbl_ref, are_ref, aim_ref, ldt_ref, bre_ref, bim_ref,
               lb_ref, abr_ref, abi_ref, bbr_ref, bbi_ref):
    logits = lbl_ref[...]
    depth = logits.shape[0]
    rows = [logits[l:l + 1] for l in range(depth)]
    mx = functools.reduce(jnp.maximum, rows)
    ex = [jnp.exp(r - mx) for r in rows]
    den = functools.reduce(lambda a, b: a + b, ex)
    soft = [e / den for e in ex]
    run = soft[0]
    for l in range(depth):
        if l > 0:
            run = run + soft[l]
        lb_ref[l:l + 1, :] = run - soft[0]

    lam_re = jnp.minimum(are_ref[...], -1e-4)
    lam_im = aim_ref[...]
    dt = jnp.exp(ldt_ref[...])
    mag = jnp.exp(lam_re * dt)
    ab_re = mag * jnp.cos(lam_im * dt)
    ab_im = mag * jnp.sin(lam_im * dt)
    den2 = lam_re * lam_re + lam_im * lam_im
    nr = ab_re - 1.0
    z_re = (nr * lam_re + ab_im * lam_im) / den2
    z_im = (ab_im * lam_re - nr * lam_im) / den2
    abr_ref[...] = ab_re
    abi_ref[...] = ab_im
    br = bre_ref[...]
    bi = bim_ref[...]
    bbr_ref[...] = z_re * br - z_im * bi
    bbi_ref[...] = z_re * bi + z_im * br


def _prepare_params(lb_logits, a_re, a_im, log_dt, b_re, b_im):
    depth = lb_logits.shape[0]
    n = N_S5_STATE
    flat = lambda a: a.reshape(depth, 1, n)
    ldt = jnp.broadcast_to(log_dt[:, :, None], (depth, S5_GROUPS, S5_STATE))
    bt = lambda b: b.transpose(0, 3, 1, 2).reshape(depth, S5_GROUP, n)
    out_shape = (
        jax.ShapeDtypeStruct((depth, D_HGRN), F32),
        jax.ShapeDtypeStruct((depth, 1, n), F32),
        jax.ShapeDtypeStruct((depth, 1, n), F32),
        jax.ShapeDtypeStruct((depth, S5_GROUP, n), F32),
        jax.ShapeDtypeStruct((depth, S5_GROUP, n), F32),
    )
    return pl.pallas_call(_prep_body, out_shape=out_shape)(
        lb_logits, flat(a_re), flat(a_im), flat(ldt), bt(b_re), bt(b_im))


def _project_inputs(x_ref, mod_ref, g_ref, w_ref, pa_ref, pg_ref, pb_ref, pc_ref):
    tl, bsz, d = x_ref.shape
    na = pa_ref.shape[0] * LANES
    ng = pg_ref.shape[1]
    nb = pb_ref.shape[1]
    part = tl // ROW_PARTS
    for s in range(ROW_PARTS):
        x = x_ref[s * part:(s + 1) * part]
        ms = jnp.mean(x * x, axis=-1, keepdims=True)
        xn = x * lax.rsqrt(ms + EPS) * g_ref[...]
        h = xn * (1.0 + mod_ref[1]) + mod_ref[0]
        h = h.reshape(part * bsz, d).astype(BF16)
        rows = slice(s * part * bsz, (s + 1) * part * bsz)
        pa = _dot(h, w_ref[:, 0:na])
        for cg in range(pa_ref.shape[0]):
            pa_ref[cg, rows, :] = pa[:, cg * LANES:(cg + 1) * LANES]
        pg_ref[rows, :] = _dot(h, w_ref[:, na:na + ng])
        pb_ref[rows, :] = _dot(h, w_ref[:, na + ng:na + ng + nb])
        pc_ref[rows, :] = _dot(h, w_ref[:, na + ng + nb:])


def _hgrn_exponent_matrix(chunk):
    n_levels = int(np.log2(chunk))
    mat = np.zeros((n_levels + 2, chunk, chunk), np.float32)
    u = np.arange(chunk)
    for k in range(n_levels):
        g = chunk >> (k + 1)
        m = 2 * g
        for t in range(chunk):
            r = (t // m) * m + g - 1
            if t % m >= g:
                mat[k, t] = (u > r) & (u <= t)
            else:
                mat[k, t] = (u > t) & (u <= r)
    for t in range(chunk):
        mat[n_levels, t] = u <= t
        mat[n_levels + 1, t] = u > t
    return mat.reshape((n_levels + 2) * chunk, chunk)


def _mixer_body(x_ref, mod_ref, g_ref, w_ref, tmat_ref, lb_ref,
                ar_ref, ai_ref, bbd_ref, cre_ref, cim_ref, dsk_ref, wglu_ref, bglu_ref,
                cw_ref, cb_ref, wa_ref, ba_ref, wx_ref, bx_ref, lam_ref, gbr_ref,
                y_ref,
                pa_ref, pg_ref, pb_ref, pc_ref,
                st_ref, o_scr, bu_scr, xs_scr, s5_state, lru_state, xext, la_scr, lb_scr, hs_scr):
    _project_inputs(x_ref, mod_ref, g_ref, w_ref, pa_ref, pg_ref, pb_ref, pc_ref)
    rows = pa_ref.shape[1]
    bsz = s5_state.shape[0]
    chunk = rows // bsz
    n_levels = tmat_ref.shape[0] // chunk - 2
    dk = HGRN_DK

    @pl.when(pl.program_id(0) == 0)
    def _():
        st_ref[...] = jnp.zeros_like(st_ref)
        s5_state[...] = jnp.zeros_like(s5_state)
        lru_state[...] = jnp.zeros_like(lru_state)
        xext[0:(CONV_W - 1) * bsz, :] = jnp.zeros(((CONV_W - 1) * bsz, D_LRU), F32)

    ti = lax.broadcasted_iota(jnp.int32, (chunk, chunk), 0)
    si = lax.broadcasted_iota(jnp.int32, (chunk, chunk), 1)
    masks = []
    for k in range(n_levels):
        g = chunk >> (k + 1)
        m = 2 * g
        same = (ti // m) == (si // m)
        masks.append(jnp.where(same, jnp.where((ti % m) >= g, jnp.where((si % m) < g, 1.0, 0.0), 0.0), 0.0))
    diag = jnp.where(ti == si, 1.0, 0.0)
    lb = lb_ref[...]
    tmat = tmat_ref[...]

    def hgrn_group(g, carry):
        grp = range(HGRN_GROUP)
        bs = [g * HGRN_GROUP + e for e in grp]
        rws = [pl.ds(b, chunk, stride=bsz) for b in bs]
        heads = lambda e, first: jnp.concatenate(
            [pa_ref[first + h, rws[e], :] for h in range(HGRN_HEADS)], axis=1)
        qa = [_silu(heads(e, 0)) for e in grp]
        f = [lb + (1.0 - lb) * _sigmoid(heads(e, HGRN_HEADS)) for e in grp]
        v16 = [heads(e, 2 * HGRN_HEADS).astype(BF16) for e in grp]
        kk = [1.0 - f[e] for e in grp]
        parts = [_split2(jnp.log(f[e])) for e in grp]
        decay = [jnp.exp(_dot(tmat, parts[e][0]) + _dot(tmat, parts[e][1])) for e in grp]
        cum = [decay[e][n_levels * chunk:(n_levels + 1) * chunk] for e in grp]
        qb16 = [(qa[e] * cum[e]).astype(BF16) for e in grp]
        ks16 = [(kk[e] * decay[e][(n_levels + 1) * chunk:]).astype(BF16) for e in grp]
        total = [cum[e][chunk - 1:chunk, :] for e in grp]
        qa16 = [qa[e].astype(BF16) for e in grp]
        kk16 = [kk[e].astype(BF16) for e in grp]
        for h in range(HGRN_HEADS):
            sl = slice(h * dk, (h + 1) * dk)
            scores = [diag * _dot_nt(qa16[e][:, sl], kk16[e][:, sl]) for e in grp]
            for k in range(n_levels):
                for e in grp:
                    dl = decay[e][k * chunk:(k + 1) * chunk, sl]
                    ql = (qa[e][:, sl] * dl).astype(BF16)
                    kl = (kk[e][:, sl] * dl).astype(BF16)
                    scores[e] = scores[e] + masks[k] * _dot_nt(ql, kl)
            for e in grp:
                st = st_ref[bs[e], h]
                o = (_dot(scores[e].astype(BF16), v16[e][:, sl])
                     + _dot_nt(qb16[e][:, sl], st.astype(BF16)))
                o_scr[h, rws[e], :] = o
                st_ref[bs[e], h] = st * total[e][:, sl] + _dot_tn(v16[e][:, sl], ks16[e][:, sl])
        return carry

    lax.fori_loop(0, bsz // HGRN_GROUP, hgrn_group, 0)

    gate = _silu(pg_ref[...])
    ya = []
    ssq = None
    for h in range(HGRN_HEADS):
        sl = slice(h * dk, (h + 1) * dk)
        oh = o_scr[h]
        oh = oh * lax.rsqrt(jnp.mean(oh * oh, axis=-1, keepdims=True) + EPS) * gate[:, sl]
        ya.append(oh)
        s = jnp.sum(oh * oh, axis=-1, keepdims=True)
        ssq = s if ssq is None else ssq + s
    scale = lax.rsqrt(ssq * (1.0 / D_HGRN) + EPS)
    for h in range(HGRN_HEADS):
        sl = slice(h * dk, (h + 1) * dk)
        y_ref[:, sl] = (ya[h] * scale * gbr_ref[:, sl]).astype(y_ref.dtype)

    n = N_S5_STATE
    u = pb_ref[...]
    bu_scr[...] = _dot(u.astype(BF16), bbd_ref[...])
    ar = jnp.broadcast_to(ar_ref[...], (bsz, n))
    ai = jnp.broadcast_to(ai_ref[...], (bsz, n))

    tail = (CONV_W - 1) * bsz
    xin = pc_ref[:, 0:D_LRU]
    xext[tail:tail + rows, :] = xin
    xc = cb_ref[...]
    for w in range(CONV_W):
        xc = xc + cw_ref[w:w + 1, :] * xext[w * bsz:w * bsz + rows, :]
    xext[0:tail, :] = xin[rows - tail:rows, :]
    xc16 = xc.astype(BF16)
    rg = _sigmoid(_dot(xc16, wa_ref[...]) + ba_ref[...])
    ig = _sigmoid(_dot(xc16, wx_ref[...]) + bx_ref[...])
    nl = -lam_ref[...]
    softplus = jnp.maximum(nl, 0.0) + jnp.log(1.0 + jnp.exp(-jnp.abs(nl)))
    log_a = (-LRU_C) * rg * softplus
    la_scr[...] = jnp.exp(log_a)
    lb_scr[...] = jnp.sqrt(1.0 - jnp.exp(2.0 * log_a)) * (ig * xc)

    def scan_step(t, st):
        xr, xi, hstate = st
        r = pl.ds(pl.multiple_of(t * bsz, bsz), bsz)
        nxr = ar * xr - ai * xi + bu_scr[r, 0:n]
        nxi = ar * xi + ai * xr + bu_scr[r, n:2 * n]
        xs_scr[r, 0:n] = nxr
        xs_scr[r, n:2 * n] = nxi
        hn = la_scr[r, :] * hstate + lb_scr[r, :]
        hs_scr[r, :] = hn
        return nxr, nxi, hn

    xr, xi, hlast = lax.fori_loop(
        0, chunk, scan_step, (s5_state[:, 0:n], s5_state[:, n:2 * n], lru_state[...]))
    s5_state[:, 0:n] = xr
    s5_state[:, n:2 * n] = xi
    lru_state[...] = hlast

    yb = (_dot(xs_scr[:, 0:n].astype(BF16), cre_ref[...])
          - _dot(xs_scr[:, n:2 * n].astype(BF16), cim_ref[...]))
    yb = yb + dsk_ref[...] * u
    z = _gelu(yb)
    yb = z * _sigmoid(_dot(z.astype(BF16), wglu_ref[...]) + bglu_ref[...])
    yb = yb * lax.rsqrt(jnp.mean(yb * yb, axis=-1, keepdims=True) + EPS)
    y_ref[:, D_HGRN:D_HGRN + D_S5] = (yb * gbr_ref[:, D_HGRN:D_HGRN + D_S5]).astype(y_ref.dtype)

    yc = hs_scr[...] * _gelu(pc_ref[:, D_LRU:2 * D_LRU])
    yc = yc * lax.rsqrt(jnp.mean(yc * yc, axis=-1, keepdims=True) + EPS)
    y_ref[:, D_HGRN + D_S5:] = (yc * gbr_ref[:, D_HGRN + D_S5:]).astype(y_ref.dtype)


def _mixer(x_tm, mod_l, g_mix, w_in16, tmat, lb, s5p, lrup, g_branch):
    seq, bsz, d = x_tm.shape
    rows = CHUNK_T * bsz
    d_mix = D_HGRN + D_S5 + D_LRU
    n = N_S5_STATE
    full = lambda a: pl.BlockSpec(a.shape, lambda i, nd=a.ndim: (0,) * nd)
    consts = [mod_l, g_mix.reshape(1, d), w_in16, tmat, lb, *s5p, *lrup, g_branch]
    return pl.pallas_call(
        _mixer_body,
        grid=(seq // CHUNK_T,),
        in_specs=[pl.BlockSpec((CHUNK_T, bsz, d), lambda i: (i, 0, 0))] + [full(a) for a in consts],
        out_specs=pl.BlockSpec((rows, d_mix), lambda i: (i, 0)),
        out_shape=jax.ShapeDtypeStruct((seq * bsz, d_mix), BF16),
        scratch_shapes=[
            pltpu.VMEM((3 * D_HGRN // LANES, rows, LANES), F32),
            pltpu.VMEM((rows, D_HGRN), F32),
            pltpu.VMEM((rows, D_S5), F32),
            pltpu.VMEM((rows, 2 * D_LRU), F32),
            pltpu.VMEM((bsz, HGRN_HEADS, HGRN_DK, HGRN_DK), F32),
            pltpu.VMEM((HGRN_HEADS, rows, HGRN_DK), F32),
            pltpu.VMEM((rows, 2 * n), F32),
            pltpu.VMEM((rows, 2 * n), F32),
            pltpu.VMEM((bsz, 2 * n), F32),
            pltpu.VMEM((bsz, D_LRU), F32),
            pltpu.VMEM(((CONV_W - 1) * bsz + rows, D_LRU), F32),
            pltpu.VMEM((rows, D_LRU), F32),
            pltpu.VMEM((rows, D_LRU), F32),
            pltpu.VMEM((rows, D_LRU), F32),
        ],
        compiler_params=_params(("arbitrary",)),
    )(x_tm, *consts)


def _outproj_body(x_ref, y_ref, mod_ref, g_ref, wo_ref, wq_ref, x1_ref, h2t_ref, q_ref):
    tl, bsz, d = x_ref.shape
    part = tl // ROW_PARTS
    parts = range(ROW_PARTS)
    rows = [slice(s * part * bsz, (s + 1) * part * bsz) for s in parts]
    o = [_dot(y_ref[rows[s], :], wo_ref[...]).reshape(part, bsz, d) for s in parts]
    x1 = [x_ref[s * part:(s + 1) * part] + mod_ref[2] * o[s] for s in parts]
    h2 = []
    for s in parts:
        x1_ref[s * part:(s + 1) * part] = x1[s]
        ms = jnp.mean(x1[s] * x1[s], axis=-1, keepdims=True)
        hs = x1[s] * lax.rsqrt(ms + EPS) * g_ref[...]
        hs = hs * (1.0 + mod_ref[4]) + mod_ref[3]
        h2.append(hs.reshape(part * bsz, d))
    for s in parts:
        q_ref[rows[s], :] = _dot(h2[s].astype(BF16), wq_ref[...]).astype(q_ref.dtype)
        h2t_ref[:, rows[s]] = h2[s].T.astype(BF16)


def _out_projection(x_tm, y, mod_l, g_ffn, w_out16, w_q16):
    seq, bsz, d = x_tm.shape
    tl = TOKEN_TILE // bsz
    tm = tl * bsz
    rows = seq * bsz
    nq = w_q16.shape[1]
    return pl.pallas_call(
        _outproj_body,
        grid=(seq // tl,),
        in_specs=[
            pl.BlockSpec((tl, bsz, d), lambda i: (i, 0, 0)),
            pl.BlockSpec((tm, y.shape[1]), lambda i: (i, 0)),
            pl.BlockSpec((N_MOD, bsz, d), lambda i: (0, 0, 0)),
            pl.BlockSpec((1, d), lambda i: (0, 0)),
            pl.BlockSpec(w_out16.shape, lambda i: (0, 0)),
            pl.BlockSpec(w_q16.shape, lambda i: (0, 0)),
        ],
        out_specs=[
            pl.BlockSpec((tl, bsz, d), lambda i: (i, 0, 0)),
            pl.BlockSpec((d, tm), lambda i: (0, i)),
            pl.BlockSpec((tm, nq), lambda i: (i, 0)),
        ],
        out_shape=(
            jax.ShapeDtypeStruct((seq, bsz, d), F32),
            jax.ShapeDtypeStruct((d, rows), BF16),
            jax.ShapeDtypeStruct((rows, nq), BF16),
        ),
        compiler_params=_params(("arbitrary",)),
    )(x_tm, y, mod_l, g_ffn.reshape(1, d), w_out16, w_q16)


def _bitonic_sort_network(n):
    ops = []
    k = 2
    while k <= n:
        j = k // 2
        while j >= 1:
            for i in range(n):
                l = i ^ j
                if l > i:
                    ops.append((i, l, (i & k) == 0))
            j //= 2
        k *= 2
    return ops


def _bitonic_merge_network(n):
    ops = []
    j = n // 2
    while j >= 1:
        for i in range(n):
            l = i ^ j
            if l > i:
                ops.append((i, l, True))
        j //= 2
    return ops


def _apply_network(vals, ops):
    vals = list(vals)
    for i, j, first_max in ops:
        hi = jnp.maximum(vals[i], vals[j])
        lo = jnp.minimum(vals[i], vals[j])
        vals[i], vals[j] = (hi, lo) if first_max else (lo, hi)
    return vals


def _top_sorted(s, k):
    n_keys = s.shape[0]
    n_slabs = n_keys // SUBLANES
    assert n_slabs == k
    slabs = [s[i * SUBLANES:(i + 1) * SUBLANES, :] for i in range(n_slabs)]
    slabs = _apply_network(slabs, _bitonic_sort_network(k))
    merge = _bitonic_merge_network(k)
    shift = SUBLANES // 2
    while shift >= 1:
        rolled = [pltpu.roll(v, shift, 0) for v in slabs]
        slabs = [jnp.maximum(slabs[i], rolled[k - 1 - i]) for i in range(k)]
        slabs = _apply_network(slabs, merge)
        shift //= 2
    return slabs


def _staircase(k):
    return [(x, y) for x in range(k) for y in range(k) if (x + 1) * (y + 1) <= k]


def _select_tree(masks, leaves):
    if not masks:
        return leaves[0]
    half = len(leaves) // 2
    return jnp.where(masks[0], _select_tree(masks[1:], leaves[half:]), _select_tree(masks[1:], leaves[:half]))


def _prefix_count(pred, vals):
    n = len(vals)
    full = pred(vals[n - 1])
    masks = []
    count = None
    stride = n // 2
    while stride >= 1:
        pivot = _select_tree(masks, [vals[base + stride - 1] for base in range(0, n, 2 * stride)])
        hit = pred(pivot)
        masks.append(hit)
        term = jnp.where(hit, float(stride), 0.0)
        count = term if count is None else count + term
        stride //= 2
    return jnp.where(full, float(n), count)


def _retrieval_body(q_ref, keys_ref, r1_ref, e1_ref, n0_ref, c0_ref, s_scr, a_scr, b_scr,
                    afull_scr, bfull_scr):
    k = PEER_TOPK
    half = PEER_DQ // 2
    tok = q_ref.shape[0]
    neg = jnp.full((SUBLANES, tok), -jnp.inf, F32)

    for h in range(PEER_HEADS):
        for p in range(2):
            off = h * PEER_DQ + p * half
            s = _dot_nt(keys_ref[p], q_ref[:, off:off + half])
            s_scr[h, p] = s
            top = _top_sorted(s, k)
            dst, dst_full = (a_scr, afull_scr) if p == 0 else (b_scr, bfull_scr)
            for x in range(k):
                dst[x, h:h + 1, :] = top[x][0:1, :]
                dst_full[h, x] = top[x]

    a = [a_scr[x] for x in range(k)]
    b = [b_scr[y] for y in range(k)]
    pairs = _staircase(k)
    cands = [a[x] + b[y] for x, y in pairs]
    n_sort = 1
    while n_sort < len(cands):
        n_sort *= 2
    ranked = _apply_network(cands + [neg] * (n_sort - len(cands)), _bitonic_sort_network(n_sort))
    tau = ranked[k - 1]
    best = a[0] + b[0]
    z = jnp.zeros_like(tau)
    for cnd in cands:
        z = z + jnp.where(cnd >= tau, jnp.exp(cnd - best), 0.0)
    inv_z = 1.0 / z

    pack = 2 * SUBLANES
    for h in range(PEER_HEADS):
        tau_h = jnp.broadcast_to(tau[h:h + 1, :], (SUBLANES, tok))
        inv_z_h = jnp.broadcast_to(inv_z[h:h + 1, :], (SUBLANES, tok))
        b_h = [bfull_scr[h, y] for y in range(k)]
        a_first = afull_scr[h, 0]
        a_last = afull_scr[h, k - 1]
        for blk in range(N_KEYS // pack):
            ranks, exps = [], []
            for sub in range(pack // SUBLANES):
                rows = slice(blk * pack + sub * SUBLANES, blk * pack + (sub + 1) * SUBLANES)
                s0 = s_scr[h, 0, rows, :]
                s1 = s_scr[h, 1, rows, :]
                ranks.append(_prefix_count(lambda v: v > s1, b_h))
                cnt = _prefix_count(lambda v: s0 + v >= tau_h, b_h)
                n0_ref[h, rows, :] = jnp.where(s0 >= a_last, cnt, 0.0)
                c0_ref[h, rows, :] = jnp.exp(s0 - a_first) * inv_z_h
                exps.append(jnp.exp(s1 - b_h[0]))
            prow = slice(blk * pack, (blk + 1) * pack)
            r1_ref[h, prow, :] = jnp.concatenate(ranks, axis=0).astype(r1_ref.dtype)
            e1_ref[h, prow, :] = jnp.concatenate(exps, axis=0).astype(e1_ref.dtype)


def _retrieval(q, keys16):
    rows = q.shape[0]
    tok = RETR_TILE
    tab = lambda dt: jax.ShapeDtypeStruct((PEER_HEADS, N_KEYS, rows), dt)
    tspec = pl.BlockSpec((PEER_HEADS, N_KEYS, tok), lambda i: (0, 0, i))
    return pl.pallas_call(
        _retrieval_body,
        grid=(rows // tok,),
        in_specs=[
            pl.BlockSpec((tok, q.shape[1]), lambda i: (i, 0)),
            pl.BlockSpec(keys16.shape, lambda i: (0, 0, 0)),
        ],
        out_specs=[tspec, tspec, tspec, tspec],
        out_shape=(tab(BF16), tab(BF16), tab(F32), tab(F32)),
        scratch_shapes=[
            pltpu.VMEM((PEER_HEADS, 2, N_KEYS, tok), F32),
            pltpu.VMEM((PEER_TOPK, SUBLANES, tok), F32),
            pltpu.VMEM((PEER_TOPK, SUBLANES, tok), F32),
            pltpu.VMEM((PEER_HEADS, PEER_TOPK, SUBLANES, tok), F32),
            pltpu.VMEM((PEER_HEADS, PEER_TOPK, SUBLANES, tok), F32),
        ],
        compiler_params=_params(("arbitrary",)),
    )(q, keys16)


def _peer_body(h2t_ref, r1_ref, e1_ref, n0_ref, c0_ref, u_ref, vt_ref, x1_ref, mod_ref,
               x2_ref, acc_scr, a_scr, p_scr):
    j = pl.program_id(1)
    n_j = pl.num_programs(1)
    te = u_ref.shape[0]
    tok = h2t_ref.shape[1]
    rows_per_step = te // N_KEYS
    pack = 2 * SUBLANES

    @pl.when(j == 0)
    def _():
        acc_scr[...] = jnp.zeros_like(acc_scr)
        a_scr[...] = _dot(u_ref[...], h2t_ref[...]).astype(BF16)

    @pl.when(j > 0)
    def _():
        zero = jnp.zeros((pack, PEER_PIECE), BF16)
        for c in range(tok // PEER_PIECE):
            cols = slice(c * PEER_PIECE, (c + 1) * PEER_PIECE)
            for r in range(rows_per_step):
                gates = [None] * (N_KEYS // pack)
                for h in range(PEER_HEADS):
                    nb = jnp.broadcast_to(n0_ref[h, r:r + 1, cols], (pack, PEER_PIECE)).astype(BF16)
                    cb = jnp.broadcast_to(c0_ref[h, r:r + 1, cols], (pack, PEER_PIECE)).astype(BF16)
                    for k in range(N_KEYS // pack):
                        rs = slice(k * pack, (k + 1) * pack)
                        sel = jnp.where(r1_ref[h, rs, cols] < nb, e1_ref[h, rs, cols], zero)
                        gates[k] = sel * cb if h == 0 else gates[k] + sel * cb
                for k in range(N_KEYS // pack):
                    rs = slice(r * N_KEYS + k * pack, r * N_KEYS + (k + 1) * pack)
                    p_scr[rs, cols] = gates[k] * _gelu(a_scr[rs, cols])
            acc_scr[:, cols] += _dot(vt_ref[...], p_scr[:, cols])
            a_scr[:, cols] = _dot(u_ref[...], h2t_ref[:, cols]).astype(BF16)

    @pl.when(j == n_j - 1)
    def _():
        tl, bsz, d = x1_ref.shape
        out = acc_scr[...].T.reshape(tl, bsz, d)
        x2_ref[...] = x1_ref[...] + mod_ref[5] * out


def _peer_dense(h2t, tables, u16, vt16, x1, mod_l):
    seq, bsz, d = x1.shape
    tok = PEER_TOKEN_TILE
    tl = tok // bsz
    te = EXPERT_TILE
    n_tiles = u16.shape[0] // te
    rows_per_step = te // N_KEYS
    cur = lambda j: jnp.minimum(j, n_tiles - 1)
    prev = lambda j: jnp.maximum(j - 1, 0)
    tspec = pl.BlockSpec((PEER_HEADS, N_KEYS, tok), lambda i, j: (0, 0, i))
    rspec = pl.BlockSpec((PEER_HEADS, rows_per_step, tok), lambda i, j: (0, prev(j), i))
    return pl.pallas_call(
        _peer_body,
        grid=(seq // tl, n_tiles + 1),
        in_specs=[
            pl.BlockSpec((d, tok), lambda i, j: (0, i)),
            tspec, tspec, rspec, rspec,
            pl.BlockSpec((te, d), lambda i, j: (cur(j), 0)),
            pl.BlockSpec((d, te), lambda i, j: (0, prev(j))),
            pl.BlockSpec((tl, bsz, d), lambda i, j: (i, 0, 0)),
            pl.BlockSpec((N_MOD, bsz, d), lambda i, j: (0, 0, 0)),
        ],
        out_specs=pl.BlockSpec((tl, bsz, d), lambda i, j: (i, 0, 0)),
        out_shape=jax.ShapeDtypeStruct((seq, bsz, d), F32),
        scratch_shapes=[
            pltpu.VMEM((d, tok), F32),
            pltpu.VMEM((te, tok), BF16),
            pltpu.VMEM((te, tok), BF16),
        ],
        compiler_params=_params(("arbitrary", "arbitrary")),
    )(h2t, *tables, u16, vt16, x1, mod_l)


def _final_body(x_ref, g_ref, o_ref):
    x = x_ref[...]
    o_ref[...] = x * lax.rsqrt(jnp.mean(x * x, axis=-1, keepdims=True) + EPS) * g_ref[...]


def _final_norm(x_tm, g):
    seq, bsz, d = x_tm.shape
    tl = TOKEN_TILE // bsz
    return pl.pallas_call(
        _final_body,
        grid=(seq // tl,),
        in_specs=[pl.BlockSpec((tl, bsz, d), lambda i: (i, 0, 0)),
                  pl.BlockSpec((1, d), lambda i: (0, 0))],
        out_specs=pl.BlockSpec((tl, bsz, d), lambda i: (i, 0, 0)),
        out_shape=jax.ShapeDtypeStruct((seq, bsz, d), F32),
        compiler_params=_params(("arbitrary",)),
    )(x_tm, g.reshape(1, d))


def _block_diag(blocks):
    n, r, c = blocks.shape
    eye = jnp.eye(n, dtype=blocks.dtype)
    return (blocks[:, :, None, :] * eye[:, None, :, None]).reshape(n * r, n * c)


def kernel(x, c, w_mod, b_mod, g_mix, w_in, hgrn_lb_logits, s5_a_re, s5_a_im, s5_b_re, s5_b_im, s5_c_re, s5_c_im, s5_log_dt, s5_d, s5_w_glu, s5_b_glu, lru_conv_w, lru_conv_b, lru_w_a, lru_b_a, lru_w_x, lru_b_x, lru_lambda, g_branch, w_out, g_ffn, peer_w_q, peer_sub_keys, peer_u, peer_v, g_final):
    bsz, seq, d = x.shape
    depth = w_in.shape[0]
    assert bsz == SUBLANES and seq % CHUNK_T == 0 and (seq * bsz) % TOKEN_TILE == 0

    x_tm = x.transpose(1, 0, 2)
    mod = _modulation(c, w_mod, b_mod)
    mod = mod.reshape(depth, bsz, N_MOD, d).transpose(0, 2, 1, 3)
    lb_all, ab_re, ab_im, bb_re, bb_im = _prepare_params(
        hgrn_lb_logits, s5_a_re, s5_a_im, s5_log_dt, s5_b_re, s5_b_im)
    tmat = jnp.asarray(_hgrn_exponent_matrix(CHUNK_T), BF16)
    row = lambda a: a.reshape(1, -1)

    for l in range(depth):
        bbd = jnp.concatenate([
            _block_diag(bb_re[l].reshape(S5_GROUP, S5_GROUPS, S5_STATE).transpose(1, 0, 2)),
            _block_diag(bb_im[l].reshape(S5_GROUP, S5_GROUPS, S5_STATE).transpose(1, 0, 2)),
        ], axis=1).astype(BF16)
        cre = _block_diag(s5_c_re[l].transpose(0, 2, 1)).astype(BF16)
        cim = _block_diag(s5_c_im[l].transpose(0, 2, 1)).astype(BF16)
        s5p = (ab_re[l], ab_im[l], bbd, cre, cim, row(s5_d[l]),
               s5_w_glu[l].astype(BF16), row(s5_b_glu[l]))
        lrup = (lru_conv_w[l], row(lru_conv_b[l]),
                _block_diag(lru_w_a[l]).astype(BF16), row(lru_b_a[l]),
                _block_diag(lru_w_x[l]).astype(BF16), row(lru_b_x[l]), row(lru_lambda[l]))

        y = _mixer(x_tm, mod[l], g_mix[l], w_in[l].astype(BF16), tmat, lb_all[l:l + 1],
                   s5p, lrup, row(g_branch[l]))
        x1, h2t, q = _out_projection(x_tm, y, mod[l], g_ffn[l],
                                    w_out[l].astype(BF16), peer_w_q[l].astype(BF16))
        tables = _retrieval(q, peer_sub_keys[l].astype(BF16))
        x_tm = _peer_dense(h2t, tables, peer_u[l].astype(BF16),
                           peer_v[l].astype(BF16).T, x1, mod[l])

    out = _final_norm(x_tm, g_final)
    return out.transpose(1, 0, 2)
```

```python
import functools

import numpy as np
import jax
import jax.numpy as jnp
from jax import lax
from jax.experimental import pallas as pl
from jax.experimental.pallas import tpu as pltpu

F32 = jnp.float32
BF16 = jnp.bfloat16

EPS = 1e-6
N_MOD = 6
HGRN_HEADS = 4
HGRN_DK = 128
D_HGRN = HGRN_HEADS * HGRN_DK
S5_GROUP = 16
S5_GROUPS = 16
S5_STATE = 64
D_S5 = S5_GROUP * S5_GROUPS
N_S5_STATE = S5_GROUPS * S5_STATE
LRU_BLOCKS = 4
LRU_BLOCK = 64
D_LRU = LRU_BLOCKS * LRU_BLOCK
CONV_W = 4
LRU_C = 8.0
PEER_HEADS = 8
N_KEYS = 128
PEER_DQ = 256
PEER_TOPK = 16

SUBLANES = 8
LANES = 128
VMEM_LIMIT_BYTES = 56 * 1024 * 1024

CHUNK_T = 64
HGRN_GROUP = 8
TOKEN_TILE = 512
ROW_PARTS = 2
EXPERT_TILE = 1024
PEER_TOKEN_TILE = 1024
PEER_PIECE = 256
RETR_TILE = 256
MOD_COL_TILE = 1536


def _sigmoid(x):
    return 1.0 / (1.0 + jnp.exp(-x))


def _silu(x):
    return x * _sigmoid(x)


def _gelu(x):
    return 0.5 * x * (1.0 + jnp.tanh(0.7978845608028654 * (x + 0.044715 * (x * x * x))))


def _dot(a, b):
    return jnp.dot(a, b, preferred_element_type=F32)


def _dot_nt(a, b):
    return lax.dot_general(a, b, (((1,), (1,)), ((), ())), preferred_element_type=F32)


def _dot_tn(a, b):
    return lax.dot_general(a, b, (((0,), (0,)), ((), ())), preferred_element_type=F32)


def _split2(x):
    hi = x.astype(BF16)
    lo = (x - hi.astype(F32)).astype(BF16)
    return hi, lo


def _params(semantics):
    return pltpu.CompilerParams(dimension_semantics=semantics, vmem_limit_bytes=VMEM_LIMIT_BYTES)


def _mod_body(c_ref, w_ref, b_ref, o_ref):
    cond = _silu(c_ref[...])
    c_hi, c_lo = _split2(cond)
    w_hi, w_lo = _split2(w_ref[0])
    o_ref[0] = _dot(c_hi, w_hi) + _dot(c_hi, w_lo) + _dot(c_lo, w_hi) + b_ref[0]


def _modulation(c, w_mod, b_mod):
    depth, d, n = w_mod.shape
    bsz = c.shape[0]
    tn = MOD_COL_TILE
    return pl.pallas_call(
        _mod_body,
        grid=(depth, n // tn),
        in_specs=[
            pl.BlockSpec((bsz, d), lambda l, j: (0, 0)),
            pl.BlockSpec((1, d, tn), lambda l, j: (l, 0, j)),
            pl.BlockSpec((1, 1, tn), lambda l, j: (l, 0, j)),
        ],
        out_specs=pl.BlockSpec((1, bsz, tn), lambda l, j: (l, 0, j)),
        out_shape=jax.ShapeDtypeStruct((depth, bsz, n), F32),
        compiler_params=_params(("arbitrary", "arbitrary")),
    )(c, w_mod, b_mod.reshape(depth, 1, n))


def _prep_body(lbl_ref, are_ref, aim_ref, ldt_ref, bre_ref, bim_ref,
               lb_ref, abr_ref, abi_ref, bbr_ref, bbi_ref):
    logits = lbl_ref[...]
    depth = logits.shape[0]
    rows = [logits[l:l + 1] for l in range(depth)]
    mx = functools.reduce(jnp.maximum, rows)
    ex = [jnp.exp(r - mx) for r in rows]
    den = functools.reduce(lambda a, b: a + b, ex)
    soft = [e / den for e in ex]
    run = soft[0]
    for l in range(depth):
        if l > 0:
            run = run + soft[l]
        lb_ref[l:l + 1, :] = run - soft[0]

    lam_re = jnp.minimum(are_ref[...], -1e-4)
    lam_im = aim_ref[...]
    dt = jnp.exp(ldt_ref[...])
    mag = jnp.exp(lam_re * dt)
    ab_re = mag * jnp.cos(lam_im * dt)
    ab_im = mag * jnp.sin(lam_im * dt)
    den2 = lam_re * lam_re + lam_im * lam_im
    nr = ab_re - 1.0
    z_re = (nr * lam_re + ab_im * lam_im) / den2
    z_im = (ab_im * lam_re - nr * lam_im) / den2
    abr_ref[...] = ab_re
    abi_ref[...] = ab_im
    br = bre_ref[...]
    bi = bim_ref[...]
    bbr_ref[...] = z_re * br - z_im * bi
    bbi_ref[...] = z_re * bi + z_im * br


def _prepare_params(lb_logits, a_re, a_im, log_dt, b_re, b_im):
    depth = lb_logits.shape[0]
    n = N_S5_STATE
    flat = lambda a: a.reshape(depth, 1, n)
    ldt = jnp.broadcast_to(log_dt[:, :, None], (depth, S5_GROUPS, S5_STATE))
    bt = lambda b: b.transpose(0, 3, 1, 2).reshape(depth, S5_GROUP, n)
    out_shape = (
        jax.ShapeDtypeStruct((depth, D_HGRN), F32),
        jax.ShapeDtypeStruct((depth, 1, n), F32),
        jax.ShapeDtypeStruct((depth, 1, n), F32),
        jax.ShapeDtypeStruct((depth, S5_GROUP, n), F32),
        jax.ShapeDtypeStruct((depth, S5_GROUP, n), F32),
    )
    return pl.pallas_call(_prep_body, out_shape=out_shape)(
        lb_logits, flat(a_re), flat(a_im), flat(ldt), bt(b_re), bt(b_im))


def _project_inputs(x_ref, mod_ref, g_ref, w_ref, pa_ref, pg_ref, pb_ref, pc_ref):
    tl, bsz, d = x_ref.shape
    na = pa_ref.shape[0] * LANES
    ng = pg_ref.shape[1]
    nb = pb_ref.shape[1]
    part = tl // ROW_PARTS
    for s in range(ROW_PARTS):
        x = x_ref[s * part:(s + 1) * part]
        ms = jnp.mean(x * x, axis=-1, keepdims=True)
        xn = x * lax.rsqrt(ms + EPS) * g_ref[...]
        h = xn * (1.0 + mod_ref[1]) + mod_ref[0]
        h = h.reshape(part * bsz, d).astype(BF16)
        rows = slice(s * part * bsz, (s + 1) * part * bsz)
        pa = _dot(h, w_ref[:, 0:na])
        for cg in range(pa_ref.shape[0]):
            pa_ref[cg, rows, :] = pa[:, cg * LANES:(cg + 1) * LANES]
        pg_ref[rows, :] = _dot(h, w_ref[:, na:na + ng])
        pb_ref[rows, :] = _dot(h, w_ref[:, na + ng:na + ng + nb])
        pc_ref[rows, :] = _dot(h, w_ref[:, na + ng + nb:])


def _hgrn_exponent_matrix(chunk):
    n_levels = int(np.log2(chunk))
    mat = np.zeros((n_levels + 2, chunk, chunk), np.float32)
    u = np.arange(chunk)
    for k in range(n_levels):
        g = chunk >> (k + 1)
        m = 2 * g
        for t in range(chunk):
            r = (t // m) * m + g - 1
            if t % m >= g:
                mat[k, t] = (u > r) & (u <= t)
            else:
                mat[k, t] = (u > t) & (u <= r)
    for t in range(chunk):
        mat[n_levels, t] = u <= t
        mat[n_levels + 1, t] = u > t
    return mat.reshape((n_levels + 2) * chunk, chunk)


def _mixer_body(x_ref, mod_ref, g_ref, w_ref, tmat_ref, lb_ref,
                ar_ref, ai_ref, bbd_ref, cre_ref, cim_ref, dsk_ref, wglu_ref, bglu_ref,
                cw_ref, cb_ref, wa_ref, ba_ref, wx_ref, bx_ref, lam_ref, gbr_ref,
                y_ref,
                pa_ref, pg_ref, pb_ref, pc_ref,
                st_ref, o_scr, bu_scr, xs_scr, s5_state, lru_state, xext, la_scr, lb_scr, hs_scr):
    _project_inputs(x_ref, mod_ref, g_ref, w_ref, pa_ref, pg_ref, pb_ref, pc_ref)
    rows = pa_ref.shape[1]
    bsz = s5_state.shape[0]
    chunk = rows // bsz
    n_levels = tmat_ref.shape[0] // chunk - 2
    dk = HGRN_DK

    @pl.when(pl.program_id(0) == 0)
    def _():
        st_ref[...] = jnp.zeros_like(st_ref)
        s5_state[...] = jnp.zeros_like(s5_state)
        lru_state[...] = jnp.zeros_like(lru_state)
        xext[0:(CONV_W - 1) * bsz, :] = jnp.zeros(((CONV_W - 1) * bsz, D_LRU), F32)

    ti = lax.broadcasted_iota(jnp.int32, (chunk, chunk), 0)
    si = lax.broadcasted_iota(jnp.int32, (chunk, chunk), 1)
    masks = []
    for k in range(n_levels):
        g = chunk >> (k + 1)
        m = 2 * g
        same = (ti // m) == (si // m)
        masks.append(jnp.where(same, jnp.where((ti % m) >= g, jnp.where((si % m) < g, 1.0, 0.0), 0.0), 0.0))
    diag = jnp.where(ti == si, 1.0, 0.0)
    lb = lb_ref[...]
    tmat = tmat_ref[...]

    def hgrn_group(g, carry):
        grp = range(HGRN_GROUP)
        bs = [g * HGRN_GROUP + e for e in grp]
        rws = [pl.ds(b, chunk, stride=bsz) for b in bs]
        heads = lambda e, first: jnp.concatenate(
            [pa_ref[first + h, rws[e], :] for h in range(HGRN_HEADS)], axis=1)
        qa = [_silu(heads(e, 0)) for e in grp]
        f = [lb + (1.0 - lb) * _sigmoid(heads(e, HGRN_HEADS)) for e in grp]
        v16 = [heads(e, 2 * HGRN_HEADS).astype(BF16) for e in grp]
        kk = [1.0 - f[e] for e in grp]
        parts = [_split2(jnp.log(f[e])) for e in grp]
        decay = [jnp.exp(_dot(tmat, parts[e][0]) + _dot(tmat, parts[e][1])) for e in grp]
        cum = [decay[e][n_levels * chunk:(n_levels + 1) * chunk] for e in grp]
        qb16 = [(qa[e] * cum[e]).astype(BF16) for e in grp]
        ks16 = [(kk[e] * decay[e][(n_levels + 1) * chunk:]).astype(BF16) for e in grp]
        total = [cum[e][chunk - 1:chunk, :] for e in grp]
        qa16 = [qa[e].astype(BF16) for e in grp]
        kk16 = [kk[e].astype(BF16) for e in grp]
        for h in range(HGRN_HEADS):
            sl = slice(h * dk, (h + 1) * dk)
            scores = [diag * _dot_nt(qa16[e][:, sl], kk16[e][:, sl]) for e in grp]
            for k in range(n_levels):
                for e in grp:
                    dl = decay[e][k * chunk:(k + 1) * chunk, sl]
                    ql = (qa[e][:, sl] * dl).astype(BF16)
                    kl = (kk[e][:, sl] * dl).astype(BF16)
                    scores[e] = scores[e] + masks[k] * _dot_nt(ql, kl)
            for e in grp:
                st = st_ref[bs[e], h]
                o = (_dot(scores[e].astype(BF16), v16[e][:, sl])
                     + _dot_nt(qb16[e][:, sl], st.astype(BF16)))
                o_scr[h, rws[e], :] = o
                st_ref[bs[e], h] = st * total[e][:, sl] + _dot_tn(v16[e][:, sl], ks16[e][:, sl])
        return carry

    lax.fori_loop(0, bsz // HGRN_GROUP, hgrn_group, 0)

    gate = _silu(pg_ref[...])
    ya = []
    ssq = None
    for h in range(HGRN_HEADS):
        sl = slice(h * dk, (h + 1) * dk)
        oh = o_scr[h]
        oh = oh * lax.rsqrt(jnp.mean(oh * oh, axis=-1, keepdims=True) + EPS) * gate[:, sl]
        ya.append(oh)
        s = jnp.sum(oh * oh, axis=-1, keepdims=True)
        ssq = s if ssq is None else ssq + s
    scale = lax.rsqrt(ssq * (1.0 / D_HGRN) + EPS)
    for h in range(HGRN_HEADS):
        sl = slice(h * dk, (h + 1) * dk)
        y_ref[:, sl] = (ya[h] * scale * gbr_ref[:, sl]).astype(y_ref.dtype)

    n = N_S5_STATE
    u = pb_ref[...]
    bu_scr[...] = _dot(u.astype(BF16), bbd_ref[...])
    ar = jnp.broadcast_to(ar_ref[...], (bsz, n))
    ai = jnp.broadcast_to(ai_ref[...], (bsz, n))

    tail = (CONV_W - 1) * bsz
    xin = pc_ref[:, 0:D_LRU]
    xext[tail:tail + rows, :] = xin
    xc = cb_ref[...]
    for w in range(CONV_W):
        xc = xc + cw_ref[w:w + 1, :] * xext[w * bsz:w * bsz + rows, :]
    xext[0:tail, :] = xin[rows - tail:rows, :]
    xc16 = xc.astype(BF16)
    rg = _sigmoid(_dot(xc16, wa_ref[...]) + ba_ref[...])
    ig = _sigmoid(_dot(xc16, wx_ref[...]) + bx_ref[...])
    nl = -lam_ref[...]
    softplus = jnp.maximum(nl, 0.0) + jnp.log(1.0 + jnp.exp(-jnp.abs(nl)))
    log_a = (-LRU_C) * rg * softplus
    la_scr[...] = jnp.exp(log_a)
    lb_scr[...] = jnp.sqrt(1.0 - jnp.exp(2.0 * log_a)) * (ig * xc)

    def scan_step(t, st):
        xr, xi, hstate = st
        r = pl.ds(pl.multiple_of(t * bsz, bsz), bsz)
        nxr = ar * xr - ai * xi + bu_scr[r, 0:n]
        nxi = ar * xi + ai * xr + bu_scr[r, n:2 * n]
        xs_scr[r, 0:n] = nxr
        xs_scr[r, n:2 * n] = nxi
        hn = la_scr[r, :] * hstate + lb_scr[r, :]
        hs_scr[r, :] = hn
        return nxr, nxi, hn

    xr, xi, hlast = lax.fori_loop(
        0, chunk, scan_step, (s5_state[:, 0:n], s5_state[:, n:2 * n], lru_state[...]))
    s5_state[:, 0:n] = xr
    s5_state[:, n:2 * n] = xi
    lru_state[...] = hlast

    yb = (_dot(xs_scr[:, 0:n].astype(BF16), cre_ref[...])
          - _dot(xs_scr[:, n:2 * n].astype(BF16), cim_ref[...]))
    yb = yb + dsk_ref[...] * u
    z = _gelu(yb)
    yb = z * _sigmoid(_dot(z.astype(BF16), wglu_ref[...]) + bglu_ref[...])
    yb = yb * lax.rsqrt(jnp.mean(yb * yb, axis=-1, keepdims=True) + EPS)
    y_ref[:, D_HGRN:D_HGRN + D_S5] = (yb * gbr_ref[:, D_HGRN:D_HGRN + D_S5]).astype(y_ref.dtype)

    yc = hs_scr[...] * _gelu(pc_ref[:, D_LRU:2 * D_LRU])
    yc = yc * lax.rsqrt(jnp.mean(yc * yc, axis=-1, keepdims=True) + EPS)
    y_ref[:, D_HGRN + D_S5:] = (yc * gbr_ref[:, D_HGRN + D_S5:]).astype(y_ref.dtype)


def _mixer(x_tm, mod_l, g_mix, w_in16, tmat, lb, s5p, lrup, g_branch):
    seq, bsz, d = x_tm.shape
    rows = CHUNK_T * bsz
    d_mix = D_HGRN + D_S5 + D_LRU
    n = N_S5_STATE
    full = lambda a: pl.BlockSpec(a.shape, lambda i, nd=a.ndim: (0,) * nd)
    consts = [mod_l, g_mix.reshape(1, d), w_in16, tmat, lb, *s5p, *lrup, g_branch]
    return pl.pallas_call(
        _mixer_body,
        grid=(seq // CHUNK_T,),
        in_specs=[pl.BlockSpec((CHUNK_T, bsz, d), lambda i: (i, 0, 0))] + [full(a) for a in consts],
        out_specs=pl.BlockSpec((rows, d_mix), lambda i: (i, 0)),
        out_shape=jax.ShapeDtypeStruct((seq * bsz, d_mix), BF16),
        scratch_shapes=[
            pltpu.VMEM((3 * D_HGRN // LANES, rows, LANES), F32),
            pltpu.VMEM((rows, D_HGRN), F32),
            pltpu.VMEM((rows, D_S5), F32),
            pltpu.VMEM((rows, 2 * D_LRU), F32),
            pltpu.VMEM((bsz, HGRN_HEADS, HGRN_DK, HGRN_DK), F32),
            pltpu.VMEM((HGRN_HEADS, rows, HGRN_DK), F32),
            pltpu.VMEM((rows, 2 * n), F32),
            pltpu.VMEM((rows, 2 * n), F32),
            pltpu.VMEM((bsz, 2 * n), F32),
            pltpu.VMEM((bsz, D_LRU), F32),
            pltpu.VMEM(((CONV_W - 1) * bsz + rows, D_LRU), F32),
            pltpu.VMEM((rows, D_LRU), F32),
            pltpu.VMEM((rows, D_LRU), F32),
            pltpu.VMEM((rows, D_LRU), F32),
        ],
        compiler_params=_params(("arbitrary",)),
    )(x_tm, *consts)


def _outproj_body(x_ref, y_ref, mod_ref, g_ref, wo_ref, wq_ref, x1_ref, h2t_ref, q_ref):
    tl, bsz, d = x_ref.shape
    part = tl // ROW_PARTS
    parts = range(ROW_PARTS)
    rows = [slice(s * part * bsz, (s + 1) * part * bsz) for s in parts]
    o = [_dot(y_ref[rows[s], :], wo_ref[...]).reshape(part, bsz, d) for s in parts]
    x1 = [x_ref[s * part:(s + 1) * part] + mod_ref[2] * o[s] for s in parts]
    h2 = []
    for s in parts:
        x1_ref[s * part:(s + 1) * part] = x1[s]
        ms = jnp.mean(x1[s] * x1[s], axis=-1, keepdims=True)
        hs = x1[s] * lax.rsqrt(ms + EPS) * g_ref[...]
        hs = hs * (1.0 + mod_ref[4]) + mod_ref[3]
        h2.append(hs.reshape(part * bsz, d))
    for s in parts:
        q_ref[rows[s], :] = _dot(h2[s].astype(BF16), wq_ref[...]).astype(q_ref.dtype)
        h2t_ref[:, rows[s]] = h2[s].T.astype(BF16)


def _out_projection(x_tm, y, mod_l, g_ffn, w_out16, w_q16):
    seq, bsz, d = x_tm.shape
    tl = TOKEN_TILE // bsz
    tm = tl * bsz
    rows = seq * bsz
    nq = w_q16.shape[1]
    return pl.pallas_call(
        _outproj_body,
        grid=(seq // tl,),
        in_specs=[
            pl.BlockSpec((tl, bsz, d), lambda i: (i, 0, 0)),
            pl.BlockSpec((tm, y.shape[1]), lambda i: (i, 0)),
            pl.BlockSpec((N_MOD, bsz, d), lambda i: (0, 0, 0)),
            pl.BlockSpec((1, d), lambda i: (0, 0)),
            pl.BlockSpec(w_out16.shape, lambda i: (0, 0)),
            pl.BlockSpec(w_q16.shape, lambda i: (0, 0)),
        ],
        out_specs=[
            pl.BlockSpec((tl, bsz, d), lambda i: (i, 0, 0)),
            pl.BlockSpec((d, tm), lambda i: (0, i)),
            pl.BlockSpec((tm, nq), lambda i: (i, 0)),
        ],
        out_shape=(
            jax.ShapeDtypeStruct((seq, bsz, d), F32),
            jax.ShapeDtypeStruct((d, rows), BF16),
            jax.ShapeDtypeStruct((rows, nq), BF16),
        ),
        compiler_params=_params(("arbitrary",)),
    )(x_tm, y, mod_l, g_ffn.reshape(1, d), w_out16, w_q16)


def _bitonic_sort_network(n):
    ops = []
    k = 2
    while k <= n:
        j = k // 2
        while j >= 1:
            for i in range(n):
                l = i ^ j
                if l > i:
                    ops.append((i, l, (i & k) == 0))
            j //= 2
        k *= 2
    return ops


def _bitonic_merge_network(n):
    ops = []
    j = n // 2
    while j >= 1:
        for i in range(n):
            l = i ^ j
            if l > i:
                ops.append((i, l, True))
        j //= 2
    return ops


def _apply_network(vals, ops):
    vals = list(vals)
    for i, j, first_max in ops:
        hi = jnp.maximum(vals[i], vals[j])
        lo = jnp.minimum(vals[i], vals[j])
        vals[i], vals[j] = (hi, lo) if first_max else (lo, hi)
    return vals


def _top_sorted(s, k):
    n_keys = s.shape[0]
    n_slabs = n_keys // SUBLANES
    assert n_slabs == k
    slabs = [s[i * SUBLANES:(i + 1) * SUBLANES, :] for i in range(n_slabs)]
    slabs = _apply_network(slabs, _bitonic_sort_network(k))
    merge = _bitonic_merge_network(k)
    shift = SUBLANES // 2
    while shift >= 1:
        rolled = [pltpu.roll(v, shift, 0) for v in slabs]
        slabs = [jnp.maximum(slabs[i], rolled[k - 1 - i]) for i in range(k)]
        slabs = _apply_network(slabs, merge)
        shift //= 2
    return slabs


def _staircase(k):
    return [(x, y) for x in range(k) for y in range(k) if (x + 1) * (y + 1) <= k]


def _select_tree(masks, leaves):
    if not masks:
        return leaves[0]
    half = len(leaves) // 2
    return jnp.where(masks[0], _select_tree(masks[1:], leaves[half:]), _select_tree(masks[1:], leaves[:half]))


def _prefix_count(pred, vals):
    n = len(vals)
    full = pred(vals[n - 1])
    masks = []
    count = None
    stride = n // 2
    while stride >= 1:
        pivot = _select_tree(masks, [vals[base + stride - 1] for base in range(0, n, 2 * stride)])
        hit = pred(pivot)
        masks.append(hit)
        term = jnp.where(hit, float(stride), 0.0)
        count = term if count is None else count + term
        stride //= 2
    return jnp.where(full, float(n), count)


def _retrieval_body(q_ref, keys_ref, r1_ref, e1_ref, n0_ref, c0_ref, s_scr, a_scr, b_scr,
                    afull_scr, bfull_scr):
    k = PEER_TOPK
    half = PEER_DQ // 2
    tok = q_ref.shape[0]
    neg = jnp.full((SUBLANES, tok), -jnp.inf, F32)

    for h in range(PEER_HEADS):
        for p in range(2):
            off = h * PEER_DQ + p * half
            s = _dot_nt(keys_ref[p], q_ref[:, off:off + half])
            s_scr[h, p] = s
            top = _top_sorted(s, k)
            dst, dst_full = (a_scr, afull_scr) if p == 0 else (b_scr, bfull_scr)
            for x in range(k):
                dst[x, h:h + 1, :] = top[x][0:1, :]
                dst_full[h, x] = top[x]

    a = [a_scr[x] for x in range(k)]
    b = [b_scr[y] for y in range(k)]
    pairs = _staircase(k)
    cands = [a[x] + b[y] for x, y in pairs]
    n_sort = 1
    while n_sort < len(cands):
        n_sort *= 2
    ranked = _apply_network(cands + [neg] * (n_sort - len(cands)), _bitonic_sort_network(n_sort))
    tau = ranked[k - 1]
    best = a[0] + b[0]
    z = jnp.zeros_like(tau)
    for cnd in cands:
        z = z + jnp.where(cnd >= tau, jnp.exp(cnd - best), 0.0)
    inv_z = 1.0 / z

    pack = 2 * SUBLANES
    for h in range(PEER_HEADS):
        tau_h = jnp.broadcast_to(tau[h:h + 1, :], (SUBLANES, tok))
        inv_z_h = jnp.broadcast_to(inv_z[h:h + 1, :], (SUBLANES, tok))
        b_h = [bfull_scr[h, y] for y in range(k)]
        a_first = afull_scr[h, 0]
        a_last = afull_scr[h, k - 1]
        for blk in range(N_KEYS // pack):
            ranks, exps = [], []
            for sub in range(pack // SUBLANES):
                rows = slice(blk * pack + sub * SUBLANES, blk * pack + (sub + 1) * SUBLANES)
                s0 = s_scr[h, 0, rows, :]
                s1 = s_scr[h, 1, rows, :]
                ranks.append(_prefix_count(lambda v: v > s1, b_h))
                cnt = _prefix_count(lambda v: s0 + v >= tau_h, b_h)
                n0_ref[h, rows, :] = jnp.where(s0 >= a_last, cnt, 0.0)
                c0_ref[h, rows, :] = jnp.exp(s0 - a_first) * inv_z_h
                exps.append(jnp.exp(s1 - b_h[0]))
            prow = slice(blk * pack, (blk + 1) * pack)
            r1_ref[h, prow, :] = jnp.concatenate(ranks, axis=0).astype(r1_ref.dtype)
            e1_ref[h, prow, :] = jnp.concatenate(exps, axis=0).astype(e1_ref.dtype)


def _retrieval(q, keys16):
    rows = q.shape[0]
    tok = RETR_TILE
    tab = lambda dt: jax.ShapeDtypeStruct((PEER_HEADS, N_KEYS, rows), dt)
    tspec = pl.BlockSpec((PEER_HEADS, N_KEYS, tok), lambda i: (0, 0, i))
    return pl.pallas_call(
        _retrieval_body,
        grid=(rows // tok,),
        in_specs=[
            pl.BlockSpec((tok, q.shape[1]), lambda i: (i, 0)),
            pl.BlockSpec(keys16.shape, lambda i: (0, 0, 0)),
        ],
        out_specs=[tspec, tspec, tspec, tspec],
        out_shape=(tab(BF16), tab(BF16), tab(F32), tab(F32)),
        scratch_shapes=[
            pltpu.VMEM((PEER_HEADS, 2, N_KEYS, tok), F32),
            pltpu.VMEM((PEER_TOPK, SUBLANES, tok), F32),
            pltpu.VMEM((PEER_TOPK, SUBLANES, tok), F32),
            pltpu.VMEM((PEER_HEADS, PEER_TOPK, SUBLANES, tok), F32),
            pltpu.VMEM((PEER_HEADS, PEER_TOPK, SUBLANES, tok), F32),
        ],
        compiler_params=_params(("arbitrary",)),
    )(q, keys16)


def _peer_body(h2t_ref, r1_ref, e1_ref, n0_ref, c0_ref, u_ref, vt_ref, x1_ref, mod_ref,
               x2_ref, acc_scr, a_scr, p_scr):
    j = pl.program_id(1)
    n_j = pl.num_programs(1)
    te = u_ref.shape[0]
    tok = h2t_ref.shape[1]
    rows_per_step = te // N_KEYS
    pack = 2 * SUBLANES

    @pl.when(j == 0)
    def _():
        acc_scr[...] = jnp.zeros_like(acc_scr)
        a_scr[...] = _dot(u_ref[...], h2t_ref[...]).astype(BF16)

    @pl.when(j > 0)
    def _():
        zero = jnp.zeros((pack, PEER_PIECE), BF16)
        for c in range(tok // PEER_PIECE):
            cols = slice(c * PEER_PIECE, (c + 1) * PEER_PIECE)
            for r in range(rows_per_step):
                gates = [zero for _ in range(N_KEYS // pack)]
                for h in range(PEER_HEADS):
                    nb = jnp.broadcast_to(n0_ref[h, r:r + 1, cols], (pack, PEER_PIECE)).astype(BF16)
                    cb = jnp.broadcast_to(c0_ref[h, r:r + 1, cols], (pack, PEER_PIECE)).astype(BF16)
                    for k in range(N_KEYS // pack):
                        rs = slice(k * pack, (k + 1) * pack)
                        sel = jnp.where(r1_ref[h, rs, cols] < nb, e1_ref[h, rs, cols], zero)
                        gates[k] = gates[k] + sel * cb
                for k in range(N_KEYS // pack):
                    rs = slice(r * N_KEYS + k * pack, r * N_KEYS + (k + 1) * pack)
                    p_scr[rs, cols] = gates[k] * _gelu(a_scr[rs, cols])
            acc_scr[:, cols] += _dot(vt_ref[...], p_scr[:, cols])
            a_scr[:, cols] = _dot(u_ref[...], h2t_ref[:, cols]).astype(BF16)

    @pl.when(j == n_j - 1)
    def _():
        tl, bsz, d = x1_ref.shape
        out = acc_scr[...].T.reshape(tl, bsz, d)
        x2_ref[...] = x1_ref[...] + mod_ref[5] * out


def _peer_dense(h2t, tables, u16, vt16, x1, mod_l):
    seq, bsz, d = x1.shape
    tok = PEER_TOKEN_TILE
    tl = tok // bsz
    te = EXPERT_TILE
    n_tiles = u16.shape[0] // te
    rows_per_step = te // N_KEYS
    cur = lambda j: jnp.minimum(j, n_tiles - 1)
    prev = lambda j: jnp.maximum(j - 1, 0)
    tspec = pl.BlockSpec((PEER_HEADS, N_KEYS, tok), lambda i, j: (0, 0, i))
    rspec = pl.BlockSpec((PEER_HEADS, rows_per_step, tok), lambda i, j: (0, prev(j), i))
    return pl.pallas_call(
        _peer_body,
        grid=(seq // tl, n_tiles + 1),
        in_specs=[
            pl.BlockSpec((d, tok), lambda i, j: (0, i)),
            tspec, tspec, rspec, rspec,
            pl.BlockSpec((te, d), lambda i, j: (cur(j), 0)),
            pl.BlockSpec((d, te), lambda i, j: (0, prev(j))),
            pl.BlockSpec((tl, bsz, d), lambda i, j: (i, 0, 0)),
            pl.BlockSpec((N_MOD, bsz, d), lambda i, j: (0, 0, 0)),
        ],
        out_specs=pl.BlockSpec((tl, bsz, d), lambda i, j: (i, 0, 0)),
        out_shape=jax.ShapeDtypeStruct((seq, bsz, d), F32),
        scratch_shapes=[
            pltpu.VMEM((d, tok), F32),
            pltpu.VMEM((te, tok), BF16),
            pltpu.VMEM((te, tok), BF16),
        ],
        compiler_params=_params(("arbitrary", "arbitrary")),
    )(h2t, *tables, u16, vt16, x1, mod_l)


def _final_body(x_ref, g_ref, o_ref):
    x = x_ref[...]
    o_ref[...] = x * lax.rsqrt(jnp.mean(x * x, axis=-1, keepdims=True) + EPS) * g_ref[...]


def _final_norm(x_tm, g):
    seq, bsz, d = x_tm.shape
    tl = TOKEN_TILE // bsz
    return pl.pallas_call(
        _final_body,
        grid=(seq // tl,),
        in_specs=[pl.BlockSpec((tl, bsz, d), lambda i: (i, 0, 0)),
                  pl.BlockSpec((1, d), lambda i: (0, 0))],
        out_specs=pl.BlockSpec((tl, bsz, d), lambda i: (i, 0, 0)),
        out_shape=jax.ShapeDtypeStruct((seq, bsz, d), F32),
        compiler_params=_params(("arbitrary",)),
    )(x_tm, g.reshape(1, d))


def _block_diag(blocks):
    n, r, c = blocks.shape
    eye = jnp.eye(n, dtype=blocks.dtype)
    return (blocks[:, :, None, :] * eye[:, None, :, None]).reshape(n * r, n * c)


def kernel(x, c, w_mod, b_mod, g_mix, w_in, hgrn_lb_logits, s5_a_re, s5_a_im, s5_b_re, s5_b_im, s5_c_re, s5_c_im, s5_log_dt, s5_d, s5_w_glu, s5_b_glu, lru_conv_w, lru_conv_b, lru_w_a, lru_b_a, lru_w_x, lru_b_x, lru_lambda, g_branch, w_out, g_ffn, peer_w_q, peer_sub_keys, peer_u, peer_v, g_final):
    bsz, seq, d = x.shape
    depth = w_in.shape[0]
    assert bsz == SUBLANES and seq % CHUNK_T == 0 and (seq * bsz) % TOKEN_TILE == 0

    x_tm = x.transpose(1, 0, 2)
    mod = _modulation(c, w_mod, b_mod)
    mod = mod.reshape(depth, bsz, N_MOD, d).transpose(0, 2, 1, 3)
    lb_all, ab_re, ab_im, bb_re, bb_im = _prepare_params(
        hgrn_lb_logits, s5_a_re, s5_a_im, s5_log_dt, s5_b_re, s5_b_im)
    tmat = jnp.asarray(_hgrn_exponent_matrix(CHUNK_T), BF16)
    row = lambda a: a.reshape(1, -1)

    for l in range(depth):
        bbd = jnp.concatenate([
            _block_diag(bb_re[l].reshape(S5_GROUP, S5_GROUPS, S5_STATE).transpose(1, 0, 2)),
            _block_diag(bb_im[l].reshape(S5_GROUP, S5_GROUPS, S5_STATE).transpose(1, 0, 2)),
        ], axis=1).astype(BF16)
        cre = _block_diag(s5_c_re[l].transpose(0, 2, 1)).astype(BF16)
        cim = _block_diag(s5_c_im[l].transpose(0, 2, 1)).astype(BF16)
        s5p = (ab_re[l], ab_im[l], bbd, cre, cim, row(s5_d[l]),
               s5_w_glu[l].astype(BF16), row(s5_b_glu[l]))
        lrup = (lru_conv_w[l], row(lru_conv_b[l]),
                _block_diag(lru_w_a[l]).astype(BF16), row(lru_b_a[l]),
                _block_diag(lru_w_x[l]).astype(BF16), row(lru_b_x[l]), row(lru_lambda[l]))

        y = _mixer(x_tm, mod[l], g_mix[l], w_in[l].astype(BF16), tmat, lb_all[l:l + 1],
                   s5p, lrup, row(g_branch[l]))
        x1, h2t, q = _out_projection(x_tm, y, mod[l], g_ffn[l],
                                    w_out[l].astype(BF16), peer_w_q[l].astype(BF16))
        tables = _retrieval(q, peer_sub_keys[l].astype(BF16))
        x_tm = _peer_dense(h2t, tables, peer_u[l].astype(BF16),
                           peer_v[l].astype(BF16).T, x1, mod[l])

    out = _final_norm(x_tm, g_final)
    return out.transpose(1, 0, 2)
```

```python
import functools

import numpy as np
import jax
import jax.numpy as jnp
from jax import lax
from jax.experimental import pallas as pl
from jax.experimental.pallas import tpu as pltpu

F32 = jnp.float32
BF16 = jnp.bfloat16

EPS = 1e-6
N_MOD = 6
HGRN_HEADS = 4
HGRN_DK = 128
D_HGRN = HGRN_HEADS * HGRN_DK
S5_GROUP = 16
S5_GROUPS = 16
S5_STATE = 64
D_S5 = S5_GROUP * S5_GROUPS
N_S5_STATE = S5_GROUPS * S5_STATE
LRU_BLOCKS = 4
LRU_BLOCK = 64
D_LRU = LRU_BLOCKS * LRU_BLOCK
CONV_W = 4
LRU_C = 8.0
PEER_HEADS = 8
N_KEYS = 128
PEER_DQ = 256
PEER_TOPK = 16

SUBLANES = 8
LANES = 128
VMEM_LIMIT_BYTES = 56 * 1024 * 1024

CHUNK_T = 64
HGRN_GROUP = 8
TOKEN_TILE = 512
ROW_PARTS = 2
EXPERT_TILE = 1024
PEER_TOKEN_TILE = 1024
PEER_PIECE = 256
RETR_TILE = 256
MOD_COL_TILE = 1536


def _sigmoid(x):
    return 1.0 / (1.0 + jnp.exp(-x))


def _silu(x):
    return x * _sigmoid(x)


def _gelu(x):
    return 0.5 * x * (1.0 + jnp.tanh(0.7978845608028654 * (x + 0.044715 * (x * x * x))))


def _dot(a, b):
    return jnp.dot(a, b, preferred_element_type=F32)


def _dot_nt(a, b):
    return lax.dot_general(a, b, (((1,), (1,)), ((), ())), preferred_element_type=F32)


def _dot_tn(a, b):
    return lax.dot_general(a, b, (((0,), (0,)), ((), ())), preferred_element_type=F32)


def _split2(x):
    hi = x.astype(BF16)
    lo = (x - hi.astype(F32)).astype(BF16)
    return hi, lo


def _params(semantics):
    return pltpu.CompilerParams(dimension_semantics=semantics, vmem_limit_bytes=VMEM_LIMIT_BYTES)


def _mod_body(c_ref, w_ref, b_ref, o_ref):
    cond = _silu(c_ref[...])
    c_hi, c_lo = _split2(cond)
    w_hi, w_lo = _split2(w_ref[0])
    o_ref[0] = _dot(c_hi, w_hi) + _dot(c_hi, w_lo) + _dot(c_lo, w_hi) + b_ref[0]


def _modulation(c, w_mod, b_mod):
    depth, d, n = w_mod.shape
    bsz = c.shape[0]
    tn = MOD_COL_TILE
    return pl.pallas_call(
        _mod_body,
        grid=(depth, n // tn),
        in_specs=[
            pl.BlockSpec((bsz, d), lambda l, j: (0, 0)),
            pl.BlockSpec((1, d, tn), lambda l, j: (l, 0, j)),
            pl.BlockSpec((1, 1, tn), lambda l, j: (l, 0, j)),
        ],
        out_specs=pl.BlockSpec((1, bsz, tn), lambda l, j: (l, 0, j)),
        out_shape=jax.ShapeDtypeStruct((depth, bsz, n), F32),
        compiler_params=_params(("arbitrary", "arbitrary")),
    )(c, w_mod, b_mod.reshape(depth, 1, n))


def _prep_body(lbl_ref, are_ref, aim_ref, ldt_ref, bre_ref, bim_ref,
               lb_ref, abr_ref, abi_ref, bbr_ref, bbi_ref):
    logits = lbl_ref[...]
    depth = logits.shape[0]
    rows = [logits[l:l + 1] for l in range(depth)]
    mx = functools.reduce(jnp.maximum, rows)
    ex = [jnp.exp(r - mx) for r in rows]
    den = functools.reduce(lambda a, b: a + b, ex)
    soft = [e / den for e in ex]
    run = soft[0]
    for l in range(depth):
        if l > 0:
            run = run + soft[l]
        lb_ref[l:l + 1, :] = run - soft[0]

    lam_re = jnp.minimum(are_ref[...], -1e-4)
    lam_im = aim_ref[...]
    dt = jnp.exp(ldt_ref[...])
    mag = jnp.exp(lam_re * dt)
    ab_re = mag * jnp.cos(lam_im * dt)
    ab_im = mag * jnp.sin(lam_im * dt)
    den2 = lam_re * lam_re + lam_im * lam_im
    nr = ab_re - 1.0
    z_re = (nr * lam_re + ab_im * lam_im) / den2
    z_im = (ab_im * lam_re - nr * lam_im) / den2
    abr_ref[...] = ab_re
    abi_ref[...] = ab_im
    br = bre_ref[...]
    bi = bim_ref[...]
    bbr_ref[...] = z_re * br - z_im * bi
    bbi_ref[...] = z_re * bi + z_im * br


def _prepare_params(lb_logits, a_re, a_im, log_dt, b_re, b_im):
    depth = lb_logits.shape[0]
    n = N_S5_STATE
    flat = lambda a: a.reshape(depth, 1, n)
    ldt = jnp.broadcast_to(log_dt[:, :, None], (depth, S5_GROUPS, S5_STATE))
    bt = lambda b: b.transpose(0, 3, 1, 2).reshape(depth, S5_GROUP, n)
    out_shape = (
        jax.ShapeDtypeStruct((depth, D_HGRN), F32),
        jax.ShapeDtypeStruct((depth, 1, n), F32),
        jax.ShapeDtypeStruct((depth, 1, n), F32),
        jax.ShapeDtypeStruct((depth, S5_GROUP, n), F32),
        jax.ShapeDtypeStruct((depth, S5_GROUP, n), F32),
    )
    return pl.pallas_call(_prep_body, out_shape=out_shape)(
        lb_logits, flat(a_re), flat(a_im), flat(ldt), bt(b_re), bt(b_im))


def _project_inputs(x_ref, mod_ref, g_ref, w_ref, pa_ref, pg_ref, pb_ref, pc_ref):
    tl, bsz, d = x_ref.shape
    na = pa_ref.shape[0] * LANES
    ng = pg_ref.shape[1]
    nb = pb_ref.shape[1]
    part = tl // ROW_PARTS
    for s in range(ROW_PARTS):
        x = x_ref[s * part:(s + 1) * part]
        ms = jnp.mean(x * x, axis=-1, keepdims=True)
        xn = x * lax.rsqrt(ms + EPS) * g_ref[...]
        h = xn * (1.0 + mod_ref[1]) + mod_ref[0]
        h = h.reshape(part * bsz, d).astype(BF16)
        rows = slice(s * part * bsz, (s + 1) * part * bsz)
        pa = _dot(h, w_ref[:, 0:na])
        for cg in range(pa_ref.shape[0]):
            pa_ref[cg, rows, :] = pa[:, cg * LANES:(cg + 1) * LANES]
        pg_ref[rows, :] = _dot(h, w_ref[:, na:na + ng])
        pb_ref[rows, :] = _dot(h, w_ref[:, na + ng:na + ng + nb])
        pc_ref[rows, :] = _dot(h, w_ref[:, na + ng + nb:])


def _hgrn_exponent_matrix(chunk):
    n_levels = int(np.log2(chunk))
    mat = np.zeros((n_levels + 2, chunk, chunk), np.float32)
    u = np.arange(chunk)
    for k in range(n_levels):
        g = chunk >> (k + 1)
        m = 2 * g
        for t in range(chunk):
            r = (t // m) * m + g - 1
            if t % m >= g:
                mat[k, t] = (u > r) & (u <= t)
            else:
                mat[k, t] = (u > t) & (u <= r)
    for t in range(chunk):
        mat[n_levels, t] = u <= t
        mat[n_levels + 1, t] = u > t
    return mat.reshape((n_levels + 2) * chunk, chunk)


def _mixer_body(x_ref, mod_ref, g_ref, w_ref, tmat_ref, lb_ref,
                ar_ref, ai_ref, bbd_ref, cre_ref, cim_ref, dsk_ref, wglu_ref, bglu_ref,
                cw_ref, cb_ref, wa_ref, ba_ref, wx_ref, bx_ref, lam_ref, gbr_ref,
                y_ref,
                pa_ref, pg_ref, pb_ref, pc_ref,
                st_ref, o_scr, bu_scr, xs_scr, s5_state, lru_state, xext, la_scr, lb_scr, hs_scr):
    _project_inputs(x_ref, mod_ref, g_ref, w_ref, pa_ref, pg_ref, pb_ref, pc_ref)
    rows = pa_ref.shape[1]
    bsz = s5_state.shape[0]
    chunk = rows // bsz
    n_levels = tmat_ref.shape[0] // chunk - 2
    dk = HGRN_DK

    @pl.when(pl.program_id(0) == 0)
    def _():
        st_ref[...] = jnp.zeros_like(st_ref)
        s5_state[...] = jnp.zeros_like(s5_state)
        lru_state[...] = jnp.zeros_like(lru_state)
        xext[0:(CONV_W - 1) * bsz, :] = jnp.zeros(((CONV_W - 1) * bsz, D_LRU), F32)

    ti = lax.broadcasted_iota(jnp.int32, (chunk, chunk), 0)
    si = lax.broadcasted_iota(jnp.int32, (chunk, chunk), 1)
    masks = []
    for k in range(n_levels):
        g = chunk >> (k + 1)
        m = 2 * g
        same = (ti // m) == (si // m)
        masks.append(jnp.where(same, jnp.where((ti % m) >= g, jnp.where((si % m) < g, 1.0, 0.0), 0.0), 0.0))
    diag = jnp.where(ti == si, 1.0, 0.0)
    lb = lb_ref[...]
    tmat = tmat_ref[...]

    def hgrn_group(g, carry):
        grp = range(HGRN_GROUP)
        bs = [g * HGRN_GROUP + e for e in grp]
        rws = [pl.ds(b, chunk, stride=bsz) for b in bs]
        heads = lambda e, first: jnp.concatenate(
            [pa_ref[first + h, rws[e], :] for h in range(HGRN_HEADS)], axis=1)
        qa = [_silu(heads(e, 0)) for e in grp]
        f = [lb + (1.0 - lb) * _sigmoid(heads(e, HGRN_HEADS)) for e in grp]
        v16 = [heads(e, 2 * HGRN_HEADS).astype(BF16) for e in grp]
        kk = [1.0 - f[e] for e in grp]
        parts = [_split2(jnp.log(f[e])) for e in grp]
        decay = [jnp.exp(_dot(tmat, parts[e][0]) + _dot(tmat, parts[e][1])) for e in grp]
        cum = [decay[e][n_levels * chunk:(n_levels + 1) * chunk] for e in grp]
        qb16 = [(qa[e] * cum[e]).astype(BF16) for e in grp]
        ks16 = [(kk[e] * decay[e][(n_levels + 1) * chunk:]).astype(BF16) for e in grp]
        total = [cum[e][chunk - 1:chunk, :] for e in grp]
        qa16 = [qa[e].astype(BF16) for e in grp]
        kk16 = [kk[e].astype(BF16) for e in grp]
        for h in range(HGRN_HEADS):
            sl = slice(h * dk, (h + 1) * dk)
            scores = [diag * _dot_nt(qa16[e][:, sl], kk16[e][:, sl]) for e in grp]
            for k in range(n_levels):
                for e in grp:
                    dl = decay[e][k * chunk:(k + 1) * chunk, sl]
                    ql = (qa[e][:, sl] * dl).astype(BF16)
                    kl = (kk[e][:, sl] * dl).astype(BF16)
                    scores[e] = scores[e] + masks[k] * _dot_nt(ql, kl)
            for e in grp:
                st = st_ref[bs[e], h]
                o = (_dot(scores[e].astype(BF16), v16[e][:, sl])
                     + _dot_nt(qb16[e][:, sl], st.astype(BF16)))
                o_scr[h, rws[e], :] = o
                st_ref[bs[e], h] = st * total[e][:, sl] + _dot_tn(v16[e][:, sl], ks16[e][:, sl])
        return carry

    lax.fori_loop(0, bsz // HGRN_GROUP, hgrn_group, 0)

    gate = _silu(pg_ref[...])
    ya = []
    ssq = None
    for h in range(HGRN_HEADS):
        sl = slice(h * dk, (h + 1) * dk)
        oh = o_scr[h]
        oh = oh * lax.rsqrt(jnp.mean(oh * oh, axis=-1, keepdims=True) + EPS) * gate[:, sl]
        ya.append(oh)
        s = jnp.sum(oh * oh, axis=-1, keepdims=True)
        ssq = s if ssq is None else ssq + s
    scale = lax.rsqrt(ssq * (1.0 / D_HGRN) + EPS)
    for h in range(HGRN_HEADS):
        sl = slice(h * dk, (h + 1) * dk)
        y_ref[:, sl] = (ya[h] * scale * gbr_ref[:, sl]).astype(y_ref.dtype)

    n = N_S5_STATE
    u = pb_ref[...]
    bu_scr[...] = _dot(u.astype(BF16), bbd_ref[...])
    ar = jnp.broadcast_to(ar_ref[...], (bsz, n))
    ai = jnp.broadcast_to(ai_ref[...], (bsz, n))

    tail = (CONV_W - 1) * bsz
    xin = pc_ref[:, 0:D_LRU]
    xext[tail:tail + rows, :] = xin
    xc = cb_ref[...]
    for w in range(CONV_W):
        xc = xc + cw_ref[w:w + 1, :] * xext[w * bsz:w * bsz + rows, :]
    xext[0:tail, :] = xin[rows - tail:rows, :]
    xc16 = xc.astype(BF16)
    rg = _sigmoid(_dot(xc16, wa_ref[...]) + ba_ref[...])
    ig = _sigmoid(_dot(xc16, wx_ref[...]) + bx_ref[...])
    nl = -lam_ref[...]
    softplus = jnp.maximum(nl, 0.0) + jnp.log(1.0 + jnp.exp(-jnp.abs(nl)))
    log_a = (-LRU_C) * rg * softplus
    la_scr[...] = jnp.exp(log_a)
    lb_scr[...] = jnp.sqrt(1.0 - jnp.exp(2.0 * log_a)) * (ig * xc)

    def scan_step(t, st):
        xr, xi, hstate = st
        r = pl.ds(pl.multiple_of(t * bsz, bsz), bsz)
        nxr = ar * xr - ai * xi + bu_scr[r, 0:n]
        nxi = ar * xi + ai * xr + bu_scr[r, n:2 * n]
        xs_scr[r, 0:n] = nxr
        xs_scr[r, n:2 * n] = nxi
        hn = la_scr[r, :] * hstate + lb_scr[r, :]
        hs_scr[r, :] = hn
        return nxr, nxi, hn

    xr, xi, hlast = lax.fori_loop(
        0, chunk, scan_step, (s5_state[:, 0:n], s5_state[:, n:2 * n], lru_state[...]))
    s5_state[:, 0:n] = xr
    s5_state[:, n:2 * n] = xi
    lru_state[...] = hlast

    yb = (_dot(xs_scr[:, 0:n].astype(BF16), cre_ref[...])
          - _dot(xs_scr[:, n:2 * n].astype(BF16), cim_ref[...]))
    yb = yb + dsk_ref[...] * u
    z = _gelu(yb)
    yb = z * _sigmoid(_dot(z.astype(BF16), wglu_ref[...]) + bglu_ref[...])
    yb = yb * lax.rsqrt(jnp.mean(yb * yb, axis=-1, keepdims=True) + EPS)
    y_ref[:, D_HGRN:D_HGRN + D_S5] = (yb * gbr_ref[:, D_HGRN:D_HGRN + D_S5]).astype(y_ref.dtype)

    yc = hs_scr[...] * _gelu(pc_ref[:, D_LRU:2 * D_LRU])
    yc = yc * lax.rsqrt(jnp.mean(yc * yc, axis=-1, keepdims=True) + EPS)
    y_ref[:, D_HGRN + D_S5:] = (yc * gbr_ref[:, D_HGRN + D_S5:]).astype(y_ref.dtype)


def _mixer(x_tm, mod_l, g_mix, w_in16, tmat, lb, s5p, lrup, g_branch):
    seq, bsz, d = x_tm.shape
    rows = CHUNK_T * bsz
    d_mix = D_HGRN + D_S5 + D_LRU
    n = N_S5_STATE
    full = lambda a: pl.BlockSpec(a.shape, lambda i, nd=a.ndim: (0,) * nd)
    consts = [mod_l, g_mix.reshape(1, d), w_in16, tmat, lb, *s5p, *lrup, g_branch]
    return pl.pallas_call(
        _mixer_body,
        grid=(seq // CHUNK_T,),
        in_specs=[pl.BlockSpec((CHUNK_T, bsz, d), lambda i: (i, 0, 0))] + [full(a) for a in consts],
        out_specs=pl.BlockSpec((rows, d_mix), lambda i: (i, 0)),
        out_shape=jax.ShapeDtypeStruct((seq * bsz, d_mix), BF16),
        scratch_shapes=[
            pltpu.VMEM((3 * D_HGRN // LANES, rows, LANES), F32),
            pltpu.VMEM((rows, D_HGRN), F32),
            pltpu.VMEM((rows, D_S5), F32),
            pltpu.VMEM((rows, 2 * D_LRU), F32),
            pltpu.VMEM((bsz, HGRN_HEADS, HGRN_DK, HGRN_DK), F32),
            pltpu.VMEM((HGRN_HEADS, rows, HGRN_DK), F32),
            pltpu.VMEM((rows, 2 * n), F32),
            pltpu.VMEM((rows, 2 * n), F32),
            pltpu.VMEM((bsz, 2 * n), F32),
            pltpu.VMEM((bsz, D_LRU), F32),
            pltpu.VMEM(((CONV_W - 1) * bsz + rows, D_LRU), F32),
            pltpu.VMEM((rows, D_LRU), F32),
            pltpu.VMEM((rows, D_LRU), F32),
            pltpu.VMEM((rows, D_LRU), F32),
        ],
        compiler_params=_params(("arbitrary",)),
    )(x_tm, *consts)


def _outproj_body(x_ref, y_ref, mod_ref, g_ref, wo_ref, wq_ref, x1_ref, h2t_ref, q_ref):
    tl, bsz, d = x_ref.shape
    part = tl // ROW_PARTS
    parts = range(ROW_PARTS)
    rows = [slice(s * part * bsz, (s + 1) * part * bsz) for s in parts]
    o = [_dot(y_ref[rows[s], :], wo_ref[...]).reshape(part, bsz, d) for s in parts]
    x1 = [x_ref[s * part:(s + 1) * part] + mod_ref[2] * o[s] for s in parts]
    h2 = []
    for s in parts:
        x1_ref[s * part:(s + 1) * part] = x1[s]
        ms = jnp.mean(x1[s] * x1[s], axis=-1, keepdims=True)
        hs = x1[s] * lax.rsqrt(ms + EPS) * g_ref[...]
        hs = hs * (1.0 + mod_ref[4]) + mod_ref[3]
        h2.append(hs.reshape(part * bsz, d))
    for s in parts:
        q_ref[rows[s], :] = _dot(h2[s].astype(BF16), wq_ref[...]).astype(q_ref.dtype)
        h2t_ref[:, rows[s]] = h2[s].T.astype(BF16)


def _out_projection(x_tm, y, mod_l, g_ffn, w_out16, w_q16):
    seq, bsz, d = x_tm.shape
    tl = TOKEN_TILE // bsz
    tm = tl * bsz
    rows = seq * bsz
    nq = w_q16.shape[1]
    return pl.pallas_call(
        _outproj_body,
        grid=(seq // tl,),
        in_specs=[
            pl.BlockSpec((tl, bsz, d), lambda i: (i, 0, 0)),
            pl.BlockSpec((tm, y.shape[1]), lambda i: (i, 0)),
            pl.BlockSpec((N_MOD, bsz, d), lambda i: (0, 0, 0)),
            pl.BlockSpec((1, d), lambda i: (0, 0)),
            pl.BlockSpec(w_out16.shape, lambda i: (0, 0)),
            pl.BlockSpec(w_q16.shape, lambda i: (0, 0)),
        ],
        out_specs=[
            pl.BlockSpec((tl, bsz, d), lambda i: (i, 0, 0)),
            pl.BlockSpec((d, tm), lambda i: (0, i)),
            pl.BlockSpec((tm, nq), lambda i: (i, 0)),
        ],
        out_shape=(
            jax.ShapeDtypeStruct((seq, bsz, d), F32),
            jax.ShapeDtypeStruct((d, rows), BF16),
            jax.ShapeDtypeStruct((rows, nq), BF16),
        ),
        compiler_params=_params(("arbitrary",)),
    )(x_tm, y, mod_l, g_ffn.reshape(1, d), w_out16, w_q16)


def _sort_network(n):
    ops = []
    p = 1
    while p < n:
        k = p
        while k >= 1:
            for j in range(k % p, n - k, 2 * k):
                for i in range(min(k, n - j - k)):
                    if (i + j) // (2 * p) == (i + j + k) // (2 * p):
                        ops.append((i + j, i + j + k, True))
            k //= 2
        p *= 2
    return ops


def _bitonic_merge_network(n):
    ops = []
    j = n // 2
    while j >= 1:
        for i in range(n):
            l = i ^ j
            if l > i:
                ops.append((i, l, True))
        j //= 2
    return ops


def _apply_network(vals, ops):
    vals = list(vals)
    for i, j, first_max in ops:
        hi = jnp.maximum(vals[i], vals[j])
        lo = jnp.minimum(vals[i], vals[j])
        vals[i], vals[j] = (hi, lo) if first_max else (lo, hi)
    return vals


def _top_sorted(s, k):
    n_keys = s.shape[0]
    n_slabs = n_keys // SUBLANES
    assert n_slabs == k
    slabs = [s[i * SUBLANES:(i + 1) * SUBLANES, :] for i in range(n_slabs)]
    slabs = _apply_network(slabs, _sort_network(k))
    merge = _bitonic_merge_network(k)
    shift = SUBLANES // 2
    while shift >= 1:
        rolled = [pltpu.roll(v, shift, 0) for v in slabs]
        slabs = [jnp.maximum(slabs[i], rolled[k - 1 - i]) for i in range(k)]
        slabs = _apply_network(slabs, merge)
        shift //= 2
    return slabs


def _staircase(k):
    return [(x, y) for x in range(k) for y in range(k) if (x + 1) * (y + 1) <= k]


def _select_tree(masks, leaves):
    if not masks:
        return leaves[0]
    half = len(leaves) // 2
    return jnp.where(masks[0], _select_tree(masks[1:], leaves[half:]), _select_tree(masks[1:], leaves[:half]))


def _prefix_count(pred, vals):
    n = len(vals)
    full = pred(vals[n - 1])
    masks = []
    count = None
    stride = n // 2
    while stride >= 1:
        pivot = _select_tree(masks, [vals[base + stride - 1] for base in range(0, n, 2 * stride)])
        hit = pred(pivot)
        masks.append(hit)
        term = jnp.where(hit, float(stride), 0.0)
        count = term if count is None else count + term
        stride //= 2
    return jnp.where(full, float(n), count)


def _retrieval_body(q_ref, keys_ref, r1_ref, e1_ref, n0_ref, c0_ref, s_scr, a_scr, b_scr,
                    afull_scr, bfull_scr):
    k = PEER_TOPK
    half = PEER_DQ // 2
    tok = q_ref.shape[0]
    neg = jnp.full((SUBLANES, tok), -jnp.inf, F32)

    for h in range(PEER_HEADS):
        for p in range(2):
            off = h * PEER_DQ + p * half
            s = _dot_nt(keys_ref[p], q_ref[:, off:off + half])
            s_scr[h, p] = s
            top = _top_sorted(s, k)
            dst, dst_full = (a_scr, afull_scr) if p == 0 else (b_scr, bfull_scr)
            for x in range(k):
                dst[x, h:h + 1, :] = top[x][0:1, :]
                dst_full[h, x] = top[x]

    a = [a_scr[x] for x in range(k)]
    b = [b_scr[y] for y in range(k)]
    pairs = _staircase(k)
    cands = [a[x] + b[y] for x, y in pairs]
    n_sort = 1
    while n_sort < len(cands):
        n_sort *= 2
    ranked = _apply_network(cands + [neg] * (n_sort - len(cands)), _sort_network(n_sort))
    tau = ranked[k - 1]
    best = a[0] + b[0]
    z = jnp.zeros_like(tau)
    for cnd in cands:
        z = z + jnp.where(cnd >= tau, jnp.exp(cnd - best), 0.0)
    inv_z = 1.0 / z

    pack = 2 * SUBLANES
    for h in range(PEER_HEADS):
        tau_h = jnp.broadcast_to(tau[h:h + 1, :], (SUBLANES, tok))
        inv_z_h = jnp.broadcast_to(inv_z[h:h + 1, :], (SUBLANES, tok))
        b_h = [bfull_scr[h, y] for y in range(k)]
        a_first = afull_scr[h, 0]
        a_last = afull_scr[h, k - 1]
        for blk in range(N_KEYS // pack):
            ranks, exps = [], []
            for sub in range(pack // SUBLANES):
                rows = slice(blk * pack + sub * SUBLANES, blk * pack + (sub + 1) * SUBLANES)
                s0 = s_scr[h, 0, rows, :]
                s1 = s_scr[h, 1, rows, :]
                ranks.append(_prefix_count(lambda v: v > s1, b_h))
                cnt = _prefix_count(lambda v: s0 + v >= tau_h, b_h)
                n0_ref[h, rows, :] = jnp.where(s0 >= a_last, cnt, 0.0)
                c0_ref[h, rows, :] = jnp.exp(s0 - a_first) * inv_z_h
                exps.append(jnp.exp(s1 - b_h[0]))
            prow = slice(blk * pack, (blk + 1) * pack)
            r1_ref[h, prow, :] = jnp.concatenate(ranks, axis=0).astype(r1_ref.dtype)
            e1_ref[h, prow, :] = jnp.concatenate(exps, axis=0).astype(e1_ref.dtype)


def _retrieval(q, keys16):
    rows = q.shape[0]
    tok = RETR_TILE
    tab = lambda dt: jax.ShapeDtypeStruct((PEER_HEADS, N_KEYS, rows), dt)
    tspec = pl.BlockSpec((PEER_HEADS, N_KEYS, tok), lambda i: (0, 0, i))
    return pl.pallas_call(
        _retrieval_body,
        grid=(rows // tok,),
        in_specs=[
            pl.BlockSpec((tok, q.shape[1]), lambda i: (i, 0)),
            pl.BlockSpec(keys16.shape, lambda i: (0, 0, 0)),
        ],
        out_specs=[tspec, tspec, tspec, tspec],
        out_shape=(tab(BF16), tab(BF16), tab(F32), tab(F32)),
        scratch_shapes=[
            pltpu.VMEM((PEER_HEADS, 2, N_KEYS, tok), F32),
            pltpu.VMEM((PEER_TOPK, SUBLANES, tok), F32),
            pltpu.VMEM((PEER_TOPK, SUBLANES, tok), F32),
            pltpu.VMEM((PEER_HEADS, PEER_TOPK, SUBLANES, tok), F32),
            pltpu.VMEM((PEER_HEADS, PEER_TOPK, SUBLANES, tok), F32),
        ],
        compiler_params=_params(("arbitrary",)),
    )(q, keys16)


def _peer_body(h2t_ref, r1_ref, e1_ref, n0_ref, c0_ref, u_ref, vt_ref, x1_ref, mod_ref,
               x2_ref, acc_scr, a_scr, p_scr):
    j = pl.program_id(1)
    n_j = pl.num_programs(1)
    te = u_ref.shape[0]
    tok = h2t_ref.shape[1]
    rows_per_step = te // N_KEYS
    pack = 2 * SUBLANES

    @pl.when(j == 0)
    def _():
        acc_scr[...] = jnp.zeros_like(acc_scr)
        a_scr[...] = _dot(u_ref[...], h2t_ref[...]).astype(BF16)

    @pl.when(j > 0)
    def _():
        zero = jnp.zeros((pack, PEER_PIECE), BF16)
        for c in range(tok // PEER_PIECE):
            cols = slice(c * PEER_PIECE, (c + 1) * PEER_PIECE)
            for r in range(rows_per_step):
                gates = [zero for _ in range(N_KEYS // pack)]
                for h in range(PEER_HEADS):
                    nb = jnp.broadcast_to(n0_ref[h, r:r + 1, cols], (pack, PEER_PIECE)).astype(BF16)
                    cb = jnp.broadcast_to(c0_ref[h, r:r + 1, cols], (pack, PEER_PIECE)).astype(BF16)
                    for k in range(N_KEYS // pack):
                        rs = slice(k * pack, (k + 1) * pack)
                        sel = jnp.where(r1_ref[h, rs, cols] < nb, e1_ref[h, rs, cols], zero)
                        gates[k] = gates[k] + sel * cb
                for k in range(N_KEYS // pack):
                    rs = slice(r * N_KEYS + k * pack, r * N_KEYS + (k + 1) * pack)
                    p_scr[rs, cols] = gates[k] * _gelu(a_scr[rs, cols])
            acc_scr[:, cols] += _dot(vt_ref[...], p_scr[:, cols])
            a_scr[:, cols] = _dot(u_ref[...], h2t_ref[:, cols]).astype(BF16)

    @pl.when(j == n_j - 1)
    def _():
        tl, bsz, d = x1_ref.shape
        out = acc_scr[...].T.reshape(tl, bsz, d)
        x2_ref[...] = x1_ref[...] + mod_ref[5] * out


def _peer_dense(h2t, tables, u16, vt16, x1, mod_l):
    seq, bsz, d = x1.shape
    tok = PEER_TOKEN_TILE
    tl = tok // bsz
    te = EXPERT_TILE
    n_tiles = u16.shape[0] // te
    rows_per_step = te // N_KEYS
    cur = lambda j: jnp.minimum(j, n_tiles - 1)
    prev = lambda j: jnp.maximum(j - 1, 0)
    tspec = pl.BlockSpec((PEER_HEADS, N_KEYS, tok), lambda i, j: (0, 0, i))
    rspec = pl.BlockSpec((PEER_HEADS, rows_per_step, tok), lambda i, j: (0, prev(j), i))
    return pl.pallas_call(
        _peer_body,
        grid=(seq // tl, n_tiles + 1),
        in_specs=[
            pl.BlockSpec((d, tok), lambda i, j: (0, i)),
            tspec, tspec, rspec, rspec,
            pl.BlockSpec((te, d), lambda i, j: (cur(j), 0)),
            pl.BlockSpec((d, te), lambda i, j: (0, prev(j))),
            pl.BlockSpec((tl, bsz, d), lambda i, j: (i, 0, 0)),
            pl.BlockSpec((N_MOD, bsz, d), lambda i, j: (0, 0, 0)),
        ],
        out_specs=pl.BlockSpec((tl, bsz, d), lambda i, j: (i, 0, 0)),
        out_shape=jax.ShapeDtypeStruct((seq, bsz, d), F32),
        scratch_shapes=[
            pltpu.VMEM((d, tok), F32),
            pltpu.VMEM((te, tok), BF16),
            pltpu.VMEM((te, tok), BF16),
        ],
        compiler_params=_params(("arbitrary", "arbitrary")),
    )(h2t, *tables, u16, vt16, x1, mod_l)


def _final_body(x_ref, g_ref, o_ref):
    x = x_ref[...]
    o_ref[...] = x * lax.rsqrt(jnp.mean(x * x, axis=-1, keepdims=True) + EPS) * g_ref[...]


def _final_norm(x_tm, g):
    seq, bsz, d = x_tm.shape
    tl = TOKEN_TILE // bsz
    return pl.pallas_call(
        _final_body,
        grid=(seq // tl,),
        in_specs=[pl.BlockSpec((tl, bsz, d), lambda i: (i, 0, 0)),
                  pl.BlockSpec((1, d), lambda i: (0, 0))],
        out_specs=pl.BlockSpec((tl, bsz, d), lambda i: (i, 0, 0)),
        out_shape=jax.ShapeDtypeStruct((seq, bsz, d), F32),
        compiler_params=_params(("arbitrary",)),
    )(x_tm, g.reshape(1, d))


def _block_diag(blocks):
    n, r, c = blocks.shape
    eye = jnp.eye(n, dtype=blocks.dtype)
    return (blocks[:, :, None, :] * eye[:, None, :, None]).reshape(n * r, n * c)


def kernel(x, c, w_mod, b_mod, g_mix, w_in, hgrn_lb_logits, s5_a_re, s5_a_im, s5_b_re, s5_b_im, s5_c_re, s5_c_im, s5_log_dt, s5_d, s5_w_glu, s5_b_glu, lru_conv_w, lru_conv_b, lru_w_a, lru_b_a, lru_w_x, lru_b_x, lru_lambda, g_branch, w_out, g_ffn, peer_w_q, peer_sub_keys, peer_u, peer_v, g_final):
    bsz, seq, d = x.shape
    depth = w_in.shape[0]
    assert bsz == SUBLANES and seq % CHUNK_T == 0 and (seq * bsz) % TOKEN_TILE == 0

    x_tm = x.transpose(1, 0, 2)
    mod = _modulation(c, w_mod, b_mod)
    mod = mod.reshape(depth, bsz, N_MOD, d).transpose(0, 2, 1, 3)
    lb_all, ab_re, ab_im, bb_re, bb_im = _prepare_params(
        hgrn_lb_logits, s5_a_re, s5_a_im, s5_log_dt, s5_b_re, s5_b_im)
    tmat = jnp.asarray(_hgrn_exponent_matrix(CHUNK_T), BF16)
    row = lambda a: a.reshape(1, -1)

    for l in range(depth):
        bbd = jnp.concatenate([
            _block_diag(bb_re[l].reshape(S5_GROUP, S5_GROUPS, S5_STATE).transpose(1, 0, 2)),
            _block_diag(bb_im[l].reshape(S5_GROUP, S5_GROUPS, S5_STATE).transpose(1, 0, 2)),
        ], axis=1).astype(BF16)
        cre = _block_diag(s5_c_re[l].transpose(0, 2, 1)).astype(BF16)
        cim = _block_diag(s5_c_im[l].transpose(0, 2, 1)).astype(BF16)
        s5p = (ab_re[l], ab_im[l], bbd, cre, cim, row(s5_d[l]),
               s5_w_glu[l].astype(BF16), row(s5_b_glu[l]))
        lrup = (lru_conv_w[l], row(lru_conv_b[l]),
                _block_diag(lru_w_a[l]).astype(BF16), row(lru_b_a[l]),
                _block_diag(lru_w_x[l]).astype(BF16), row(lru_b_x[l]), row(lru_lambda[l]))

        y = _mixer(x_tm, mod[l], g_mix[l], w_in[l].astype(BF16), tmat, lb_all[l:l + 1],
                   s5p, lrup, row(g_branch[l]))
        x1, h2t, q = _out_projection(x_tm, y, mod[l], g_ffn[l],
                                    w_out[l].astype(BF16), peer_w_q[l].astype(BF16))
        tables = _retrieval(q, peer_sub_keys[l].astype(BF16))
        x_tm = _peer_dense(h2t, tables, peer_u[l].astype(BF16),
                           peer_v[l].astype(BF16).T, x1, mod[l])

    out = _final_norm(x_tm, g_final)
    return out.transpose(1, 0, 2)
```

```python
import functools

import numpy as np
import jax
import jax.numpy as jnp
from jax import lax
from jax.experimental import pallas as pl
from jax.experimental.pallas import tpu as pltpu

F32 = jnp.float32
BF16 = jnp.bfloat16

EPS = 1e-6
N_MOD = 6
HGRN_HEADS = 4
HGRN_DK = 128
D_HGRN = HGRN_HEADS * HGRN_DK
S5_GROUP = 16
S5_GROUPS = 16
S5_STATE = 64
D_S5 = S5_GROUP * S5_GROUPS
N_S5_STATE = S5_GROUPS * S5_STATE
LRU_BLOCKS = 4
LRU_BLOCK = 64
D_LRU = LRU_BLOCKS * LRU_BLOCK
CONV_W = 4
LRU_C = 8.0
PEER_HEADS = 8
N_KEYS = 128
PEER_DQ = 256
PEER_TOPK = 16

SUBLANES = 8
LANES = 128
VMEM_LIMIT_BYTES = 56 * 1024 * 1024

CHUNK_T = 64
HGRN_GROUP = 8
TOKEN_TILE = 512
ROW_PARTS = 2
EXPERT_TILE = 1024
PEER_TOKEN_TILE = 1024
PEER_PIECE = 256
RETR_TILE = 128
MOD_COL_TILE = 1536


def _sigmoid(x):
    return 1.0 / (1.0 + jnp.exp(-x))


def _silu(x):
    return x * _sigmoid(x)


def _gelu(x):
    return 0.5 * x * (1.0 + jnp.tanh(0.7978845608028654 * (x + 0.044715 * (x * x * x))))


def _dot(a, b):
    return jnp.dot(a, b, preferred_element_type=F32)


def _dot_nt(a, b):
    return lax.dot_general(a, b, (((1,), (1,)), ((), ())), preferred_element_type=F32)


def _dot_tn(a, b):
    return lax.dot_general(a, b, (((0,), (0,)), ((), ())), preferred_element_type=F32)


def _split2(x):
    hi = x.astype(BF16)
    lo = (x - hi.astype(F32)).astype(BF16)
    return hi, lo


def _params(semantics):
    return pltpu.CompilerParams(dimension_semantics=semantics, vmem_limit_bytes=VMEM_LIMIT_BYTES)


def _mod_body(c_ref, w_ref, b_ref, o_ref):
    cond = _silu(c_ref[...])
    c_hi, c_lo = _split2(cond)
    w_hi, w_lo = _split2(w_ref[0])
    o_ref[0] = _dot(c_hi, w_hi) + _dot(c_hi, w_lo) + _dot(c_lo, w_hi) + b_ref[0]


def _modulation(c, w_mod, b_mod):
    depth, d, n = w_mod.shape
    bsz = c.shape[0]
    tn = MOD_COL_TILE
    return pl.pallas_call(
        _mod_body,
        grid=(depth, n // tn),
        in_specs=[
            pl.BlockSpec((bsz, d), lambda l, j: (0, 0)),
            pl.BlockSpec((1, d, tn), lambda l, j: (l, 0, j)),
            pl.BlockSpec((1, 1, tn), lambda l, j: (l, 0, j)),
        ],
        out_specs=pl.BlockSpec((1, bsz, tn), lambda l, j: (l, 0, j)),
        out_shape=jax.ShapeDtypeStruct((depth, bsz, n), F32),
        compiler_params=_params(("arbitrary", "arbitrary")),
    )(c, w_mod, b_mod.reshape(depth, 1, n))


def _prep_body(lbl_ref, are_ref, aim_ref, ldt_ref, bre_ref, bim_ref,
               lb_ref, abr_ref, abi_ref, bbr_ref, bbi_ref):
    logits = lbl_ref[...]
    depth = logits.shape[0]
    rows = [logits[l:l + 1] for l in range(depth)]
    mx = functools.reduce(jnp.maximum, rows)
    ex = [jnp.exp(r - mx) for r in rows]
    den = functools.reduce(lambda a, b: a + b, ex)
    soft = [e / den for e in ex]
    run = soft[0]
    for l in range(depth):
        if l > 0:
            run = run + soft[l]
        lb_ref[l:l + 1, :] = run - soft[0]

    lam_re = jnp.minimum(are_ref[...], -1e-4)
    lam_im = aim_ref[...]
    dt = jnp.exp(ldt_ref[...])
    mag = jnp.exp(lam_re * dt)
    ab_re = mag * jnp.cos(lam_im * dt)
    ab_im = mag * jnp.sin(lam_im * dt)
    den2 = lam_re * lam_re + lam_im * lam_im
    nr = ab_re - 1.0
    z_re = (nr * lam_re + ab_im * lam_im) / den2
    z_im = (ab_im * lam_re - nr * lam_im) / den2
    abr_ref[...] = ab_re
    abi_ref[...] = ab_im
    br = bre_ref[...]
    bi = bim_ref[...]
    bbr_ref[...] = z_re * br - z_im * bi
    bbi_ref[...] = z_re * bi + z_im * br


def _prepare_params(lb_logits, a_re, a_im, log_dt, b_re, b_im):
    depth = lb_logits.shape[0]
    n = N_S5_STATE
    flat = lambda a: a.reshape(depth, 1, n)
    ldt = jnp.broadcast_to(log_dt[:, :, None], (depth, S5_GROUPS, S5_STATE))
    bt = lambda b: b.transpose(0, 3, 1, 2).reshape(depth, S5_GROUP, n)
    out_shape = (
        jax.ShapeDtypeStruct((depth, D_HGRN), F32),
        jax.ShapeDtypeStruct((depth, 1, n), F32),
        jax.ShapeDtypeStruct((depth, 1, n), F32),
        jax.ShapeDtypeStruct((depth, S5_GROUP, n), F32),
        jax.ShapeDtypeStruct((depth, S5_GROUP, n), F32),
    )
    return pl.pallas_call(_prep_body, out_shape=out_shape)(
        lb_logits, flat(a_re), flat(a_im), flat(ldt), bt(b_re), bt(b_im))


def _project_inputs(x_ref, mod_ref, g_ref, w_ref, pa_ref, pg_ref, pb_ref, pc_ref):
    tl, bsz, d = x_ref.shape
    na = pa_ref.shape[0] * LANES
    ng = pg_ref.shape[1]
    nb = pb_ref.shape[1]
    part = tl // ROW_PARTS
    for s in range(ROW_PARTS):
        x = x_ref[s * part:(s + 1) * part]
        ms = jnp.mean(x * x, axis=-1, keepdims=True)
        xn = x * lax.rsqrt(ms + EPS) * g_ref[...]
        h = xn * (1.0 + mod_ref[1]) + mod_ref[0]
        h = h.reshape(part * bsz, d).astype(BF16)
        rows = slice(s * part * bsz, (s + 1) * part * bsz)
        pa = _dot(h, w_ref[:, 0:na])
        for cg in range(pa_ref.shape[0]):
            pa_ref[cg, rows, :] = pa[:, cg * LANES:(cg + 1) * LANES]
        pg_ref[rows, :] = _dot(h, w_ref[:, na:na + ng])
        pb_ref[rows, :] = _dot(h, w_ref[:, na + ng:na + ng + nb])
        pc_ref[rows, :] = _dot(h, w_ref[:, na + ng + nb:])


def _hgrn_exponent_matrix(chunk):
    n_levels = int(np.log2(chunk))
    mat = np.zeros((n_levels + 2, chunk, chunk), np.float32)
    u = np.arange(chunk)
    for k in range(n_levels):
        g = chunk >> (k + 1)
        m = 2 * g
        for t in range(chunk):
            r = (t // m) * m + g - 1
            if t % m >= g:
                mat[k, t] = (u > r) & (u <= t)
            else:
                mat[k, t] = (u > t) & (u <= r)
    for t in range(chunk):
        mat[n_levels, t] = u <= t
        mat[n_levels + 1, t] = u > t
    return mat.reshape((n_levels + 2) * chunk, chunk)


def _mixer_body(x_ref, mod_ref, g_ref, w_ref, tmat_ref, lb_ref,
                ar_ref, ai_ref, bbd_ref, cre_ref, cim_ref, dsk_ref, wglu_ref, bglu_ref,
                cw_ref, cb_ref, wa_ref, ba_ref, wx_ref, bx_ref, lam_ref, gbr_ref,
                y_ref,
                pa_ref, pg_ref, pb_ref, pc_ref,
                st_ref, o_scr, bu_scr, xs_scr, s5_state, lru_state, xext, la_scr, lb_scr, hs_scr):
    _project_inputs(x_ref, mod_ref, g_ref, w_ref, pa_ref, pg_ref, pb_ref, pc_ref)
    rows = pa_ref.shape[1]
    bsz = s5_state.shape[0]
    chunk = rows // bsz
    n_levels = tmat_ref.shape[0] // chunk - 2
    dk = HGRN_DK

    @pl.when(pl.program_id(0) == 0)
    def _():
        st_ref[...] = jnp.zeros_like(st_ref)
        s5_state[...] = jnp.zeros_like(s5_state)
        lru_state[...] = jnp.zeros_like(lru_state)
        xext[0:(CONV_W - 1) * bsz, :] = jnp.zeros(((CONV_W - 1) * bsz, D_LRU), F32)

    ti = lax.broadcasted_iota(jnp.int32, (chunk, chunk), 0)
    si = lax.broadcasted_iota(jnp.int32, (chunk, chunk), 1)
    masks = []
    for k in range(n_levels):
        g = chunk >> (k + 1)
        m = 2 * g
        same = (ti // m) == (si // m)
        masks.append(jnp.where(same, jnp.where((ti % m) >= g, jnp.where((si % m) < g, 1.0, 0.0), 0.0), 0.0))
    diag = jnp.where(ti == si, 1.0, 0.0)
    lb = lb_ref[...]
    tmat = tmat_ref[...]

    def hgrn_group(g, carry):
        grp = range(HGRN_GROUP)
        bs = [g * HGRN_GROUP + e for e in grp]
        rws = [pl.ds(b, chunk, stride=bsz) for b in bs]
        heads = lambda e, first: jnp.concatenate(
            [pa_ref[first + h, rws[e], :] for h in range(HGRN_HEADS)], axis=1)
        qa = [_silu(heads(e, 0)) for e in grp]
        f = [lb + (1.0 - lb) * _sigmoid(heads(e, HGRN_HEADS)) for e in grp]
        v16 = [heads(e, 2 * HGRN_HEADS).astype(BF16) for e in grp]
        kk = [1.0 - f[e] for e in grp]
        parts = [_split2(jnp.log(f[e])) for e in grp]
        decay = [jnp.exp(_dot(tmat, parts[e][0]) + _dot(tmat, parts[e][1])) for e in grp]
        cum = [decay[e][n_levels * chunk:(n_levels + 1) * chunk] for e in grp]
        qb16 = [(qa[e] * cum[e]).astype(BF16) for e in grp]
        ks16 = [(kk[e] * decay[e][(n_levels + 1) * chunk:]).astype(BF16) for e in grp]
        total = [cum[e][chunk - 1:chunk, :] for e in grp]
        qa16 = [qa[e].astype(BF16) for e in grp]
        kk16 = [kk[e].astype(BF16) for e in grp]
        for h in range(HGRN_HEADS):
            sl = slice(h * dk, (h + 1) * dk)
            scores = [diag * _dot_nt(qa16[e][:, sl], kk16[e][:, sl]) for e in grp]
            for k in range(n_levels):
                for e in grp:
                    dl = decay[e][k * chunk:(k + 1) * chunk, sl]
                    ql = (qa[e][:, sl] * dl).astype(BF16)
                    kl = (kk[e][:, sl] * dl).astype(BF16)
                    scores[e] = scores[e] + masks[k] * _dot_nt(ql, kl)
            for e in grp:
                st = st_ref[bs[e], h]
                o = (_dot(scores[e].astype(BF16), v16[e][:, sl])
                     + _dot_nt(qb16[e][:, sl], st.astype(BF16)))
                o_scr[h, rws[e], :] = o
                st_ref[bs[e], h] = st * total[e][:, sl] + _dot_tn(v16[e][:, sl], ks16[e][:, sl])
        return carry

    lax.fori_loop(0, bsz // HGRN_GROUP, hgrn_group, 0)

    gate = _silu(pg_ref[...])
    ya = []
    ssq = None
    for h in range(HGRN_HEADS):
        sl = slice(h * dk, (h + 1) * dk)
        oh = o_scr[h]
        oh = oh * lax.rsqrt(jnp.mean(oh * oh, axis=-1, keepdims=True) + EPS) * gate[:, sl]
        ya.append(oh)
        s = jnp.sum(oh * oh, axis=-1, keepdims=True)
        ssq = s if ssq is None else ssq + s
    scale = lax.rsqrt(ssq * (1.0 / D_HGRN) + EPS)
    for h in range(HGRN_HEADS):
        sl = slice(h * dk, (h + 1) * dk)
        y_ref[:, sl] = (ya[h] * scale * gbr_ref[:, sl]).astype(y_ref.dtype)

    n = N_S5_STATE
    u = pb_ref[...]
    bu_scr[...] = _dot(u.astype(BF16), bbd_ref[...])
    ar = jnp.broadcast_to(ar_ref[...], (bsz, n))
    ai = jnp.broadcast_to(ai_ref[...], (bsz, n))

    tail = (CONV_W - 1) * bsz
    xin = pc_ref[:, 0:D_LRU]
    xext[tail:tail + rows, :] = xin
    xc = cb_ref[...]
    for w in range(CONV_W):
        xc = xc + cw_ref[w:w + 1, :] * xext[w * bsz:w * bsz + rows, :]
    xext[0:tail, :] = xin[rows - tail:rows, :]
    xc16 = xc.astype(BF16)
    rg = _sigmoid(_dot(xc16, wa_ref[...]) + ba_ref[...])
    ig = _sigmoid(_dot(xc16, wx_ref[...]) + bx_ref[...])
    nl = -lam_ref[...]
    softplus = jnp.maximum(nl, 0.0) + jnp.log(1.0 + jnp.exp(-jnp.abs(nl)))
    log_a = (-LRU_C) * rg * softplus
    la_scr[...] = jnp.exp(log_a)
    lb_scr[...] = jnp.sqrt(1.0 - jnp.exp(2.0 * log_a)) * (ig * xc)

    def scan_step(t, st):
        xr, xi, hstate = st
        r = pl.ds(pl.multiple_of(t * bsz, bsz), bsz)
        nxr = ar * xr - ai * xi + bu_scr[r, 0:n]
        nxi = ar * xi + ai * xr + bu_scr[r, n:2 * n]
        xs_scr[r, 0:n] = nxr
        xs_scr[r, n:2 * n] = nxi
        hn = la_scr[r, :] * hstate + lb_scr[r, :]
        hs_scr[r, :] = hn
        return nxr, nxi, hn

    xr, xi, hlast = lax.fori_loop(
        0, chunk, scan_step, (s5_state[:, 0:n], s5_state[:, n:2 * n], lru_state[...]))
    s5_state[:, 0:n] = xr
    s5_state[:, n:2 * n] = xi
    lru_state[...] = hlast

    yb = (_dot(xs_scr[:, 0:n].astype(BF16), cre_ref[...])
          - _dot(xs_scr[:, n:2 * n].astype(BF16), cim_ref[...]))
    yb = yb + dsk_ref[...] * u
    z = _gelu(yb)
    yb = z * _sigmoid(_dot(z.astype(BF16), wglu_ref[...]) + bglu_ref[...])
    yb = yb * lax.rsqrt(jnp.mean(yb * yb, axis=-1, keepdims=True) + EPS)
    y_ref[:, D_HGRN:D_HGRN + D_S5] = (yb * gbr_ref[:, D_HGRN:D_HGRN + D_S5]).astype(y_ref.dtype)

    yc = hs_scr[...] * _gelu(pc_ref[:, D_LRU:2 * D_LRU])
    yc = yc * lax.rsqrt(jnp.mean(yc * yc, axis=-1, keepdims=True) + EPS)
    y_ref[:, D_HGRN + D_S5:] = (yc * gbr_ref[:, D_HGRN + D_S5:]).astype(y_ref.dtype)


def _mixer(x_tm, mod_l, g_mix, w_in16, tmat, lb, s5p, lrup, g_branch):
    seq, bsz, d = x_tm.shape
    rows = CHUNK_T * bsz
    d_mix = D_HGRN + D_S5 + D_LRU
    n = N_S5_STATE
    full = lambda a: pl.BlockSpec(a.shape, lambda i, nd=a.ndim: (0,) * nd)
    consts = [mod_l, g_mix.reshape(1, d), w_in16, tmat, lb, *s5p, *lrup, g_branch]
    return pl.pallas_call(
        _mixer_body,
        grid=(seq // CHUNK_T,),
        in_specs=[pl.BlockSpec((CHUNK_T, bsz, d), lambda i: (i, 0, 0))] + [full(a) for a in consts],
        out_specs=pl.BlockSpec((rows, d_mix), lambda i: (i, 0)),
        out_shape=jax.ShapeDtypeStruct((seq * bsz, d_mix), BF16),
        scratch_shapes=[
            pltpu.VMEM((3 * D_HGRN // LANES, rows, LANES), F32),
            pltpu.VMEM((rows, D_HGRN), F32),
            pltpu.VMEM((rows, D_S5), F32),
            pltpu.VMEM((rows, 2 * D_LRU), F32),
            pltpu.VMEM((bsz, HGRN_HEADS, HGRN_DK, HGRN_DK), F32),
            pltpu.VMEM((HGRN_HEADS, rows, HGRN_DK), F32),
            pltpu.VMEM((rows, 2 * n), F32),
            pltpu.VMEM((rows, 2 * n), F32),
            pltpu.VMEM((bsz, 2 * n), F32),
            pltpu.VMEM((bsz, D_LRU), F32),
            pltpu.VMEM(((CONV_W - 1) * bsz + rows, D_LRU), F32),
            pltpu.VMEM((rows, D_LRU), F32),
            pltpu.VMEM((rows, D_LRU), F32),
            pltpu.VMEM((rows, D_LRU), F32),
        ],
        compiler_params=_params(("arbitrary",)),
    )(x_tm, *consts)


def _outproj_body(x_ref, y_ref, mod_ref, g_ref, wo_ref, wq_ref, x1_ref, h2t_ref, q_ref):
    tl, bsz, d = x_ref.shape
    part = tl // ROW_PARTS
    parts = range(ROW_PARTS)
    rows = [slice(s * part * bsz, (s + 1) * part * bsz) for s in parts]
    o = [_dot(y_ref[rows[s], :], wo_ref[...]).reshape(part, bsz, d) for s in parts]
    x1 = [x_ref[s * part:(s + 1) * part] + mod_ref[2] * o[s] for s in parts]
    h2 = []
    for s in parts:
        x1_ref[s * part:(s + 1) * part] = x1[s]
        ms = jnp.mean(x1[s] * x1[s], axis=-1, keepdims=True)
        hs = x1[s] * lax.rsqrt(ms + EPS) * g_ref[...]
        hs = hs * (1.0 + mod_ref[4]) + mod_ref[3]
        h2.append(hs.reshape(part * bsz, d))
    for s in parts:
        q_ref[rows[s], :] = _dot(h2[s].astype(BF16), wq_ref[...]).astype(q_ref.dtype)
        h2t_ref[:, rows[s]] = h2[s].T.astype(BF16)


def _out_projection(x_tm, y, mod_l, g_ffn, w_out16, w_q16):
    seq, bsz, d = x_tm.shape
    tl = TOKEN_TILE // bsz
    tm = tl * bsz
    rows = seq * bsz
    nq = w_q16.shape[1]
    return pl.pallas_call(
        _outproj_body,
        grid=(seq // tl,),
        in_specs=[
            pl.BlockSpec((tl, bsz, d), lambda i: (i, 0, 0)),
            pl.BlockSpec((tm, y.shape[1]), lambda i: (i, 0)),
            pl.BlockSpec((N_MOD, bsz, d), lambda i: (0, 0, 0)),
            pl.BlockSpec((1, d), lambda i: (0, 0)),
            pl.BlockSpec(w_out16.shape, lambda i: (0, 0)),
            pl.BlockSpec(w_q16.shape, lambda i: (0, 0)),
        ],
        out_specs=[
            pl.BlockSpec((tl, bsz, d), lambda i: (i, 0, 0)),
            pl.BlockSpec((d, tm), lambda i: (0, i)),
            pl.BlockSpec((tm, nq), lambda i: (i, 0)),
        ],
        out_shape=(
            jax.ShapeDtypeStruct((seq, bsz, d), F32),
            jax.ShapeDtypeStruct((d, rows), BF16),
            jax.ShapeDtypeStruct((rows, nq), BF16),
        ),
        compiler_params=_params(("arbitrary",)),
    )(x_tm, y, mod_l, g_ffn.reshape(1, d), w_out16, w_q16)


def _sort_network(n):
    ops = []
    p = 1
    while p < n:
        k = p
        while k >= 1:
            for j in range(k % p, n - k, 2 * k):
                for i in range(min(k, n - j - k)):
                    if (i + j) // (2 * p) == (i + j + k) // (2 * p):
                        ops.append((i + j, i + j + k, True))
            k //= 2
        p *= 2
    return ops


def _bitonic_merge_network(n):
    ops = []
    j = n // 2
    while j >= 1:
        for i in range(n):
            l = i ^ j
            if l > i:
                ops.append((i, l, True))
        j //= 2
    return ops


def _apply_network(vals, ops):
    vals = list(vals)
    for i, j, first_max in ops:
        hi = jnp.maximum(vals[i], vals[j])
        lo = jnp.minimum(vals[i], vals[j])
        vals[i], vals[j] = (hi, lo) if first_max else (lo, hi)
    return vals


def _top_sorted(s, k):
    n_keys = s.shape[0]
    n_slabs = n_keys // SUBLANES
    assert n_slabs == k
    slabs = [s[i * SUBLANES:(i + 1) * SUBLANES, :] for i in range(n_slabs)]
    slabs = _apply_network(slabs, _sort_network(k))
    merge = _bitonic_merge_network(k)
    shift = SUBLANES // 2
    while shift >= 1:
        rolled = [pltpu.roll(v, shift, 0) for v in slabs]
        slabs = [jnp.maximum(slabs[i], rolled[k - 1 - i]) for i in range(k)]
        slabs = _apply_network(slabs, merge)
        shift //= 2
    return slabs


def _staircase(k):
    return [(x, y) for x in range(k) for y in range(k) if (x + 1) * (y + 1) <= k]


def _select_tree(masks, leaves):
    if not masks:
        return leaves[0]
    half = len(leaves) // 2
    return jnp.where(masks[0], _select_tree(masks[1:], leaves[half:]), _select_tree(masks[1:], leaves[:half]))


def _prefix_count(pred, vals):
    n = len(vals)
    full = pred(vals[n - 1])
    masks = []
    count = None
    stride = n // 2
    while stride >= 1:
        pivot = _select_tree(masks, [vals[base + stride - 1] for base in range(0, n, 2 * stride)])
        hit = pred(pivot)
        masks.append(hit)
        term = jnp.where(hit, float(stride), 0.0)
        count = term if count is None else count + term
        stride //= 2
    return jnp.where(full, float(n), count)


def _retrieval_body(q_ref, keys_ref, r1_ref, e1_ref, n0_ref, c0_ref, s_scr, a_scr, b_scr,
                    afull_scr, bfull_scr):
    k = PEER_TOPK
    half = PEER_DQ // 2
    tok = q_ref.shape[0]
    neg = jnp.full((SUBLANES, tok), -jnp.inf, F32)

    for h in range(PEER_HEADS):
        for p in range(2):
            off = h * PEER_DQ + p * half
            s = _dot_nt(keys_ref[p], q_ref[:, off:off + half])
            s_scr[h, p] = s
            top = _top_sorted(s, k)
            dst, dst_full = (a_scr, afull_scr) if p == 0 else (b_scr, bfull_scr)
            for x in range(k):
                dst[x, h:h + 1, :] = top[x][0:1, :]
                dst_full[h, x] = top[x]

    a = [a_scr[x] for x in range(k)]
    b = [b_scr[y] for y in range(k)]
    pairs = _staircase(k)
    cands = [a[x] + b[y] for x, y in pairs]
    n_sort = 1
    while n_sort < len(cands):
        n_sort *= 2
    ranked = _apply_network(cands + [neg] * (n_sort - len(cands)), _sort_network(n_sort))
    tau = ranked[k - 1]
    best = a[0] + b[0]
    z = jnp.zeros_like(tau)
    for cnd in cands:
        z = z + jnp.where(cnd >= tau, jnp.exp(cnd - best), 0.0)
    inv_z = 1.0 / z

    pack = 2 * SUBLANES
    for h in range(PEER_HEADS):
        tau_h = jnp.broadcast_to(tau[h:h + 1, :], (SUBLANES, tok))
        inv_z_h = jnp.broadcast_to(inv_z[h:h + 1, :], (SUBLANES, tok))
        b_h = [bfull_scr[h, y] for y in range(k)]
        a_first = afull_scr[h, 0]
        a_last = afull_scr[h, k - 1]
        for blk in range(N_KEYS // pack):
            ranks, exps = [], []
            for sub in range(pack // SUBLANES):
                rows = slice(blk * pack + sub * SUBLANES, blk * pack + (sub + 1) * SUBLANES)
                s0 = s_scr[h, 0, rows, :]
                s1 = s_scr[h, 1, rows, :]
                ranks.append(_prefix_count(lambda v: v > s1, b_h))
                cnt = _prefix_count(lambda v: s0 + v >= tau_h, b_h)
                n0_ref[h, rows, :] = jnp.where(s0 >= a_last, cnt, 0.0)
                c0_ref[h, rows, :] = jnp.exp(s0 - a_first) * inv_z_h
                exps.append(jnp.exp(s1 - b_h[0]))
            prow = slice(blk * pack, (blk + 1) * pack)
            r1_ref[h, prow, :] = jnp.concatenate(ranks, axis=0).astype(r1_ref.dtype)
            e1_ref[h, prow, :] = jnp.concatenate(exps, axis=0).astype(e1_ref.dtype)


def _retrieval(q, keys16):
    rows = q.shape[0]
    tok = RETR_TILE
    tab = lambda dt: jax.ShapeDtypeStruct((PEER_HEADS, N_KEYS, rows), dt)
    tspec = pl.BlockSpec((PEER_HEADS, N_KEYS, tok), lambda i: (0, 0, i))
    return pl.pallas_call(
        _retrieval_body,
        grid=(rows // tok,),
        in_specs=[
            pl.BlockSpec((tok, q.shape[1]), lambda i: (i, 0)),
            pl.BlockSpec(keys16.shape, lambda i: (0, 0, 0)),
        ],
        out_specs=[tspec, tspec, tspec, tspec],
        out_shape=(tab(BF16), tab(BF16), tab(F32), tab(F32)),
        scratch_shapes=[
            pltpu.VMEM((PEER_HEADS, 2, N_KEYS, tok), F32),
            pltpu.VMEM((PEER_TOPK, SUBLANES, tok), F32),
            pltpu.VMEM((PEER_TOPK, SUBLANES, tok), F32),
            pltpu.VMEM((PEER_HEADS, PEER_TOPK, SUBLANES, tok), F32),
            pltpu.VMEM((PEER_HEADS, PEER_TOPK, SUBLANES, tok), F32),
        ],
        compiler_params=_params(("arbitrary",)),
    )(q, keys16)


def _peer_body(h2t_ref, r1_ref, e1_ref, n0_ref, c0_ref, u_ref, vt_ref, x1_ref, mod_ref,
               x2_ref, acc_scr, a_scr, p_scr):
    j = pl.program_id(1)
    n_j = pl.num_programs(1)
    te = u_ref.shape[0]
    tok = h2t_ref.shape[1]
    rows_per_step = te // N_KEYS
    pack = 2 * SUBLANES

    @pl.when(j == 0)
    def _():
        acc_scr[...] = jnp.zeros_like(acc_scr)
        a_scr[...] = _dot(u_ref[...], h2t_ref[...]).astype(BF16)

    @pl.when(j > 0)
    def _():
        zero = jnp.zeros((pack, PEER_PIECE), BF16)
        for c in range(tok // PEER_PIECE):
            cols = slice(c * PEER_PIECE, (c + 1) * PEER_PIECE)
            for r in range(rows_per_step):
                gates = [zero for _ in range(N_KEYS // pack)]
                for h in range(PEER_HEADS):
                    nb = jnp.broadcast_to(n0_ref[h, r:r + 1, cols], (pack, PEER_PIECE)).astype(BF16)
                    cb = jnp.broadcast_to(c0_ref[h, r:r + 1, cols], (pack, PEER_PIECE)).astype(BF16)
                    for k in range(N_KEYS // pack):
                        rs = slice(k * pack, (k + 1) * pack)
                        sel = jnp.where(r1_ref[h, rs, cols] < nb, e1_ref[h, rs, cols], zero)
                        gates[k] = gates[k] + sel * cb
                for k in range(N_KEYS // pack):
                    rs = slice(r * N_KEYS + k * pack, r * N_KEYS + (k + 1) * pack)
                    p_scr[rs, cols] = gates[k] * _gelu(a_scr[rs, cols])
            acc_scr[:, cols] += _dot(vt_ref[...], p_scr[:, cols])
            a_scr[:, cols] = _dot(u_ref[...], h2t_ref[:, cols]).astype(BF16)

    @pl.when(j == n_j - 1)
    def _():
        tl, bsz, d = x1_ref.shape
        out = acc_scr[...].T.reshape(tl, bsz, d)
        x2_ref[...] = x1_ref[...] + mod_ref[5] * out


def _peer_dense(h2t, tables, u16, vt16, x1, mod_l):
    seq, bsz, d = x1.shape
    tok = PEER_TOKEN_TILE
    tl = tok // bsz
    te = EXPERT_TILE
    n_tiles = u16.shape[0] // te
    rows_per_step = te // N_KEYS
    cur = lambda j: jnp.minimum(j, n_tiles - 1)
    prev = lambda j: jnp.maximum(j - 1, 0)
    tspec = pl.BlockSpec((PEER_HEADS, N_KEYS, tok), lambda i, j: (0, 0, i))
    rspec = pl.BlockSpec((PEER_HEADS, rows_per_step, tok), lambda i, j: (0, prev(j), i))
    return pl.pallas_call(
        _peer_body,
        grid=(seq // tl, n_tiles + 1),
        in_specs=[
            pl.BlockSpec((d, tok), lambda i, j: (0, i)),
            tspec, tspec, rspec, rspec,
            pl.BlockSpec((te, d), lambda i, j: (cur(j), 0)),
            pl.BlockSpec((d, te), lambda i, j: (0, prev(j))),
            pl.BlockSpec((tl, bsz, d), lambda i, j: (i, 0, 0)),
            pl.BlockSpec((N_MOD, bsz, d), lambda i, j: (0, 0, 0)),
        ],
        out_specs=pl.BlockSpec((tl, bsz, d), lambda i, j: (i, 0, 0)),
        out_shape=jax.ShapeDtypeStruct((seq, bsz, d), F32),
        scratch_shapes=[
            pltpu.VMEM((d, tok), F32),
            pltpu.VMEM((te, tok), BF16),
            pltpu.VMEM((te, tok), BF16),
        ],
        compiler_params=_params(("arbitrary", "arbitrary")),
    )(h2t, *tables, u16, vt16, x1, mod_l)


def _final_body(x_ref, g_ref, o_ref):
    x = x_ref[...]
    o_ref[...] = x * lax.rsqrt(jnp.mean(x * x, axis=-1, keepdims=True) + EPS) * g_ref[...]


def _final_norm(x_tm, g):
    seq, bsz, d = x_tm.shape
    tl = TOKEN_TILE // bsz
    return pl.pallas_call(
        _final_body,
        grid=(seq // tl,),
        in_specs=[pl.BlockSpec((tl, bsz, d), lambda i: (i, 0, 0)),
                  pl.BlockSpec((1, d), lambda i: (0, 0))],
        out_specs=pl.BlockSpec((tl, bsz, d), lambda i: (i, 0, 0)),
        out_shape=jax.ShapeDtypeStruct((seq, bsz, d), F32),
        compiler_params=_params(("arbitrary",)),
    )(x_tm, g.reshape(1, d))


def _block_diag(blocks):
    n, r, c = blocks.shape
    eye = jnp.eye(n, dtype=blocks.dtype)
    return (blocks[:, :, None, :] * eye[:, None, :, None]).reshape(n * r, n * c)


def kernel(x, c, w_mod, b_mod, g_mix, w_in, hgrn_lb_logits, s5_a_re, s5_a_im, s5_b_re, s5_b_im, s5_c_re, s5_c_im, s5_log_dt, s5_d, s5_w_glu, s5_b_glu, lru_conv_w, lru_conv_b, lru_w_a, lru_b_a, lru_w_x, lru_b_x, lru_lambda, g_branch, w_out, g_ffn, peer_w_q, peer_sub_keys, peer_u, peer_v, g_final):
    bsz, seq, d = x.shape
    depth = w_in.shape[0]
    assert bsz == SUBLANES and seq % CHUNK_T == 0 and (seq * bsz) % TOKEN_TILE == 0

    x_tm = x.transpose(1, 0, 2)
    mod = _modulation(c, w_mod, b_mod)
    mod = mod.reshape(depth, bsz, N_MOD, d).transpose(0, 2, 1, 3)
    lb_all, ab_re, ab_im, bb_re, bb_im = _prepare_params(
        hgrn_lb_logits, s5_a_re, s5_a_im, s5_log_dt, s5_b_re, s5_b_im)
    tmat = jnp.asarray(_hgrn_exponent_matrix(CHUNK_T), BF16)
    row = lambda a: a.reshape(1, -1)

    for l in range(depth):
        bbd = jnp.concatenate([
            _block_diag(bb_re[l].reshape(S5_GROUP, S5_GROUPS, S5_STATE).transpose(1, 0, 2)),
            _block_diag(bb_im[l].reshape(S5_GROUP, S5_GROUPS, S5_STATE).transpose(1, 0, 2)),
        ], axis=1).astype(BF16)
        cre = _block_diag(s5_c_re[l].transpose(0, 2, 1)).astype(BF16)
        cim = _block_diag(s5_c_im[l].transpose(0, 2, 1)).astype(BF16)
        s5p = (ab_re[l], ab_im[l], bbd, cre, cim, row(s5_d[l]),
               s5_w_glu[l].astype(BF16), row(s5_b_glu[l]))
        lrup = (lru_conv_w[l], row(lru_conv_b[l]),
                _block_diag(lru_w_a[l]).astype(BF16), row(lru_b_a[l]),
                _block_diag(lru_w_x[l]).astype(BF16), row(lru_b_x[l]), row(lru_lambda[l]))

        y = _mixer(x_tm, mod[l], g_mix[l], w_in[l].astype(BF16), tmat, lb_all[l:l + 1],
                   s5p, lrup, row(g_branch[l]))
        x1, h2t, q = _out_projection(x_tm, y, mod[l], g_ffn[l],
                                    w_out[l].astype(BF16), peer_w_q[l].astype(BF16))
        tables = _retrieval(q, peer_sub_keys[l].astype(BF16))
        x_tm = _peer_dense(h2t, tables, peer_u[l].astype(BF16),
                           peer_v[l].astype(BF16).T, x1, mod[l])

    out = _final_norm(x_tm, g_final)
    return out.transpose(1, 0, 2)
```

```python
import functools

import numpy as np
import jax
import jax.numpy as jnp
from jax import lax
from jax.experimental import pallas as pl
from jax.experimental.pallas import tpu as pltpu

F32 = jnp.float32
BF16 = jnp.bfloat16

EPS = 1e-6
N_MOD = 6
HGRN_HEADS = 4
HGRN_DK = 128
D_HGRN = HGRN_HEADS * HGRN_DK
S5_GROUP = 16
S5_GROUPS = 16
S5_STATE = 64
D_S5 = S5_GROUP * S5_GROUPS
N_S5_STATE = S5_GROUPS * S5_STATE
LRU_BLOCKS = 4
LRU_BLOCK = 64
D_LRU = LRU_BLOCKS * LRU_BLOCK
CONV_W = 4
LRU_C = 8.0
PEER_HEADS = 8
N_KEYS = 128
PEER_DQ = 256
PEER_TOPK = 16

SUBLANES = 8
LANES = 128
VMEM_LIMIT_BYTES = 56 * 1024 * 1024

CHUNK_T = 64
HGRN_GROUP = 8
TOKEN_TILE = 512
ROW_PARTS = 2
EXPERT_TILE = 1024
PEER_TOKEN_TILE = 1024
PEER_PIECE = 256
RETR_TILE = 128
MOD_COL_TILE = 1536


def _sigmoid(x):
    return 1.0 / (1.0 + jnp.exp(-x))


def _silu(x):
    return x * _sigmoid(x)


def _gelu(x):
    return 0.5 * x * (1.0 + jnp.tanh(0.7978845608028654 * (x + 0.044715 * (x * x * x))))


def _dot(a, b):
    return jnp.dot(a, b, preferred_element_type=F32)


def _dot_nt(a, b):
    return lax.dot_general(a, b, (((1,), (1,)), ((), ())), preferred_element_type=F32)


def _dot_tn(a, b):
    return lax.dot_general(a, b, (((0,), (0,)), ((), ())), preferred_element_type=F32)


def _split2(x):
    hi = x.astype(BF16)
    lo = (x - hi.astype(F32)).astype(BF16)
    return hi, lo


def _params(semantics):
    return pltpu.CompilerParams(dimension_semantics=semantics, vmem_limit_bytes=VMEM_LIMIT_BYTES)


def _mod_body(c_ref, w_ref, b_ref, o_ref):
    cond = _silu(c_ref[...])
    c_hi, c_lo = _split2(cond)
    w_hi, w_lo = _split2(w_ref[0])
    o_ref[0] = _dot(c_hi, w_hi) + _dot(c_hi, w_lo) + _dot(c_lo, w_hi) + b_ref[0]


def _modulation(c, w_mod, b_mod):
    depth, d, n = w_mod.shape
    bsz = c.shape[0]
    tn = MOD_COL_TILE
    return pl.pallas_call(
        _mod_body,
        grid=(depth, n // tn),
        in_specs=[
            pl.BlockSpec((bsz, d), lambda l, j: (0, 0)),
            pl.BlockSpec((1, d, tn), lambda l, j: (l, 0, j)),
            pl.BlockSpec((1, 1, tn), lambda l, j: (l, 0, j)),
        ],
        out_specs=pl.BlockSpec((1, bsz, tn), lambda l, j: (l, 0, j)),
        out_shape=jax.ShapeDtypeStruct((depth, bsz, n), F32),
        compiler_params=_params(("arbitrary", "arbitrary")),
    )(c, w_mod, b_mod.reshape(depth, 1, n))


def _prep_body(lbl_ref, are_ref, aim_ref, ldt_ref, bre_ref, bim_ref,
               lb_ref, abr_ref, abi_ref, bbr_ref, bbi_ref):
    logits = lbl_ref[...]
    depth = logits.shape[0]
    rows = [logits[l:l + 1] for l in range(depth)]
    mx = functools.reduce(jnp.maximum, rows)
    ex = [jnp.exp(r - mx) for r in rows]
    den = functools.reduce(lambda a, b: a + b, ex)
    soft = [e / den for e in ex]
    run = soft[0]
    for l in range(depth):
        if l > 0:
            run = run + soft[l]
        lb_ref[l:l + 1, :] = run - soft[0]

    lam_re = jnp.minimum(are_ref[...], -1e-4)
    lam_im = aim_ref[...]
    dt = jnp.exp(ldt_ref[...])
    mag = jnp.exp(lam_re * dt)
    ab_re = mag * jnp.cos(lam_im * dt)
    ab_im = mag * jnp.sin(lam_im * dt)
    den2 = lam_re * lam_re + lam_im * lam_im
    nr = ab_re - 1.0
    z_re = (nr * lam_re + ab_im * lam_im) / den2
    z_im = (ab_im * lam_re - nr * lam_im) / den2
    abr_ref[...] = ab_re
    abi_ref[...] = ab_im
    br = bre_ref[...]
    bi = bim_ref[...]
    bbr_ref[...] = z_re * br - z_im * bi
    bbi_ref[...] = z_re * bi + z_im * br


def _prepare_params(lb_logits, a_re, a_im, log_dt, b_re, b_im):
    depth = lb_logits.shape[0]
    n = N_S5_STATE
    flat = lambda a: a.reshape(depth, 1, n)
    ldt = jnp.broadcast_to(log_dt[:, :, None], (depth, S5_GROUPS, S5_STATE))
    bt = lambda b: b.transpose(0, 3, 1, 2).reshape(depth, S5_GROUP, n)
    out_shape = (
        jax.ShapeDtypeStruct((depth, D_HGRN), F32),
        jax.ShapeDtypeStruct((depth, 1, n), F32),
        jax.ShapeDtypeStruct((depth, 1, n), F32),
        jax.ShapeDtypeStruct((depth, S5_GROUP, n), F32),
        jax.ShapeDtypeStruct((depth, S5_GROUP, n), F32),
    )
    return pl.pallas_call(_prep_body, out_shape=out_shape)(
        lb_logits, flat(a_re), flat(a_im), flat(ldt), bt(b_re), bt(b_im))


def _project_inputs(x_ref, mod_ref, g_ref, w_ref, pa_ref, pg_ref, pb_ref, pc_ref):
    tl, bsz, d = x_ref.shape
    na = pa_ref.shape[0] * LANES
    ng = pg_ref.shape[1]
    nb = pb_ref.shape[1]
    part = tl // ROW_PARTS
    for s in range(ROW_PARTS):
        x = x_ref[s * part:(s + 1) * part]
        ms = jnp.mean(x * x, axis=-1, keepdims=True)
        xn = x * lax.rsqrt(ms + EPS) * g_ref[...]
        h = xn * (1.0 + mod_ref[1]) + mod_ref[0]
        h = h.reshape(part * bsz, d).astype(BF16)
        rows = slice(s * part * bsz, (s + 1) * part * bsz)
        pa = _dot(h, w_ref[:, 0:na])
        for cg in range(pa_ref.shape[0]):
            pa_ref[cg, rows, :] = pa[:, cg * LANES:(cg + 1) * LANES]
        pg_ref[rows, :] = _dot(h, w_ref[:, na:na + ng])
        pb_ref[rows, :] = _dot(h, w_ref[:, na + ng:na + ng + nb])
        pc_ref[rows, :] = _dot(h, w_ref[:, na + ng + nb:])


def _hgrn_exponent_matrix(chunk):
    n_levels = int(np.log2(chunk))
    mat = np.zeros((n_levels + 2, chunk, chunk), np.float32)
    u = np.arange(chunk)
    for k in range(n_levels):
        g = chunk >> (k + 1)
        m = 2 * g
        for t in range(chunk):
            r = (t // m) * m + g - 1
            if t % m >= g:
                mat[k, t] = (u > r) & (u <= t)
            else:
                mat[k, t] = (u > t) & (u <= r)
    for t in range(chunk):
        mat[n_levels, t] = u <= t
        mat[n_levels + 1, t] = u > t
    return mat.reshape((n_levels + 2) * chunk, chunk)


def _mixer_body(x_ref, mod_ref, g_ref, w_ref, tmat_ref, lb_ref,
                ar_ref, ai_ref, bbd_ref, cre_ref, cim_ref, dsk_ref, wglu_ref, bglu_ref,
                cw_ref, cb_ref, wa_ref, ba_ref, wx_ref, bx_ref, lam_ref, gbr_ref,
                y_ref,
                pa_ref, pg_ref, pb_ref, pc_ref,
                st_ref, o_scr, bu_scr, xs_scr, s5_state, lru_state, xext, la_scr, lb_scr, hs_scr):
    _project_inputs(x_ref, mod_ref, g_ref, w_ref, pa_ref, pg_ref, pb_ref, pc_ref)
    rows = pa_ref.shape[1]
    bsz = s5_state.shape[0]
    chunk = rows // bsz
    n_levels = tmat_ref.shape[0] // chunk - 2
    dk = HGRN_DK

    @pl.when(pl.program_id(0) == 0)
    def _():
        st_ref[...] = jnp.zeros_like(st_ref)
        s5_state[...] = jnp.zeros_like(s5_state)
        lru_state[...] = jnp.zeros_like(lru_state)
        xext[0:(CONV_W - 1) * bsz, :] = jnp.zeros(((CONV_W - 1) * bsz, D_LRU), F32)

    ti = lax.broadcasted_iota(jnp.int32, (chunk, chunk), 0)
    si = lax.broadcasted_iota(jnp.int32, (chunk, chunk), 1)
    masks = []
    for k in range(n_levels):
        g = chunk >> (k + 1)
        m = 2 * g
        same = (ti // m) == (si // m)
        masks.append(jnp.where(same, jnp.where((ti % m) >= g, jnp.where((si % m) < g, 1.0, 0.0), 0.0), 0.0))
    diag = jnp.where(ti == si, 1.0, 0.0)
    lb = lb_ref[...]
    tmat = tmat_ref[...]

    def hgrn_group(g, carry):
        grp = range(HGRN_GROUP)
        bs = [g * HGRN_GROUP + e for e in grp]
        rws = [pl.ds(b, chunk, stride=bsz) for b in bs]
        heads = lambda e, first: jnp.concatenate(
            [pa_ref[first + h, rws[e], :] for h in range(HGRN_HEADS)], axis=1)
        qa = [_silu(heads(e, 0)) for e in grp]
        f = [lb + (1.0 - lb) * _sigmoid(heads(e, HGRN_HEADS)) for e in grp]
        v16 = [heads(e, 2 * HGRN_HEADS).astype(BF16) for e in grp]
        kk = [1.0 - f[e] for e in grp]
        parts = [_split2(jnp.log(f[e])) for e in grp]
        decay = [jnp.exp(_dot(tmat, parts[e][0]) + _dot(tmat, parts[e][1])) for e in grp]
        cum = [decay[e][n_levels * chunk:(n_levels + 1) * chunk] for e in grp]
        qb16 = [(qa[e] * cum[e]).astype(BF16) for e in grp]
        ks16 = [(kk[e] * decay[e][(n_levels + 1) * chunk:]).astype(BF16) for e in grp]
        total = [cum[e][chunk - 1:chunk, :] for e in grp]
        qa16 = [qa[e].astype(BF16) for e in grp]
        kk16 = [kk[e].astype(BF16) for e in grp]
        for h in range(HGRN_HEADS):
            sl = slice(h * dk, (h + 1) * dk)
            scores = [diag * _dot_nt(qa16[e][:, sl], kk16[e][:, sl]) for e in grp]
            for k in range(n_levels):
                for e in grp:
                    dl = decay[e][k * chunk:(k + 1) * chunk, sl]
                    ql = (qa[e][:, sl] * dl).astype(BF16)
                    kl = (kk[e][:, sl] * dl).astype(BF16)
                    scores[e] = scores[e] + masks[k] * _dot_nt(ql, kl)
            for e in grp:
                st = st_ref[bs[e], h]
                o = (_dot(scores[e].astype(BF16), v16[e][:, sl])
                     + _dot_nt(qb16[e][:, sl], st.astype(BF16)))
                o_scr[h, rws[e], :] = o
                st_ref[bs[e], h] = st * total[e][:, sl] + _dot_tn(v16[e][:, sl], ks16[e][:, sl])
        return carry

    lax.fori_loop(0, bsz // HGRN_GROUP, hgrn_group, 0)

    gate = _silu(pg_ref[...])
    ya = []
    ssq = None
    for h in range(HGRN_HEADS):
        sl = slice(h * dk, (h + 1) * dk)
        oh = o_scr[h]
        oh = oh * lax.rsqrt(jnp.mean(oh * oh, axis=-1, keepdims=True) + EPS) * gate[:, sl]
        ya.append(oh)
        s = jnp.sum(oh * oh, axis=-1, keepdims=True)
        ssq = s if ssq is None else ssq + s
    scale = lax.rsqrt(ssq * (1.0 / D_HGRN) + EPS)
    for h in range(HGRN_HEADS):
        sl = slice(h * dk, (h + 1) * dk)
        y_ref[:, sl] = (ya[h] * scale * gbr_ref[:, sl]).astype(y_ref.dtype)

    n = N_S5_STATE
    u = pb_ref[...]
    bu_scr[...] = _dot(u.astype(BF16), bbd_ref[...])
    ar = jnp.broadcast_to(ar_ref[...], (bsz, n))
    ai = jnp.broadcast_to(ai_ref[...], (bsz, n))

    tail = (CONV_W - 1) * bsz
    xin = pc_ref[:, 0:D_LRU]
    xext[tail:tail + rows, :] = xin
    xc = cb_ref[...]
    for w in range(CONV_W):
        xc = xc + cw_ref[w:w + 1, :] * xext[w * bsz:w * bsz + rows, :]
    xext[0:tail, :] = xin[rows - tail:rows, :]
    xc16 = xc.astype(BF16)
    rg = _sigmoid(_dot(xc16, wa_ref[...]) + ba_ref[...])
    ig = _sigmoid(_dot(xc16, wx_ref[...]) + bx_ref[...])
    nl = -lam_ref[...]
    softplus = jnp.maximum(nl, 0.0) + jnp.log(1.0 + jnp.exp(-jnp.abs(nl)))
    log_a = (-LRU_C) * rg * softplus
    la_scr[...] = jnp.exp(log_a)
    lb_scr[...] = jnp.sqrt(1.0 - jnp.exp(2.0 * log_a)) * (ig * xc)

    def scan_step(t, st):
        xr, xi, hstate = st
        r = pl.ds(pl.multiple_of(t * bsz, bsz), bsz)
        nxr = ar * xr - ai * xi + bu_scr[r, 0:n]
        nxi = ar * xi + ai * xr + bu_scr[r, n:2 * n]
        xs_scr[r, 0:n] = nxr
        xs_scr[r, n:2 * n] = nxi
        hn = la_scr[r, :] * hstate + lb_scr[r, :]
        hs_scr[r, :] = hn
        return nxr, nxi, hn

    xr, xi, hlast = lax.fori_loop(
        0, chunk, scan_step, (s5_state[:, 0:n], s5_state[:, n:2 * n], lru_state[...]), unroll=4)
    s5_state[:, 0:n] = xr
    s5_state[:, n:2 * n] = xi
    lru_state[...] = hlast

    yb = (_dot(xs_scr[:, 0:n].astype(BF16), cre_ref[...])
          - _dot(xs_scr[:, n:2 * n].astype(BF16), cim_ref[...]))
    yb = yb + dsk_ref[...] * u
    z = _gelu(yb)
    yb = z * _sigmoid(_dot(z.astype(BF16), wglu_ref[...]) + bglu_ref[...])
    yb = yb * lax.rsqrt(jnp.mean(yb * yb, axis=-1, keepdims=True) + EPS)
    y_ref[:, D_HGRN:D_HGRN + D_S5] = (yb * gbr_ref[:, D_HGRN:D_HGRN + D_S5]).astype(y_ref.dtype)

    yc = hs_scr[...] * _gelu(pc_ref[:, D_LRU:2 * D_LRU])
    yc = yc * lax.rsqrt(jnp.mean(yc * yc, axis=-1, keepdims=True) + EPS)
    y_ref[:, D_HGRN + D_S5:] = (yc * gbr_ref[:, D_HGRN + D_S5:]).astype(y_ref.dtype)


def _mixer(x_tm, mod_l, g_mix, w_in16, tmat, lb, s5p, lrup, g_branch):
    seq, bsz, d = x_tm.shape
    rows = CHUNK_T * bsz
    d_mix = D_HGRN + D_S5 + D_LRU
    n = N_S5_STATE
    full = lambda a: pl.BlockSpec(a.shape, lambda i, nd=a.ndim: (0,) * nd)
    consts = [mod_l, g_mix.reshape(1, d), w_in16, tmat, lb, *s5p, *lrup, g_branch]
    return pl.pallas_call(
        _mixer_body,
        grid=(seq // CHUNK_T,),
        in_specs=[pl.BlockSpec((CHUNK_T, bsz, d), lambda i: (i, 0, 0))] + [full(a) for a in consts],
        out_specs=pl.BlockSpec((rows, d_mix), lambda i: (i, 0)),
        out_shape=jax.ShapeDtypeStruct((seq * bsz, d_mix), BF16),
        scratch_shapes=[
            pltpu.VMEM((3 * D_HGRN // LANES, rows, LANES), F32),
            pltpu.VMEM((rows, D_HGRN), F32),
            pltpu.VMEM((rows, D_S5), F32),
            pltpu.VMEM((rows, 2 * D_LRU), F32),
            pltpu.VMEM((bsz, HGRN_HEADS, HGRN_DK, HGRN_DK), F32),
            pltpu.VMEM((HGRN_HEADS, rows, HGRN_DK), F32),
            pltpu.VMEM((rows, 2 * n), F32),
            pltpu.VMEM((rows, 2 * n), F32),
            pltpu.VMEM((bsz, 2 * n), F32),
            pltpu.VMEM((bsz, D_LRU), F32),
            pltpu.VMEM(((CONV_W - 1) * bsz + rows, D_LRU), F32),
            pltpu.VMEM((rows, D_LRU), F32),
            pltpu.VMEM((rows, D_LRU), F32),
            pltpu.VMEM((rows, D_LRU), F32),
        ],
        compiler_params=_params(("arbitrary",)),
    )(x_tm, *consts)


def _outproj_body(x_ref, y_ref, mod_ref, g_ref, wo_ref, wq_ref, x1_ref, h2t_ref, q_ref):
    tl, bsz, d = x_ref.shape
    part = tl // ROW_PARTS
    parts = range(ROW_PARTS)
    rows = [slice(s * part * bsz, (s + 1) * part * bsz) for s in parts]
    o = [_dot(y_ref[rows[s], :], wo_ref[...]).reshape(part, bsz, d) for s in parts]
    x1 = [x_ref[s * part:(s + 1) * part] + mod_ref[2] * o[s] for s in parts]
    h2 = []
    for s in parts:
        x1_ref[s * part:(s + 1) * part] = x1[s]
        ms = jnp.mean(x1[s] * x1[s], axis=-1, keepdims=True)
        hs = x1[s] * lax.rsqrt(ms + EPS) * g_ref[...]
        hs = hs * (1.0 + mod_ref[4]) + mod_ref[3]
        h2.append(hs.reshape(part * bsz, d))
    for s in parts:
        q_ref[rows[s], :] = _dot(h2[s].astype(BF16), wq_ref[...]).astype(q_ref.dtype)
        h2t_ref[:, rows[s]] = h2[s].T.astype(BF16)


def _out_projection(x_tm, y, mod_l, g_ffn, w_out16, w_q16):
    seq, bsz, d = x_tm.shape
    tl = TOKEN_TILE // bsz
    tm = tl * bsz
    rows = seq * bsz
    nq = w_q16.shape[1]
    return pl.pallas_call(
        _outproj_body,
        grid=(seq // tl,),
        in_specs=[
            pl.BlockSpec((tl, bsz, d), lambda i: (i, 0, 0)),
            pl.BlockSpec((tm, y.shape[1]), lambda i: (i, 0)),
            pl.BlockSpec((N_MOD, bsz, d), lambda i: (0, 0, 0)),
            pl.BlockSpec((1, d), lambda i: (0, 0)),
            pl.BlockSpec(w_out16.shape, lambda i: (0, 0)),
            pl.BlockSpec(w_q16.shape, lambda i: (0, 0)),
        ],
        out_specs=[
            pl.BlockSpec((tl, bsz, d), lambda i: (i, 0, 0)),
            pl.BlockSpec((d, tm), lambda i: (0, i)),
            pl.BlockSpec((tm, nq), lambda i: (i, 0)),
        ],
        out_shape=(
            jax.ShapeDtypeStruct((seq, bsz, d), F32),
            jax.ShapeDtypeStruct((d, rows), BF16),
            jax.ShapeDtypeStruct((rows, nq), BF16),
        ),
        compiler_params=_params(("arbitrary",)),
    )(x_tm, y, mod_l, g_ffn.reshape(1, d), w_out16, w_q16)


def _sort_network(n):
    ops = []
    p = 1
    while p < n:
        k = p
        while k >= 1:
            for j in range(k % p, n - k, 2 * k):
                for i in range(min(k, n - j - k)):
                    if (i + j) // (2 * p) == (i + j + k) // (2 * p):
                        ops.append((i + j, i + j + k, True))
            k //= 2
        p *= 2
    return ops


def _bitonic_merge_network(n):
    ops = []
    j = n // 2
    while j >= 1:
        for i in range(n):
            l = i ^ j
            if l > i:
                ops.append((i, l, True))
        j //= 2
    return ops


def _apply_network(vals, ops):
    vals = list(vals)
    for i, j, first_max in ops:
        hi = jnp.maximum(vals[i], vals[j])
        lo = jnp.minimum(vals[i], vals[j])
        vals[i], vals[j] = (hi, lo) if first_max else (lo, hi)
    return vals


def _top_sorted(s, k):
    n_keys = s.shape[0]
    n_slabs = n_keys // SUBLANES
    assert n_slabs == k
    slabs = [s[i * SUBLANES:(i + 1) * SUBLANES, :] for i in range(n_slabs)]
    slabs = _apply_network(slabs, _sort_network(k))
    merge = _bitonic_merge_network(k)
    shift = SUBLANES // 2
    while shift >= 1:
        rolled = [pltpu.roll(v, shift, 0) for v in slabs]
        slabs = [jnp.maximum(slabs[i], rolled[k - 1 - i]) for i in range(k)]
        slabs = _apply_network(slabs, merge)
        shift //= 2
    return slabs


def _staircase(k):
    return [(x, y) for x in range(k) for y in range(k) if (x + 1) * (y + 1) <= k]


def _select_tree(masks, leaves):
    if not masks:
        return leaves[0]
    half = len(leaves) // 2
    return jnp.where(masks[0], _select_tree(masks[1:], leaves[half:]), _select_tree(masks[1:], leaves[:half]))


def _prefix_count(pred, vals):
    n = len(vals)
    full = pred(vals[n - 1])
    masks = []
    count = None
    stride = n // 2
    while stride >= 1:
        pivot = _select_tree(masks, [vals[base + stride - 1] for base in range(0, n, 2 * stride)])
        hit = pred(pivot)
        masks.append(hit)
        term = jnp.where(hit, float(stride), 0.0)
        count = term if count is None else count + term
        stride //= 2
    return jnp.where(full, float(n), count)


def _retrieval_body(q_ref, keys_ref, r1_ref, e1_ref, n0_ref, c0_ref, s_scr, a_scr, b_scr,
                    afull_scr, bfull_scr):
    k = PEER_TOPK
    half = PEER_DQ // 2
    tok = q_ref.shape[0]
    neg = jnp.full((SUBLANES, tok), -jnp.inf, F32)

    for h in range(PEER_HEADS):
        for p in range(2):
            off = h * PEER_DQ + p * half
            s = _dot_nt(keys_ref[p], q_ref[:, off:off + half])
            s_scr[h, p] = s
            top = _top_sorted(s, k)
            dst, dst_full = (a_scr, afull_scr) if p == 0 else (b_scr, bfull_scr)
            for x in range(k):
                dst[x, h:h + 1, :] = top[x][0:1, :]
                dst_full[h, x] = top[x]

    a = [a_scr[x] for x in range(k)]
    b = [b_scr[y] for y in range(k)]
    pairs = _staircase(k)
    cands = [a[x] + b[y] for x, y in pairs]
    n_sort = 1
    while n_sort < len(cands):
        n_sort *= 2
    ranked = _apply_network(cands + [neg] * (n_sort - len(cands)), _sort_network(n_sort))
    tau = ranked[k - 1]
    best = a[0] + b[0]
    z = jnp.zeros_like(tau)
    for cnd in cands:
        z = z + jnp.where(cnd >= tau, jnp.exp(cnd - best), 0.0)
    inv_z = 1.0 / z

    pack = 2 * SUBLANES
    for h in range(PEER_HEADS):
        tau_h = jnp.broadcast_to(tau[h:h + 1, :], (SUBLANES, tok))
        inv_z_h = jnp.broadcast_to(inv_z[h:h + 1, :], (SUBLANES, tok))
        b_h = [bfull_scr[h, y] for y in range(k)]
        a_first = afull_scr[h, 0]
        a_last = afull_scr[h, k - 1]
        for blk in range(N_KEYS // pack):
            ranks, exps = [], []
            for sub in range(pack // SUBLANES):
                rows = slice(blk * pack + sub * SUBLANES, blk * pack + (sub + 1) * SUBLANES)
                s0 = s_scr[h, 0, rows, :]
                s1 = s_scr[h, 1, rows, :]
                ranks.append(_prefix_count(lambda v: v > s1, b_h))
                cnt = _prefix_count(lambda v: s0 + v >= tau_h, b_h)
                n0_ref[h, rows, :] = jnp.where(s0 >= a_last, cnt, 0.0)
                c0_ref[h, rows, :] = jnp.exp(s0 - a_first) * inv_z_h
                exps.append(jnp.exp(s1 - b_h[0]))
            prow = slice(blk * pack, (blk + 1) * pack)
            r1_ref[h, prow, :] = jnp.concatenate(ranks, axis=0).astype(r1_ref.dtype)
            e1_ref[h, prow, :] = jnp.concatenate(exps, axis=0).astype(e1_ref.dtype)


def _retrieval(q, keys16):
    rows = q.shape[0]
    tok = RETR_TILE
    tab = lambda dt: jax.ShapeDtypeStruct((PEER_HEADS, N_KEYS, rows), dt)
    tspec = pl.BlockSpec((PEER_HEADS, N_KEYS, tok), lambda i: (0, 0, i))
    return pl.pallas_call(
        _retrieval_body,
        grid=(rows // tok,),
        in_specs=[
            pl.BlockSpec((tok, q.shape[1]), lambda i: (i, 0)),
            pl.BlockSpec(keys16.shape, lambda i: (0, 0, 0)),
        ],
        out_specs=[tspec, tspec, tspec, tspec],
        out_shape=(tab(BF16), tab(BF16), tab(F32), tab(F32)),
        scratch_shapes=[
            pltpu.VMEM((PEER_HEADS, 2, N_KEYS, tok), F32),
            pltpu.VMEM((PEER_TOPK, SUBLANES, tok), F32),
            pltpu.VMEM((PEER_TOPK, SUBLANES, tok), F32),
            pltpu.VMEM((PEER_HEADS, PEER_TOPK, SUBLANES, tok), F32),
            pltpu.VMEM((PEER_HEADS, PEER_TOPK, SUBLANES, tok), F32),
        ],
        compiler_params=_params(("arbitrary",)),
    )(q, keys16)


def _peer_body(h2t_ref, r1_ref, e1_ref, n0_ref, c0_ref, u_ref, vt_ref, x1_ref, mod_ref,
               x2_ref, acc_scr, a_scr, p_scr):
    j = pl.program_id(1)
    n_j = pl.num_programs(1)
    te = u_ref.shape[0]
    tok = h2t_ref.shape[1]
    rows_per_step = te // N_KEYS
    pack = 2 * SUBLANES

    @pl.when(j == 0)
    def _():
        acc_scr[...] = jnp.zeros_like(acc_scr)
        a_scr[...] = _dot(u_ref[...], h2t_ref[...]).astype(BF16)

    @pl.when(j > 0)
    def _():
        zero = jnp.zeros((pack, PEER_PIECE), BF16)
        for c in range(tok // PEER_PIECE):
            cols = slice(c * PEER_PIECE, (c + 1) * PEER_PIECE)
            for r in range(rows_per_step):
                gates = [zero for _ in range(N_KEYS // pack)]
                for h in range(PEER_HEADS):
                    nb = jnp.broadcast_to(n0_ref[h, r:r + 1, cols], (pack, PEER_PIECE)).astype(BF16)
                    cb = jnp.broadcast_to(c0_ref[h, r:r + 1, cols], (pack, PEER_PIECE)).astype(BF16)
                    for k in range(N_KEYS // pack):
                        rs = slice(k * pack, (k + 1) * pack)
                        sel = jnp.where(r1_ref[h, rs, cols] < nb, e1_ref[h, rs, cols], zero)
                        gates[k] = gates[k] + sel * cb
                for k in range(N_KEYS // pack):
                    rs = slice(r * N_KEYS + k * pack, r * N_KEYS + (k + 1) * pack)
                    p_scr[rs, cols] = gates[k] * _gelu(a_scr[rs, cols])
            acc_scr[:, cols] += _dot(vt_ref[...], p_scr[:, cols])
            a_scr[:, cols] = _dot(u_ref[...], h2t_ref[:, cols]).astype(BF16)

    @pl.when(j == n_j - 1)
    def _():
        tl, bsz, d = x1_ref.shape
        out = acc_scr[...].T.reshape(tl, bsz, d)
        x2_ref[...] = x1_ref[...] + mod_ref[5] * out


def _peer_dense(h2t, tables, u16, vt16, x1, mod_l):
    seq, bsz, d = x1.shape
    tok = PEER_TOKEN_TILE
    tl = tok // bsz
    te = EXPERT_TILE
    n_tiles = u16.shape[0] // te
    rows_per_step = te // N_KEYS
    cur = lambda j: jnp.minimum(j, n_tiles - 1)
    prev = lambda j: jnp.maximum(j - 1, 0)
    tspec = pl.BlockSpec((PEER_HEADS, N_KEYS, tok), lambda i, j: (0, 0, i))
    rspec = pl.BlockSpec((PEER_HEADS, rows_per_step, tok), lambda i, j: (0, prev(j), i))
    return pl.pallas_call(
        _peer_body,
        grid=(seq // tl, n_tiles + 1),
        in_specs=[
            pl.BlockSpec((d, tok), lambda i, j: (0, i)),
            tspec, tspec, rspec, rspec,
            pl.BlockSpec((te, d), lambda i, j: (cur(j), 0)),
            pl.BlockSpec((d, te), lambda i, j: (0, prev(j))),
            pl.BlockSpec((tl, bsz, d), lambda i, j: (i, 0, 0)),
            pl.BlockSpec((N_MOD, bsz, d), lambda i, j: (0, 0, 0)),
        ],
        out_specs=pl.BlockSpec((tl, bsz, d), lambda i, j: (i, 0, 0)),
        out_shape=jax.ShapeDtypeStruct((seq, bsz, d), F32),
        scratch_shapes=[
            pltpu.VMEM((d, tok), F32),
            pltpu.VMEM((te, tok), BF16),
            pltpu.VMEM((te, tok), BF16),
        ],
        compiler_params=_params(("arbitrary", "arbitrary")),
    )(h2t, *tables, u16, vt16, x1, mod_l)


def _final_body(x_ref, g_ref, o_ref):
    x = x_ref[...]
    o_ref[...] = x * lax.rsqrt(jnp.mean(x * x, axis=-1, keepdims=True) + EPS) * g_ref[...]


def _final_norm(x_tm, g):
    seq, bsz, d = x_tm.shape
    tl = TOKEN_TILE // bsz
    return pl.pallas_call(
        _final_body,
        grid=(seq // tl,),
        in_specs=[pl.BlockSpec((tl, bsz, d), lambda i: (i, 0, 0)),
                  pl.BlockSpec((1, d), lambda i: (0, 0))],
        out_specs=pl.BlockSpec((tl, bsz, d), lambda i: (i, 0, 0)),
        out_shape=jax.ShapeDtypeStruct((seq, bsz, d), F32),
        compiler_params=_params(("arbitrary",)),
    )(x_tm, g.reshape(1, d))


def _block_diag(blocks):
    n, r, c = blocks.shape
    eye = jnp.eye(n, dtype=blocks.dtype)
    return (blocks[:, :, None, :] * eye[:, None, :, None]).reshape(n * r, n * c)


def kernel(x, c, w_mod, b_mod, g_mix, w_in, hgrn_lb_logits, s5_a_re, s5_a_im, s5_b_re, s5_b_im, s5_c_re, s5_c_im, s5_log_dt, s5_d, s5_w_glu, s5_b_glu, lru_conv_w, lru_conv_b, lru_w_a, lru_b_a, lru_w_x, lru_b_x, lru_lambda, g_branch, w_out, g_ffn, peer_w_q, peer_sub_keys, peer_u, peer_v, g_final):
    bsz, seq, d = x.shape
    depth = w_in.shape[0]
    assert bsz == SUBLANES and seq % CHUNK_T == 0 and (seq * bsz) % TOKEN_TILE == 0

    x_tm = x.transpose(1, 0, 2)
    mod = _modulation(c, w_mod, b_mod)
    mod = mod.reshape(depth, bsz, N_MOD, d).transpose(0, 2, 1, 3)
    lb_all, ab_re, ab_im, bb_re, bb_im = _prepare_params(
        hgrn_lb_logits, s5_a_re, s5_a_im, s5_log_dt, s5_b_re, s5_b_im)
    tmat = jnp.asarray(_hgrn_exponent_matrix(CHUNK_T), BF16)
    row = lambda a: a.reshape(1, -1)

    for l in range(depth):
        bbd = jnp.concatenate([
            _block_diag(bb_re[l].reshape(S5_GROUP, S5_GROUPS, S5_STATE).transpose(1, 0, 2)),
            _block_diag(bb_im[l].reshape(S5_GROUP, S5_GROUPS, S5_STATE).transpose(1, 0, 2)),
        ], axis=1).astype(BF16)
        cre = _block_diag(s5_c_re[l].transpose(0, 2, 1)).astype(BF16)
        cim = _block_diag(s5_c_im[l].transpose(0, 2, 1)).astype(BF16)
        s5p = (ab_re[l], ab_im[l], bbd, cre, cim, row(s5_d[l]),
               s5_w_glu[l].astype(BF16), row(s5_b_glu[l]))
        lrup = (lru_conv_w[l], row(lru_conv_b[l]),
                _block_diag(lru_w_a[l]).astype(BF16), row(lru_b_a[l]),
                _block_diag(lru_w_x[l]).astype(BF16), row(lru_b_x[l]), row(lru_lambda[l]))

        y = _mixer(x_tm, mod[l], g_mix[l], w_in[l].astype(BF16), tmat, lb_all[l:l + 1],
                   s5p, lrup, row(g_branch[l]))
        x1, h2t, q = _out_projection(x_tm, y, mod[l], g_ffn[l],
                                    w_out[l].astype(BF16), peer_w_q[l].astype(BF16))
        tables = _retrieval(q, peer_sub_keys[l].astype(BF16))
        x_tm = _peer_dense(h2t, tables, peer_u[l].astype(BF16),
                           peer_v[l].astype(BF16).T, x1, mod[l])

    out = _final_norm(x_tm, g_final)
    return out.transpose(1, 0, 2)
```
